```python
import math
import jax
import jax.numpy as jnp
from jax import lax
import numpy as np

D_MODEL = 1024
BATCH = 4
SEQ = 4096
DEPTH = 2

GRID_W = 64
CTX_LEN = 256

S5_WIDTH = 256
S5_GROUP = 16
S5_GROUPS = S5_WIDTH // S5_GROUP
S5_STATE = 64
S5_DT_MIN = 1e-3
S5_DT_MAX = 1e-1

GM_WIDTH = 256
GM_HEADS = 4
GM_CHUNK = 128

DA_HEADS = 4
DA_HEAD_DIM = 64
DA_QK = DA_HEADS * 2 * DA_HEAD_DIM
DA_V = DA_HEADS * 2 * DA_HEAD_DIM
Q_BLOCK = 128
ROPE_BASE = 10000.0

N_BRANCH = 3
IN_SPLITS = (S5_WIDTH,
             S5_WIDTH + 2 * GM_WIDTH,
             S5_WIDTH + 2 * GM_WIDTH + DA_QK,
             S5_WIDTH + 2 * GM_WIDTH + 2 * DA_QK,
             S5_WIDTH + 2 * GM_WIDTH + 2 * DA_QK + DA_V)
IN_WIDTH = IN_SPLITS[-1] + N_BRANCH * D_MODEL

N_EXPERTS = 64
TOP_K = 8
N_GROUPS = 8
TOPK_GROUPS = 4
EXPERT_HIDDEN = 256
SHARED_HIDDEN = 256
ROUTED_SCALE = 2.5
MOE_BLOCK = 128

DN_ALPHA = (2 * DEPTH) ** 0.25
DN_BETA = (8 * DEPTH) ** -0.25
LN_EPS = 1e-5

kernel_name = 'hybrid_s5_gmlp_diffattn_moe_dit'


def layer_norm(x, g=None, b=None):
    xf = x.astype(jnp.float32)
    mu = jnp.mean(xf, -1, keepdims=True)
    var = jnp.mean(jnp.square(xf - mu), -1, keepdims=True)
    y = ((xf - mu) * lax.rsqrt(var + LN_EPS)).astype(x.dtype)
    return y if g is None else y * g + b


def rms_norm(x, g):
    xf = x.astype(jnp.float32)
    y = xf * lax.rsqrt(jnp.mean(jnp.square(xf), -1, keepdims=True) + LN_EPS)
    return y.astype(x.dtype) * g


def modulate(x, shift, scale):
    return x * (1 + scale) + shift


def axial_rope_tables(n_tokens):
    rows = n_tokens // GRID_W
    r, col = jnp.meshgrid(jnp.arange(rows), jnp.arange(GRID_W), indexing='ij')
    axis_dim = DA_HEAD_DIM // 2
    inv_freq = ROPE_BASE ** (-jnp.arange(0, axis_dim, 2, dtype=jnp.float32) / axis_dim)
    ang = jnp.concatenate([r.reshape(-1, 1).astype(jnp.float32) * inv_freq,
                           col.reshape(-1, 1).astype(jnp.float32) * inv_freq], -1)
    return jnp.cos(ang), jnp.sin(ang)


def _rotate(x, cos, sin):
    h = x.shape[-1] // 2
    x1, x2 = x[..., :h], x[..., h:]
    return jnp.concatenate([x1 * cos - x2 * sin, x2 * cos + x1 * sin], -1)


def apply_axial_rope(x, cos, sin):
    bshape = (1, x.shape[1]) + (1,) * (x.ndim - 3) + (cos.shape[-1],)
    cos = cos.reshape(bshape).astype(x.dtype)
    sin = sin.reshape(bshape).astype(x.dtype)
    half = DA_HEAD_DIM // 2
    nf = half // 2
    return jnp.concatenate([_rotate(x[..., :half], cos[..., :nf], sin[..., :nf]),
                            _rotate(x[..., half:], cos[..., nf:], sin[..., nf:])], -1)


def s5_discretize(lam_re, lam_im, log_step, b_re, b_im):
    dt = jnp.exp(log_step.astype(jnp.float32))[:, None]
    lr = lam_re.astype(jnp.float32)
    li = lam_im.astype(jnp.float32)
    mag = jnp.exp(lr * dt)
    a_re = mag * jnp.cos(li * dt)
    a_im = mag * jnp.sin(li * dt)
    den = lr * lr + li * li
    n_re = a_re - 1.0
    z_re = (n_re * lr + a_im * li) / den
    z_im = (a_im * lr - n_re * li) / den
    br = b_re.astype(jnp.float32)
    bi = b_im.astype(jnp.float32)
    bb_re = z_re[..., None] * br - z_im[..., None] * bi
    bb_im = z_re[..., None] * bi + z_im[..., None] * br
    return a_re, a_im, bb_re, bb_im


def _complex_affine_combine(e1, e2):
    a1r, a1i, b1r, b1i = e1
    a2r, a2i, b2r, b2i = e2
    return (a1r * a2r - a1i * a2i,
            a1r * a2i + a1i * a2r,
            a2r * b1r - a2i * b1i + b2r,
            a2r * b1i + a2i * b1r + b2i)


def s5_scan(u, a_re, a_im, bb_re, bb_im, h0, reverse):
    bu_re = jnp.einsum('blgc,gpc->blgp', u, bb_re)
    bu_im = jnp.einsum('blgc,gpc->blgp', u, bb_im)
    if h0 is not None:
        h_re, h_im = h0
        edge = -1 if reverse else 0
        bu_re = bu_re.at[:, edge].add(a_re * h_re - a_im * h_im)
        bu_im = bu_im.at[:, edge].add(a_re * h_im + a_im * h_re)
    a_re_t = jnp.broadcast_to(a_re, bu_re.shape)
    a_im_t = jnp.broadcast_to(a_im, bu_re.shape)
    _, _, s_re, s_im = lax.associative_scan(_complex_affine_combine, (a_re_t, a_im_t, bu_re, bu_im),
                                            reverse=reverse, axis=1)
    return s_re, s_im


def s5_readout(s_re, s_im, c_re, c_im):
    return jnp.einsum('blgp,gcp->blgc', s_re, c_re) - jnp.einsum('blgp,gcp->blgc', s_im, c_im)


def s5_branch(a_lat, a_ctx, lam_re, lam_im, log_step, b_re, b_im, c_re, c_im, d_skip, ctx_out):
    def grouped(a):
        return a.astype(jnp.float32).reshape(a.shape[:2] + (S5_GROUPS, S5_GROUP))
    u_lat, u_ctx = grouped(a_lat), grouped(a_ctx)
    d = d_skip.astype(jnp.float32).reshape(S5_GROUPS, S5_GROUP)
    y_lat = d * u_lat
    y_ctx = d * u_ctx
    for direction in range(2):
        reverse = direction == 1
        a_re, a_im, bb_re, bb_im = s5_discretize(lam_re[direction], lam_im[direction],
                                                 log_step[direction], b_re[direction], b_im[direction])
        cr = c_re[direction].astype(jnp.float32)
        ci = c_im[direction].astype(jnp.float32)
        sc_re, sc_im = s5_scan(u_ctx, a_re, a_im, bb_re, bb_im, None, reverse)
        edge = 0 if reverse else -1
        sl_re, sl_im = s5_scan(u_lat, a_re, a_im, bb_re, bb_im, (sc_re[:, edge], sc_im[:, edge]), reverse)
        y_lat = y_lat + s5_readout(sl_re, sl_im, cr, ci)
        if ctx_out:
            y_ctx = y_ctx + s5_readout(sc_re, sc_im, cr, ci)

    def finish(y, like):
        return jax.nn.gelu(y).reshape(like.shape).astype(like.dtype)
    return finish(y_lat, a_lat), (finish(y_ctx, a_ctx) if ctx_out else None)


def chunk_gmlp(z, w_s, b_s):
    bsz, n, _ = z.shape
    u, v = jnp.split(jax.nn.gelu(z), 2, axis=-1)
    v = layer_norm(v).reshape(bsz, n // GM_CHUNK, GM_CHUNK, GM_HEADS, GM_WIDTH // GM_HEADS)
    s = jnp.einsum('hts,bnshc->bnthc', w_s, v) + b_s.T[None, None, :, :, None]
    return u * s.reshape(bsz, n, GM_WIDTH)


def diff_attend(q, k, v, lam):
    s = jnp.einsum('bqhmd,bkhmd->bhmqk', q, k).astype(jnp.float32) * (DA_HEAD_DIM ** -0.5)
    p = jax.nn.softmax(s, axis=-1)
    attn = p[:, :, 0] - lam * p[:, :, 1]
    return jnp.einsum('bhqk,bkhe->bqhe', attn.astype(v.dtype), v)


def diff_attention(q_lat, k_lat, v_lat, q_ctx, k_ctx, v_ctx, da_lam, subln_g, cos, sin, lam_init, ctx_out):
    bsz, n, _ = q_lat.shape

    def qk_heads(t):
        return t.reshape(t.shape[:2] + (DA_HEADS, 2, DA_HEAD_DIM))

    def v_heads(t):
        return t.reshape(t.shape[:2] + (DA_HEADS, 2 * DA_HEAD_DIM))
    q_l = apply_axial_rope(qk_heads(q_lat), cos, sin)
    k_l = apply_axial_rope(qk_heads(k_lat), cos, sin)
    q_c, k_c = qk_heads(q_ctx), qk_heads(k_ctx)
    v_l, v_c = v_heads(v_lat), v_heads(v_ctx)
    lf = da_lam.astype(jnp.float32)
    lam = jnp.exp(jnp.sum(lf[0] * lf[1])) - jnp.exp(jnp.sum(lf[2] * lf[3])) + lam_init
    k_all = jnp.concatenate([k_c, k_l], axis=1)
    v_all = jnp.concatenate([v_c, v_l], axis=1)
    nb = n // Q_BLOCK
    qb = jnp.moveaxis(q_l.reshape(bsz, nb, Q_BLOCK, DA_HEADS, 2, DA_HEAD_DIM), 1, 0)
    o = lax.map(lambda qq: diff_attend(qq, k_all, v_all, lam), qb)
    o_lat = jnp.moveaxis(o, 0, 1).reshape(bsz, n, DA_HEADS, 2 * DA_HEAD_DIM)

    def finish(t):
        return (rms_norm(t, subln_g) * (1.0 - lam_init)).reshape(t.shape[:2] + (DA_V,))
    y_ctx = finish(diff_attend(q_c, k_c, v_c, lam)) if ctx_out else None
    return finish(o_lat), y_ctx


def token_mixer(u_lat, u_ctx, w_in, s5_lam_re, s5_lam_im, s5_log_step, s5_b_re, s5_b_im, s5_c_re, s5_c_im,
                s5_d, gm_w_s, gm_b_s, da_lam, da_subln_g, w_glu_val, w_glu_gate, w_proj_gm, w_proj_da,
                w_out, cos, sin, lam_init, ctx_out):
    p_lat = u_lat @ w_in
    p_ctx = u_ctx @ w_in
    a_lat, z_lat, q_lat, k_lat, v_lat, g_lat = jnp.split(p_lat, IN_SPLITS, axis=-1)
    a_ctx, z_ctx, q_ctx, k_ctx, v_ctx, g_ctx = jnp.split(p_ctx, IN_SPLITS, axis=-1)
    ya_lat, ya_ctx = s5_branch(a_lat, a_ctx, s5_lam_re, s5_lam_im, s5_log_step, s5_b_re, s5_b_im,
                               s5_c_re, s5_c_im, s5_d, ctx_out)
    yc_lat, yc_ctx = diff_attention(q_lat, k_lat, v_lat, q_ctx, k_ctx, v_ctx, da_lam, da_subln_g,
                                    cos, sin, lam_init, ctx_out)

    def merge(y_a, y_b, y_c, gates):
        gate_a, gate_b, gate_c = jnp.split(jax.nn.sigmoid(gates), N_BRANCH, axis=-1)
        branch_a = (y_a @ w_glu_val) * jax.nn.sigmoid(y_a @ w_glu_gate)
        m = gate_a * branch_a + gate_b * (y_b @ w_proj_gm) + gate_c * (y_c @ w_proj_da)
        return m @ w_out
    out_lat = merge(ya_lat, chunk_gmlp(z_lat, gm_w_s, gm_b_s), yc_lat, g_lat)
    out_ctx = merge(ya_ctx, chunk_gmlp(z_ctx, gm_w_s, gm_b_s), yc_ctx, g_ctx) if ctx_out else None
    return out_lat, out_ctx


def swiglu(x, wg, wu, wd):
    return (jax.nn.silu(x @ wg) * (x @ wu)) @ wd


def routed_experts(h, eidx, w, w_gate, w_up, w_down):
    t, d = h.shape
    n = t * TOP_K
    flat_e = eidx.reshape(-1)
    flat_t = jnp.arange(n, dtype=jnp.int32) // TOP_K
    order = jnp.argsort(flat_e)
    se, st, sw = flat_e[order], flat_t[order], w.reshape(-1)[order]
    counts = jnp.bincount(flat_e, length=N_EXPERTS)
    padded = (counts + MOE_BLOCK - 1) // MOE_BLOCK * MOE_BLOCK
    start = jnp.cumsum(counts) - counts
    pend = jnp.cumsum(padded)
    pstart = pend - padded
    dest = pstart[se] + jnp.arange(n, dtype=jnp.int32) - start[se]
    n_blk = -(-n // MOE_BLOCK) + N_EXPERTS
    n_rows = n_blk * MOE_BLOCK
    row_tok = jnp.full((n_rows,), t, jnp.int32).at[dest].set(st)
    row_w = jnp.zeros((n_rows,), h.dtype).at[dest].set(sw)
    blk_e = jnp.minimum(jnp.searchsorted(pend, jnp.arange(n_blk) * MOE_BLOCK, side='right'), N_EXPERTS - 1)
    h_pad = jnp.concatenate([h, jnp.zeros((1, d), h.dtype)], axis=0)

    def block_ffn(args):
        tok, wt, e = args
        return swiglu(h_pad[tok], w_gate[e], w_up[e], w_down[e]) * wt[:, None]
    yb = lax.map(block_ffn, (row_tok.reshape(n_blk, MOE_BLOCK), row_w.reshape(n_blk, MOE_BLOCK), blk_e))
    return jax.ops.segment_sum(yb.reshape(n_rows, d), row_tok, num_segments=t + 1)[:t]


def moe_ffn(h, w_router, b_router, w_exp_gate, w_exp_up, w_exp_down, w_sh_gate, w_sh_up, w_sh_down):
    t = h.shape[0]
    scores = jax.nn.sigmoid((h @ w_router).astype(jnp.float32))
    sel = scores + b_router.astype(jnp.float32)
    group_score = jnp.sum(lax.top_k(sel.reshape(t, N_GROUPS, N_EXPERTS // N_GROUPS), 2)[0], axis=-1)
    _, gidx = lax.top_k(group_score, TOPK_GROUPS)
    gmask = jnp.sum(jax.nn.one_hot(gidx, N_GROUPS, dtype=jnp.float32), axis=1) > 0
    emask = jnp.repeat(gmask, N_EXPERTS // N_GROUPS, axis=1)
    _, eidx = lax.top_k(jnp.where(emask, sel, -jnp.inf), TOP_K)
    wts = jnp.take_along_axis(scores, eidx, axis=1)
    wts = wts / jnp.sum(wts, -1, keepdims=True) * ROUTED_SCALE
    routed = routed_experts(h, eidx, wts.astype(h.dtype), w_exp_gate, w_exp_up, w_exp_down)
    return routed + swiglu(h, w_sh_gate, w_sh_up, w_sh_down)


def setup_inputs(seed: int = 0) -> dict:
    key = jax.random.key(seed)
    ks = iter(jax.random.split(key, 48))
    f32 = jnp.float32

    def nrm(shape, scale):
        return jax.random.normal(next(ks), shape, f32) * scale
    D = D_MODEL
    G, P, C = S5_GROUPS, S5_STATE, S5_GROUP
    return {
        'x': nrm((BATCH, SEQ, D), 1.0),
        'c': nrm((BATCH, D), 1.0),
        'ctx': nrm((BATCH, CTX_LEN, D), 1.0),
        'c_ctx': nrm((D,), 1.0),
        'w_mod': nrm((DEPTH, D, 6 * D), 0.5 * D ** -0.5),
        'b_mod': nrm((DEPTH, 6 * D), 0.02),
        'w_in': nrm((DEPTH, D, IN_WIDTH), D ** -0.5),
        's5_lam_re': -0.5 + nrm((DEPTH, 2, G, P), 0.01),
        's5_lam_im': jnp.pi * jnp.arange(P, dtype=f32) + nrm((DEPTH, 2, G, P), 0.01),
        's5_log_step': jax.random.uniform(next(ks), (DEPTH, 2, G), f32,
                                          math.log(S5_DT_MIN), math.log(S5_DT_MAX)),
        's5_b_re': nrm((DEPTH, 2, G, P, C), (2 * C) ** -0.5),
        's5_b_im': nrm((DEPTH, 2, G, P, C), (2 * C) ** -0.5),
        's5_c_re': nrm((DEPTH, 2, G, C, P), P ** -0.5),
        's5_c_im': nrm((DEPTH, 2, G, C, P), P ** -0.5),
        's5_d': nrm((DEPTH, S5_WIDTH), 1.0),
        'gm_w_s': nrm((DEPTH, GM_HEADS, GM_CHUNK, GM_CHUNK), GM_CHUNK ** -0.5),
        'gm_b_s': 1.0 + nrm((DEPTH, GM_HEADS, GM_CHUNK), 0.01),
        'da_lam': nrm((DEPTH, 4, DA_HEAD_DIM), 0.1),
        'da_subln_g': 1.0 + nrm((DEPTH, 2 * DA_HEAD_DIM), 0.01),
        'w_glu_val': nrm((DEPTH, S5_WIDTH, D), S5_WIDTH ** -0.5),
        'w_glu_gate': nrm((DEPTH, S5_WIDTH, D), S5_WIDTH ** -0.5),
        'w_proj_gm': nrm((DEPTH, GM_WIDTH, D), GM_WIDTH ** -0.5),
        'w_proj_da': nrm((DEPTH, DA_V, D), DA_V ** -0.5),
        'w_out': nrm((DEPTH, D, D), DN_BETA * D ** -0.5),
        'ln1_g': 1.0 + nrm((DEPTH, D), 0.01),
        'ln1_b': nrm((DEPTH, D), 0.01),
        'ln2_g': 1.0 + nrm((DEPTH, D), 0.01),
        'ln2_b': nrm((DEPTH, D), 0.01),
        'w_router': nrm((DEPTH, D, N_EXPERTS), D ** -0.5),
        'b_router': nrm((DEPTH, N_EXPERTS), 0.01),
        'w_exp_gate': nrm((DEPTH, N_EXPERTS, D, EXPERT_HIDDEN), D ** -0.5),
        'w_exp_up': nrm((DEPTH, N_EXPERTS, D, EXPERT_HIDDEN), D ** -0.5),
        'w_exp_down': nrm((DEPTH, N_EXPERTS, EXPERT_HIDDEN, D), DN_BETA * EXPERT_HIDDEN ** -0.5),
        'w_sh_gate': nrm((DEPTH, D, SHARED_HIDDEN), D ** -0.5),
        'w_sh_up': nrm((DEPTH, D, SHARED_HIDDEN), D ** -0.5),
        'w_sh_down': nrm((DEPTH, SHARED_HIDDEN, D), DN_BETA * SHARED_HIDDEN ** -0.5),
    }


def reference(x, c, ctx, c_ctx, w_mod, b_mod, w_in, s5_lam_re, s5_lam_im, s5_log_step, s5_b_re, s5_b_im,
              s5_c_re, s5_c_im, s5_d, gm_w_s, gm_b_s, da_lam, da_subln_g, w_glu_val, w_glu_gate,
              w_proj_gm, w_proj_da, w_out, ln1_g, ln1_b, ln2_g, ln2_b, w_router, b_router,
              w_exp_gate, w_exp_up, w_exp_down, w_sh_gate, w_sh_up, w_sh_down):
    bsz, n, d = x.shape
    cos, sin = axial_rope_tables(n)
    silu_c = jax.nn.silu(c)
    silu_cc = jax.nn.silu(c_ctx)
    h_lat, h_ctx = x, ctx
    for l in range(DEPTH):
        last = l == DEPTH - 1
        mod_lat = (silu_c @ w_mod[l] + b_mod[l])[:, None, :]
        mod_ctx = (silu_cc @ w_mod[l] + b_mod[l])[None, None, :]
        sh1, sc1, g1, sh2, sc2, g2 = jnp.split(mod_lat, 6, axis=-1)
        csh1, csc1, cg1, csh2, csc2, cg2 = jnp.split(mod_ctx, 6, axis=-1)
        lam_init = 0.8 - 0.6 * math.exp(-0.3 * l)
        mix_lat, mix_ctx = token_mixer(
            modulate(layer_norm(h_lat), sh1, sc1), modulate(layer_norm(h_ctx), csh1, csc1),
            w_in[l], s5_lam_re[l], s5_lam_im[l], s5_log_step[l], s5_b_re[l], s5_b_im[l],
            s5_c_re[l], s5_c_im[l], s5_d[l], gm_w_s[l], gm_b_s[l], da_lam[l], da_subln_g[l],
            w_glu_val[l], w_glu_gate[l], w_proj_gm[l], w_proj_da[l], w_out[l],
            cos, sin, lam_init, not last)
        h_lat = layer_norm(DN_ALPHA * h_lat + g1 * mix_lat, ln1_g[l], ln1_b[l])
        f_lat = modulate(layer_norm(h_lat), sh2, sc2).reshape(-1, d)
        moe_args = (w_router[l], b_router[l], w_exp_gate[l], w_exp_up[l], w_exp_down[l],
                    w_sh_gate[l], w_sh_up[l], w_sh_down[l])
        if last:
            y_lat = moe_ffn(f_lat, *moe_args).reshape(bsz, n, d)
        else:
            h_ctx = layer_norm(DN_ALPHA * h_ctx + cg1 * mix_ctx, ln1_g[l], ln1_b[l])
            f_ctx = modulate(layer_norm(h_ctx), csh2, csc2).reshape(-1, d)
            y = moe_ffn(jnp.concatenate([f_lat, f_ctx], axis=0), *moe_args)
            y_lat = y[:bsz * n].reshape(bsz, n, d)
            h_ctx = layer_norm(DN_ALPHA * h_ctx + cg2 * y[bsz * n:].reshape(h_ctx.shape), ln2_g[l], ln2_b[l])
        h_lat = layer_norm(DN_ALPHA * h_lat + g2 * y_lat, ln2_g[l], ln2_b[l])
    return h_lat
```

```python
import functools
import math

import jax
import jax.numpy as jnp
from jax import lax
from jax.experimental import pallas as pl
from jax.experimental.pallas import tpu as pltpu

F32 = jnp.float32
BF16 = jnp.bfloat16
HIGHEST = lax.Precision.HIGHEST

D_MODEL = 1024
GRID_W = 64
S5_WIDTH = 256
S5_GROUP = 16
S5_GROUPS = S5_WIDTH // S5_GROUP
S5_STATE = 64
S5_CHUNK = 16
GM_WIDTH = 256
GM_HEADS = 4
GM_CHUNK = 128
DA_HEADS = 4
DA_HEAD_DIM = 64
DA_QK = DA_HEADS * 2 * DA_HEAD_DIM
DA_V = DA_HEADS * 2 * DA_HEAD_DIM
ROPE_BASE = 10000.0
N_BRANCH = 3
N_EXPERTS = 64
TOP_K = 8
N_GROUPS = 8
TOPK_GROUPS = 4
EXPERT_HIDDEN = 256
ROUTED_SCALE = 2.5
LN_EPS = 1e-5

COL_GATE = 0
COL_Q = N_BRANCH * D_MODEL
COL_K = COL_Q + DA_QK
COL_V = COL_K + DA_QK
COL_S5 = COL_V + DA_V
COL_ZU = COL_S5 + S5_WIDTH
COL_ZV = COL_ZU + GM_WIDTH
IN_WIDTH = COL_ZV + GM_WIDTH

LANES = 128
ROW_TILE = 1024
SUB_TILE = 256
VMEM_LIMIT = 48 * 1024 * 1024


def _cparams(*sem):
    return pltpu.CompilerParams(dimension_semantics=sem, vmem_limit_bytes=VMEM_LIMIT)


def _ln(x):
    mu = jnp.mean(x, -1, keepdims=True)
    xc = x - mu
    var = jnp.mean(xc * xc, -1, keepdims=True)
    return xc * lax.rsqrt(var + LN_EPS)


def _gelu(x):
    return 0.5 * x * (1.0 + jnp.tanh(math.sqrt(2.0 / math.pi) * (x + 0.044715 * (x * x * x))))


def _sigmoid(x):
    return 1.0 / (1.0 + jnp.exp(-x))


def _silu(x):
    return x * _sigmoid(x)


def _mod_kernel(c_ref, w_ref, b_ref, o_ref):
    s = _silu(c_ref[...])
    o_ref[...] = jnp.dot(s, w_ref[...], preferred_element_type=F32, precision=HIGHEST) + b_ref[...]


def _modulation(cond, w_mod, b_mod):
    depth, d, n = w_mod.shape
    tn = 1536
    return pl.pallas_call(
        _mod_kernel,
        grid=(depth, n // tn),
        in_specs=[pl.BlockSpec((8, d), lambda l, j: (0, 0)),
                  pl.BlockSpec((None, d, tn), lambda l, j: (l, 0, j)),
                  pl.BlockSpec((None, 1, tn), lambda l, j: (l, 0, j))],
        out_specs=pl.BlockSpec((None, 8, tn), lambda l, j: (l, 0, j)),
        out_shape=jax.ShapeDtypeStruct((depth, 8, n), F32),
        compiler_params=_cparams("arbitrary", "arbitrary"),
        name="modulation",
    )(cond, w_mod, b_mod.reshape(depth, 1, n))


def _inproj_kernel(h_ref, sh_ref, sc_ref, w_ref, o_ref, xn_ref):
    @pl.when(pl.program_id(1) == 0)
    def _():
        x = _ln(h_ref[...])
        xn_ref[...] = (x * (1.0 + sc_ref[...]) + sh_ref[...]).astype(BF16)

    o_ref[...] = jnp.dot(xn_ref[...], w_ref[...], preferred_element_type=F32).astype(BF16)


def _inproj(h, shift, scale, w, tiles_per_batch, n_batch):
    t, d = h.shape
    n = w.shape[1]
    tn = 768
    mod_row = lambda i, j: (jnp.minimum(i // tiles_per_batch, n_batch), 0, 0)
    return pl.pallas_call(
        _inproj_kernel,
        grid=(t // ROW_TILE, n // tn),
        in_specs=[pl.BlockSpec((ROW_TILE, d), lambda i, j: (i, 0)),
                  pl.BlockSpec((None, 1, d), mod_row),
                  pl.BlockSpec((None, 1, d), mod_row),
                  pl.BlockSpec((d, tn), lambda i, j: (0, j))],
        out_specs=pl.BlockSpec((ROW_TILE, tn), lambda i, j: (i, j)),
        out_shape=jax.ShapeDtypeStruct((t, n), BF16),
        scratch_shapes=[pltpu.VMEM((ROW_TILE, d), BF16)],
        compiler_params=_cparams("arbitrary", "arbitrary"),
        name="inproj",
    )(h, shift, scale, w)


def _rope_kernel(q_ref, k_ref, v_ref, cos_ref, sin_ref, qo_ref, k0_ref, k1_ref, vo_ref, *, n_lat_tiles):
    is_lat = pl.program_id(0) < n_lat_tiles
    shape = q_ref.shape
    lane = lax.broadcasted_iota(jnp.int32, shape, 1)
    upper16 = (lane % 32) >= 16
    map1 = (lane % LANES) >= DA_HEAD_DIM
    cos = jnp.where(is_lat, cos_ref[...], 1.0)
    sin = jnp.where(is_lat, sin_ref[...], 0.0)

    def rope(x):
        partner = jnp.where(upper16, pltpu.roll(x, 16, 1), pltpu.roll(x, shape[1] - 16, 1))
        return x * cos + partner * sin

    q = rope(q_ref[...].astype(F32)) * (DA_HEAD_DIM ** -0.5)
    k = rope(k_ref[...].astype(F32))
    qo_ref[...] = q.astype(BF16)
    k0_ref[...] = jnp.where(map1, 0.0, k).astype(BF16)
    k1_ref[...] = jnp.where(map1, k, 0.0).astype(BF16)
    vo_ref[...] = v_ref[...]


def _rope_pack(p, cos_t, sin_t, n_batch, seq, ctx):
    t = p.shape[0]
    nl = seq // SUB_TILE
    nc = ctx // SUB_TILE
    n_lat_tiles = n_batch * nl

    def kv_map(i):
        j = i - n_lat_tiles
        b = jnp.where(i < n_lat_tiles, i // nl, j // nc)
        blk = jnp.where(i < n_lat_tiles, nc + i % nl, j % nc)
        return (b, blk, 0)

    tab_map = lambda i: (jnp.where(i < n_lat_tiles, i % nl, 0), 0)
    col = lambda c: (lambda i: (i, c // DA_QK))
    kv_shape = jax.ShapeDtypeStruct((n_batch, ctx + seq, DA_QK), BF16)
    return pl.pallas_call(
        functools.partial(_rope_kernel, n_lat_tiles=n_lat_tiles),
        grid=(t // SUB_TILE,),
        in_specs=[pl.BlockSpec((SUB_TILE, DA_QK), col(COL_Q)),
                  pl.BlockSpec((SUB_TILE, DA_QK), col(COL_K)),
                  pl.BlockSpec((SUB_TILE, DA_V), col(COL_V)),
                  pl.BlockSpec((SUB_TILE, DA_QK), tab_map),
                  pl.BlockSpec((SUB_TILE, DA_QK), tab_map)],
        out_specs=[pl.BlockSpec((SUB_TILE, DA_QK), lambda i: (i, 0)),
                   pl.BlockSpec((None, SUB_TILE, DA_QK), kv_map),
                   pl.BlockSpec((None, SUB_TILE, DA_QK), kv_map),
                   pl.BlockSpec((None, SUB_TILE, DA_V), kv_map)],
        out_shape=[jax.ShapeDtypeStruct((t, DA_QK), BF16), kv_shape, kv_shape, kv_shape],
        compiler_params=_cparams("arbitrary"),
        name="rope_pack",
    )(p, p, p, cos_t, sin_t)


def _rope_tables(seq):
    pos = jnp.arange(seq)
    row = (pos // GRID_W).astype(F32)[:, None]
    colp = (pos % GRID_W).astype(F32)[:, None]
    axis_dim = DA_HEAD_DIM // 2
    inv_freq = ROPE_BASE ** (-jnp.arange(0, axis_dim, 2, dtype=F32) / axis_dim)
    ang_r = row * inv_freq
    ang_c = colp * inv_freq
    ang = jnp.concatenate([ang_r, ang_r, ang_c, ang_c], -1)
    sign = jnp.concatenate([-jnp.ones((16,), F32), jnp.ones((16,), F32)] * 2)
    reps = DA_QK // DA_HEAD_DIM
    return jnp.tile(jnp.cos(ang), (1, reps)), jnp.tile(jnp.sin(ang) * sign, (1, reps))


def _attn_kernel(lam_ref, q_ref, k0_ref, k1_ref, v_ref, g_ref, o_ref, *, head_len, n_tail, tk, out_scale):
    q = q_ref[...]
    tq = q.shape[0]
    nt = (((1,), (1,)), ((), ()))

    def online(state, kc, vc):
        m, l, acc = state
        s = lax.dot_general(q, kc, nt, preferred_element_type=F32)
        m_new = jnp.maximum(m, jnp.max(s, -1, keepdims=True))
        alpha = jnp.exp(m - m_new)
        p = jnp.exp(s - m_new)
        l = alpha * l + jnp.sum(p, -1, keepdims=True)
        acc = alpha * acc + jnp.dot(p.astype(BF16), vc, preferred_element_type=F32)
        return m_new, l, acc

    def step(carry, start, size):
        st0, st1 = carry
        vc = v_ref[pl.ds(start, size), :]
        st0 = online(st0, k0_ref[pl.ds(start, size), :], vc)
        st1 = online(st1, k1_ref[pl.ds(start, size), :], vc)
        return st0, st1

    init = (jnp.full((tq, 1), -jnp.inf, F32), jnp.zeros((tq, 1), F32), jnp.zeros((tq, LANES), F32))
    carry = step((init, init), 0, head_len)
    if n_tail:
        carry = lax.fori_loop(
            0, n_tail, lambda j, c: step(c, pl.multiple_of(head_len + j * tk, tk), tk), carry)
    (_, l0, a0), (_, l1, a1) = carry
    o = a0 / l0 - lam_ref[0] * (a1 / l1)
    o = o * lax.rsqrt(jnp.mean(o * o, -1, keepdims=True) + LN_EPS)
    o_ref[...] = (o * g_ref[...] * out_scale).astype(BF16)


def _attention(lam, q, k0, k1, v, subln_g, *, n_rows, q_row0, tq, kv_len, head_len, tk, out_scale):
    n_batch = k0.shape[0]
    per_batch = n_rows // n_batch // tq
    q0 = q_row0 // tq
    n_tail = (kv_len - head_len) // tk
    kv_spec = pl.BlockSpec((None, kv_len, LANES), lambda b, h, i: (b, 0, h))
    return pl.pallas_call(
        functools.partial(_attn_kernel, head_len=head_len, n_tail=n_tail, tk=tk, out_scale=out_scale),
        grid=(n_batch, DA_HEADS, per_batch),
        in_specs=[pl.BlockSpec(memory_space=pltpu.SMEM),
                  pl.BlockSpec((tq, LANES), lambda b, h, i: (q0 + b * per_batch + i, h)),
                  kv_spec, kv_spec, kv_spec,
                  pl.BlockSpec((1, LANES), lambda b, h, i: (0, 0))],
        out_specs=pl.BlockSpec((tq, LANES), lambda b, h, i: (b * per_batch + i, h)),
        out_shape=jax.ShapeDtypeStruct((n_rows, DA_V), BF16),
        compiler_params=_cparams("arbitrary", "arbitrary", "arbitrary"),
        name="diff_attention",
    )(lam, q, k0, k1, v, subln_g)


def _s5_in_kernel(u_ref, bf_ref, bb_ref, z_ref):
    u = u_ref[...]
    zf = jnp.dot(u, bf_ref[...], preferred_element_type=F32)
    zb = jnp.dot(u, bb_ref[...], preferred_element_type=F32)
    fwd_row = (lax.broadcasted_iota(jnp.int32, zf.shape, 0) % 8) < 4
    z_ref[...] = jnp.where(fwd_row, zf, zb)


def _s5_scan_kernel(z_ref, a1_ref, a2_ref, p_ref, *, n_steps):
    a1 = a1_ref[...]
    a2 = a2_ref[...]
    width = a1.shape[1]

    def swap_halves(s):
        return jnp.concatenate(
            [pltpu.roll(s[:, j * LANES:(j + 1) * LANES], LANES // 2, 1) for j in range(width // LANES)], axis=1)

    def body(i, s):
        r = pl.multiple_of(i * 8, 8)
        p_ref[pl.ds(r, 8), :] = s
        return a1 * s + a2 * swap_halves(s) + z_ref[pl.ds(r, 8), :]

    lax.fori_loop(0, n_steps, body, jnp.zeros(a1.shape, F32))


def _s5_out_kernel(u_ref, p_ref, tf_ref, tb_ref, cf_ref, cb_ref, y_ref):
    u = u_ref[...]
    p = p_ref[...].astype(BF16)
    yf = jnp.dot(u, tf_ref[...], preferred_element_type=F32) + jnp.dot(p, cf_ref[...], preferred_element_type=F32)
    yb = jnp.dot(u, tb_ref[...], preferred_element_type=F32) + jnp.dot(p, cb_ref[...], preferred_element_type=F32)
    fwd_row = (lax.broadcasted_iota(jnp.int32, yf.shape, 0) % 8) < 4
    y_ref[...] = jnp.where(fwd_row, yf, yb)


def _s5_mixer(u8, mats):
    g, r, w = u8.shape
    sw = 2 * S5_STATE
    mat_spec = lambda rows: pl.BlockSpec((None, rows, mats["tf"].shape[2]), lambda i: (i, 0, 0))
    z = pl.pallas_call(
        _s5_in_kernel,
        grid=(g,),
        in_specs=[pl.BlockSpec((None, r, w), lambda i: (i, 0, 0)),
                  pl.BlockSpec((None, w, sw), lambda i: (i, 0, 0)),
                  pl.BlockSpec((None, w, sw), lambda i: (i, 0, 0))],
        out_specs=pl.BlockSpec((r, sw), lambda i: (0, i)),
        out_shape=jax.ShapeDtypeStruct((r, g * sw), F32),
        compiler_params=_cparams("arbitrary"),
        name="s5_in",
    )(u8, mats["bf"], mats["bb"])
    quarter = g * sw // 4
    prev = pl.pallas_call(
        functools.partial(_s5_scan_kernel, n_steps=r // 8),
        grid=(4,),
        in_specs=[pl.BlockSpec((r, quarter), lambda i: (0, i)),
                  pl.BlockSpec((8, quarter), lambda i: (0, i)),
                  pl.BlockSpec((8, quarter), lambda i: (0, i))],
        out_specs=pl.BlockSpec((r, quarter), lambda i: (0, i)),
        out_shape=jax.ShapeDtypeStruct((r, g * sw), F32),
        compiler_params=_cparams("arbitrary"),
        name="s5_scan",
    )(z, mats["a1"], mats["a2"])
    return pl.pallas_call(
        _s5_out_kernel,
        grid=(g,),
        in_specs=[pl.BlockSpec((None, r, w), lambda i: (i, 0, 0)),
                  pl.BlockSpec((r, sw), lambda i: (0, i)),
                  mat_spec(w), mat_spec(w), mat_spec(sw), mat_spec(sw)],
        out_specs=pl.BlockSpec((None, r, w), lambda i: (i, 0, 0)),
        out_shape=jax.ShapeDtypeStruct((g, r, w), F32),
        compiler_params=_cparams("arbitrary"),
        name="s5_out",
    )(u8, prev, mats["tf"], mats["tb"], mats["cf"], mats["cb"])


def _s5_matrices(lam_re, lam_im, log_step, b_re, b_im, c_re, c_im, d_skip):
    n = S5_CHUNK
    dt = jnp.exp(log_step.astype(F32))[..., None]
    lr, li = lam_re.astype(F32), lam_im.astype(F32)
    mag = jnp.exp(lr * dt)
    a_re, a_im = mag * jnp.cos(li * dt), mag * jnp.sin(li * dt)
    den = lr * lr + li * li
    n_re = a_re - 1.0
    z_re = (n_re * lr + a_im * li) / den
    z_im = (a_im * lr - n_re * li) / den
    br, bi = b_re.astype(F32), b_im.astype(F32)
    bb_re = z_re[..., None] * br - z_im[..., None] * bi
    bb_im = z_re[..., None] * bi + z_im[..., None] * br
    j = jnp.arange(n + 1, dtype=F32)[:, None, None, None]
    pmag = jnp.exp(lr * dt * j)
    pw_re, pw_im = pmag * jnp.cos(li * dt * j), pmag * jnp.sin(li * dt * j)
    cr, ci = c_re.astype(F32), c_im.astype(F32)
    g, p, c = S5_GROUPS, S5_STATE, S5_GROUP

    def cmul(xr, xi, yr, yi):
        return xr * yr - xi * yi, xr * yi + xi * yr

    def in_mat(direction, powers):
        er = pw_re[powers, direction][:, :, :, None]
        ei = pw_im[powers, direction][:, :, :, None]
        xr, xi = cmul(er, ei, bb_re[direction][None], bb_im[direction][None])
        m = jnp.concatenate([xr, xi], axis=2)
        return m.transpose(1, 0, 3, 2).reshape(g, n * c, 2 * p)

    def out_mat(direction, powers):
        er = pw_re[powers, direction][:, :, None, :]
        ei = pw_im[powers, direction][:, :, None, :]
        wr, wi = cmul(cr[direction][None], ci[direction][None], er, ei)
        m = jnp.concatenate([wr, -wi], axis=3)
        return m.transpose(1, 3, 0, 2).reshape(g, 2 * p, n * c)

    def toeplitz(direction):
        er = pw_re[:n, direction][:, :, None, :, None]
        ei = pw_im[:n, direction][:, :, None, :, None]
        wr, wi = cmul(cr[direction][None, :, :, :, None], ci[direction][None, :, :, :, None], er, ei)
        k = jnp.sum(wr * bb_re[direction][None, :, None] - wi * bb_im[direction][None, :, None], axis=3)
        return k

    steps = jnp.arange(n)
    lag = steps[None, :] - steps[:, None]
    kf, kb = toeplitz(0), toeplitz(1)
    tf = jnp.where((lag >= 0)[:, :, None, None, None], kf[jnp.clip(lag, 0, n - 1)], 0.0)
    tb = jnp.where((lag <= 0)[:, :, None, None, None], kb[jnp.clip(-lag, 0, n - 1)], 0.0)
    skip = d_skip.astype(F32).reshape(g, c)
    eye_t = jnp.eye(n, dtype=F32)[:, :, None, None, None]
    eye_c = jnp.eye(c, dtype=F32)[None, None, None]
    tf = tf + eye_t * eye_c * skip[None, None, :, :, None]
    to_mat = lambda t: t.transpose(2, 0, 4, 1, 3).reshape(g, n * c, n * c)

    def coef(direction):
        ar, ai = pw_re[n, direction], pw_im[n, direction]
        return (jnp.concatenate([ar, ar], -1).reshape(-1), jnp.concatenate([-ai, ai], -1).reshape(-1))

    a1f, a2f = coef(0)
    a1b, a2b = coef(1)
    rows = lambda f, b: jnp.concatenate([jnp.tile(f[None], (4, 1)), jnp.tile(b[None], (4, 1))], 0)
    return {
        "bf": in_mat(0, n - 1 - steps).astype(BF16), "bb": in_mat(1, steps).astype(BF16),
        "cf": out_mat(0, steps + 1).astype(BF16), "cb": out_mat(1, n - steps).astype(BF16),
        "tf": to_mat(tf).astype(BF16), "tb": to_mat(tb).astype(BF16),
        "a1": rows(a1f, a1b), "a2": rows(a2f, a2b),
    }


def _s5_pack(p, n_batch, seq, ctx):
    n = S5_CHUNK
    a = p[:, COL_S5:COL_S5 + S5_WIDTH]
    lat = a[:n_batch * seq].reshape(n_batch, seq // n, n, S5_GROUPS, S5_GROUP)
    cx = a[n_batch * seq:].reshape(n_batch, ctx // n, n, S5_GROUPS, S5_GROUP)
    fwd = jnp.concatenate([cx, lat], axis=1)
    bwd = jnp.concatenate([cx[:, ::-1], lat[:, ::-1]], axis=1)
    both = jnp.concatenate([fwd, bwd], axis=0)
    steps = both.shape[1]
    return both.transpose(3, 1, 0, 2, 4).reshape(S5_GROUPS, steps * 2 * n_batch, n * S5_GROUP)


def _s5_unpack(y, n_batch, seq, ctx):
    n = S5_CHUNK
    steps = y.shape[1] // (2 * n_batch)
    y = y.reshape(S5_GROUPS, steps, 2 * n_batch, n, S5_GROUP).transpose(2, 1, 3, 0, 4)
    nc = ctx // n
    fwd, bwd = y[:n_batch], y[n_batch:]
    cx = fwd[:, :nc] + bwd[:, :nc][:, ::-1]
    lat = fwd[:, nc:] + bwd[:, nc:][:, ::-1]
    return jnp.concatenate([lat.reshape(n_batch * seq, S5_WIDTH), cx.reshape(n_batch * ctx, S5_WIDTH)], 0)


def _merge_kernel(ya_ref, zu_ref, zv_ref, yc_ref, ga_ref, gb_ref, gc_ref, h_ref,
                  g1_ref, sh2_ref, sc2_ref, lng_ref, lnb_ref,
                  wglu_ref, wgm_ref, wda_ref, wout_ref, ws_ref, bs_ref, wr_ref,
                  h1_ref, f_ref, lg_ref, *, alpha):
    f32 = lambda ref: ref[...].astype(F32)
    ya = _gelu(ya_ref[...]).astype(BF16)
    glu = jnp.dot(ya, wglu_ref[...], preferred_element_type=F32)
    branch_a = glu[:, :D_MODEL] * _sigmoid(glu[:, D_MODEL:])

    u = _gelu(f32(zu_ref))
    v = _ln(_gelu(f32(zv_ref))).astype(BF16)
    head = lax.broadcasted_iota(jnp.int32, (GM_CHUNK, GM_WIDTH), 1) // (GM_WIDTH // GM_HEADS)
    parts = []
    for ck in range(v.shape[0] // GM_CHUNK):
        vc = v[ck * GM_CHUNK:(ck + 1) * GM_CHUNK]
        s = bs_ref[...]
        for hd in range(GM_HEADS):
            s = s + jnp.dot(ws_ref[hd], jnp.where(head == hd, vc, jnp.zeros_like(vc)),
                            preferred_element_type=F32)
        parts.append(s)
    yb = (u * jnp.concatenate(parts, axis=0)).astype(BF16)

    m = _sigmoid(f32(ga_ref)) * branch_a
    m = m + _sigmoid(f32(gb_ref)) * jnp.dot(yb, wgm_ref[...], preferred_element_type=F32)
    m = m + _sigmoid(f32(gc_ref)) * jnp.dot(yc_ref[...], wda_ref[...], preferred_element_type=F32)
    mix = jnp.dot(m.astype(BF16), wout_ref[...], preferred_element_type=F32)

    h1 = _ln(alpha * h_ref[...] + g1_ref[...] * mix) * lng_ref[...] + lnb_ref[...]
    h1_ref[...] = h1
    f = _ln(h1) * (1.0 + sc2_ref[...]) + sh2_ref[...]
    f_ref[...] = f.astype(BF16)
    lg_ref[...] = jnp.dot(f, wr_ref[...], preferred_element_type=F32, precision=HIGHEST)


def _merge(ya, p, yc, h, mods, ln_g, ln_b, w, *, n_rows, tiles_per_batch, n_batch, alpha):
    d = D_MODEL
    tm = SUB_TILE
    row = lambda i: (i, 0)
    mod_row = lambda i: (jnp.minimum(i // tiles_per_batch, n_batch), 0, 0)
    pcol = lambda c, width: pl.BlockSpec((tm, width), lambda i: (i, c // width))
    full = lambda a: pl.BlockSpec(a.shape, lambda i: (0,) * a.ndim)
    mod_spec = pl.BlockSpec((None, 1, d), mod_row)
    vec = lambda a: a.reshape(1, d)
    weights = (w["glu"], w["gm"], w["da"], w["out"], w["gm_ws"], w["gm_bs"], w["router"])
    return pl.pallas_call(
        functools.partial(_merge_kernel, alpha=alpha),
        grid=(n_rows // tm,),
        in_specs=[pl.BlockSpec((tm, S5_WIDTH), row),
                  pcol(COL_ZU, GM_WIDTH), pcol(COL_ZV, GM_WIDTH),
                  pl.BlockSpec((tm, DA_V), row),
                  pcol(COL_GATE, d), pcol(COL_GATE + d, d), pcol(COL_GATE + 2 * d, d),
                  pl.BlockSpec((tm, d), row),
                  mod_spec, mod_spec, mod_spec,
                  pl.BlockSpec((1, d), lambda i: (0, 0)), pl.BlockSpec((1, d), lambda i: (0, 0))]
                 + [full(a) for a in weights],
        out_specs=[pl.BlockSpec((tm, d), row), pl.BlockSpec((tm, d), row), pl.BlockSpec((tm, LANES), row)],
        out_shape=[jax.ShapeDtypeStruct((n_rows, d), F32), jax.ShapeDtypeStruct((n_rows, d), BF16),
                   jax.ShapeDtypeStruct((n_rows, LANES), F32)],
        compiler_params=_cparams("arbitrary"),
        name="merge",
    )(ya, p, p, yc, p, p, p, h, mods["g1"], mods["sh2"], mods["sc2"], vec(ln_g), vec(ln_b), *weights)


def _router_kernel(lg_ref, b_ref, w_ref):
    tm = lg_ref.shape[0]
    per_group = N_EXPERTS // N_GROUPS
    neg = -jnp.inf
    logits = lg_ref[...].T[:N_EXPERTS]
    scores = _sigmoid(logits).reshape(N_GROUPS, per_group, tm)
    sel = scores + b_ref[...].reshape(N_GROUPS, per_group, 1)

    in_group = lax.broadcasted_iota(jnp.int32, sel.shape, 1)
    top1 = jnp.max(sel, axis=1, keepdims=True)
    first = jnp.min(jnp.where(sel == top1, in_group, per_group), axis=1, keepdims=True)
    top2 = jnp.max(jnp.where(in_group == first, neg, sel), axis=1, keepdims=True)
    gscore = top1 + top2

    gidx = lax.broadcasted_iota(jnp.int32, gscore.shape, 0)
    gsel = jnp.zeros(gscore.shape, jnp.bool_)
    for _ in range(TOPK_GROUPS):
        best = jnp.max(gscore, axis=0, keepdims=True)
        hit = gidx == jnp.min(jnp.where(gscore == best, gidx, N_GROUPS), axis=0, keepdims=True)
        gsel = gsel | hit
        gscore = jnp.where(hit, neg, gscore)

    eidx = lax.broadcasted_iota(jnp.int32, sel.shape, 0) * per_group + in_group
    cand = jnp.where(gsel, sel, neg)
    chosen = jnp.zeros(sel.shape, jnp.bool_)
    for _ in range(TOP_K):
        best = jnp.max(jnp.max(cand, axis=1, keepdims=True), axis=0, keepdims=True)
        at = jnp.where(cand == best, eidx, N_EXPERTS)
        hit = eidx == jnp.min(jnp.min(at, axis=1, keepdims=True), axis=0, keepdims=True)
        chosen = chosen | hit
        cand = jnp.where(hit, neg, cand)

    w = jnp.where(chosen, scores, 0.0)
    total = jnp.sum(jnp.sum(w, axis=1, keepdims=True), axis=0, keepdims=True)
    w = (w / total * ROUTED_SCALE).reshape(N_EXPERTS, tm)
    w_ref[...] = jnp.concatenate([w, jnp.zeros((LANES - N_EXPERTS, tm), F32)], axis=0).T


def _route(logits, b_router):
    t = logits.shape[0]
    tm = 512
    return pl.pallas_call(
        _router_kernel,
        grid=(t // tm,),
        in_specs=[pl.BlockSpec((tm, LANES), lambda i: (i, 0)),
                  pl.BlockSpec((N_EXPERTS, 1), lambda i: (0, 0))],
        out_specs=pl.BlockSpec((tm, LANES), lambda i: (i, 0)),
        out_shape=jax.ShapeDtypeStruct((t, LANES), F32),
        compiler_params=_cparams("arbitrary"),
        name="router",
    )(logits, b_router.astype(F32).reshape(N_EXPERTS, 1))


def _moe_kernel(x_ref, wt_ref, wgu_ref, wdn_ref, h1_ref, g2_ref, lng_ref, lnb_ref, o_ref, acc_ref, *, alpha):
    e = pl.program_id(1)
    hgu = jnp.dot(x_ref[...], wgu_ref[...], preferred_element_type=F32)
    hid = _silu(hgu[:, :EXPERT_HIDDEN]) * hgu[:, EXPERT_HIDDEN:]

    wt = wt_ref[...]
    pick = (lax.broadcasted_iota(jnp.int32, (LANES, LANES), 0) == e).astype(BF16)
    hi = wt.astype(BF16)
    lo = (wt - hi.astype(F32)).astype(BF16)
    col = jnp.dot(hi, pick, preferred_element_type=F32) + jnp.dot(lo, pick, preferred_element_type=F32)
    col = jnp.where(e == N_EXPERTS, 1.0, col)
    hid = hid * jnp.concatenate([col] * (EXPERT_HIDDEN // LANES), axis=1)
    y = jnp.dot(hid.astype(BF16), wdn_ref[...], preferred_element_type=F32)

    @pl.when(e == 0)
    def _():
        acc_ref[...] = y

    @pl.when(e > 0)
    def _():
        acc_ref[...] += y

    @pl.when(e == N_EXPERTS)
    def _():
        o_ref[...] = _ln(alpha * h1_ref[...] + g2_ref[...] * acc_ref[...]) * lng_ref[...] + lnb_ref[...]


def _moe(f, wt, wgu, wdn, h1, g2, ln_g, ln_b, *, tiles_per_batch, n_batch, alpha):
    t, d = f.shape
    tm = ROW_TILE
    mod_row = lambda i, e: (jnp.minimum(i // tiles_per_batch, n_batch), 0, 0)
    return pl.pallas_call(
        functools.partial(_moe_kernel, alpha=alpha),
        grid=(t // tm, N_EXPERTS + 1),
        in_specs=[pl.BlockSpec((tm, d), lambda i, e: (i, 0)),
                  pl.BlockSpec((tm, LANES), lambda i, e: (i, 0)),
                  pl.BlockSpec((None, d, 2 * EXPERT_HIDDEN), lambda i, e: (e, 0, 0)),
                  pl.BlockSpec((None, EXPERT_HIDDEN, d), lambda i, e: (e, 0, 0)),
                  pl.BlockSpec((tm, d), lambda i, e: (i, 0)),
                  pl.BlockSpec((None, 1, d), mod_row),
                  pl.BlockSpec((1, d), lambda i, e: (0, 0)),
                  pl.BlockSpec((1, d), lambda i, e: (0, 0))],
        out_specs=pl.BlockSpec((tm, d), lambda i, e: (i, 0)),
        out_shape=jax.ShapeDtypeStruct((t, d), F32),
        scratch_shapes=[pltpu.VMEM((tm, d), F32)],
        compiler_params=_cparams("arbitrary", "arbitrary"),
        name="moe",
    )(f, wt, wgu, wdn, h1, g2, ln_g.reshape(1, d), ln_b.reshape(1, d))


def _forward(x, c, ctx, c_ctx, w_mod, b_mod, w_in, s5_lam_re, s5_lam_im, s5_log_step, s5_b_re, s5_b_im,
             s5_c_re, s5_c_im, s5_d, gm_w_s, gm_b_s, da_lam, da_subln_g, w_glu_val, w_glu_gate,
             w_proj_gm, w_proj_da, w_out, ln1_g, ln1_b, ln2_g, ln2_b, w_router, b_router,
             w_exp_gate, w_exp_up, w_exp_down, w_sh_gate, w_sh_up, w_sh_down):
    n_batch, seq, d = x.shape
    ctx_len = ctx.shape[1]
    depth = w_mod.shape[0]
    n_lat = n_batch * seq
    alpha = (2 * depth) ** 0.25
    tiles_per_batch = seq // ROW_TILE
    sub_per_batch = seq // SUB_TILE
    assert d == D_MODEL and seq % ROW_TILE == 0 and (n_batch * ctx_len) % ROW_TILE == 0
    assert ctx_len % SUB_TILE == 0 and n_batch == 4

    cond = jnp.concatenate([c, c_ctx[None], jnp.zeros((8 - n_batch - 1, d), F32)], axis=0)
    mod = _modulation(cond, w_mod, b_mod)
    cos_t, sin_t = _rope_tables(seq)
    h = jnp.concatenate([x.reshape(n_lat, d), ctx.reshape(n_batch * ctx_len, d)], axis=0)

    for l in range(depth):
        last = l == depth - 1
        lam_init = 0.8 - 0.6 * math.exp(-0.3 * l)
        names = ("sh1", "sc1", "g1", "sh2", "sc2", "g2")
        mods = {k: mod[l, :, i * d:(i + 1) * d].reshape(8, 1, d) for i, k in enumerate(names)}

        wi = w_in[l]
        a_w, z_w, q_w, k_w, v_w, g_w = jnp.split(wi, (256, 768, 1280, 1792, 2304), axis=1)
        w_in_l = jnp.concatenate([g_w, q_w, k_w, v_w, a_w, z_w], axis=1).astype(BF16)
        p = _inproj(h, mods["sh1"], mods["sc1"], w_in_l, tiles_per_batch, n_batch)

        lf = da_lam[l].astype(F32)
        lam = (jnp.exp(jnp.sum(lf[0] * lf[1])) - jnp.exp(jnp.sum(lf[2] * lf[3])) + lam_init).reshape(1)
        q, k0, k1, v = _rope_pack(p, cos_t, sin_t, n_batch, seq, ctx_len)
        subln = da_subln_g[l].astype(F32).reshape(1, LANES)
        attn = functools.partial(_attention, lam, q, k0, k1, v, subln, out_scale=1.0 - lam_init)
        yc = attn(n_rows=n_lat, q_row0=0, tq=512, kv_len=ctx_len + seq, head_len=ctx_len, tk=512)
        if not last:
            yc_ctx = attn(n_rows=n_batch * ctx_len, q_row0=n_lat, tq=ctx_len, kv_len=ctx_len,
                          head_len=ctx_len, tk=ctx_len)
            yc = jnp.concatenate([yc, yc_ctx], axis=0)

        mats = _s5_matrices(s5_lam_re[l], s5_lam_im[l], s5_log_step[l], s5_b_re[l], s5_b_im[l],
                            s5_c_re[l], s5_c_im[l], s5_d[l])
        ya = _s5_unpack(_s5_mixer(_s5_pack(p, n_batch, seq, ctx_len), mats), n_batch, seq, ctx_len)

        n_rows = n_lat if last else h.shape[0]
        router_w = jnp.concatenate([w_router[l].astype(F32), jnp.zeros((d, LANES - N_EXPERTS), F32)], axis=1)
        merge_w = {
            "glu": jnp.concatenate([w_glu_val[l], w_glu_gate[l]], axis=1).astype(BF16),
            "gm": w_proj_gm[l].astype(BF16), "da": w_proj_da[l].astype(BF16), "out": w_out[l].astype(BF16),
            "gm_ws": gm_w_s[l].astype(BF16),
            "gm_bs": jnp.repeat(gm_b_s[l].astype(F32).T, GM_WIDTH // GM_HEADS, axis=1),
            "router": router_w,
        }
        h1, f, logits = _merge(ya, p, yc, h, mods, ln1_g[l], ln1_b[l], merge_w, n_rows=n_rows,
                               tiles_per_batch=sub_per_batch, n_batch=n_batch, alpha=alpha)

        wt = _route(logits, b_router[l])
        wgu = jnp.concatenate([jnp.concatenate([w_exp_gate[l], w_exp_up[l]], axis=2),
                               jnp.concatenate([w_sh_gate[l], w_sh_up[l]], axis=1)[None]], axis=0).astype(BF16)
        wdn = jnp.concatenate([w_exp_down[l], w_sh_down[l][None]], axis=0).astype(BF16)
        h = _moe(f, wt, wgu, wdn, h1, mods["g2"], ln2_g[l], ln2_b[l],
                 tiles_per_batch=tiles_per_batch, n_batch=n_batch, alpha=alpha)

    return h[:n_lat].reshape(n_batch, seq, d)


def kernel(x, c, ctx, c_ctx, w_mod, b_mod, w_in, s5_lam_re, s5_lam_im, s5_log_step, s5_b_re, s5_b_im, s5_c_re, s5_c_im, s5_d, gm_w_s, gm_b_s, da_lam, da_subln_g, w_glu_val, w_glu_gate, w_proj_gm, w_proj_da, w_out, ln1_g, ln1_b, ln2_g, ln2_b, w_router, b_router, w_exp_gate, w_exp_up, w_exp_down, w_sh_gate, w_sh_up, w_sh_down):
    return _forward(x, c, ctx, c_ctx, w_mod, b_mod, w_in, s5_lam_re, s5_lam_im, s5_log_step, s5_b_re, s5_b_im,
                    s5_c_re, s5_c_im, s5_d, gm_w_s, gm_b_s, da_lam, da_subln_g, w_glu_val, w_glu_gate,
                    w_proj_gm, w_proj_da, w_out, ln1_g, ln1_b, ln2_g, ln2_b, w_router, b_router,
                    w_exp_gate, w_exp_up, w_exp_down, w_sh_gate, w_sh_up, w_sh_down)
```

```python
import functools
import math

import jax
import jax.numpy as jnp
from jax import lax
from jax.experimental import pallas as pl
from jax.experimental.pallas import tpu as pltpu

F32 = jnp.float32
BF16 = jnp.bfloat16
HIGHEST = lax.Precision.HIGHEST

D_MODEL = 1024
GRID_W = 64
S5_WIDTH = 256
S5_GROUP = 16
S5_GROUPS = S5_WIDTH // S5_GROUP
S5_STATE = 64
S5_CHUNK = 16
GM_WIDTH = 256
GM_HEADS = 4
GM_CHUNK = 128
DA_HEADS = 4
DA_HEAD_DIM = 64
DA_QK = DA_HEADS * 2 * DA_HEAD_DIM
DA_V = DA_HEADS * 2 * DA_HEAD_DIM
ROPE_BASE = 10000.0
N_BRANCH = 3
N_EXPERTS = 64
TOP_K = 8
N_GROUPS = 8
TOPK_GROUPS = 4
EXPERT_HIDDEN = 256
ROUTED_SCALE = 2.5
LN_EPS = 1e-5

COL_GATE = 0
COL_Q = N_BRANCH * D_MODEL
COL_K = COL_Q + DA_QK
COL_V = COL_K + DA_QK
COL_S5 = COL_V + DA_V
COL_ZU = COL_S5 + S5_WIDTH
COL_ZV = COL_ZU + GM_WIDTH
IN_WIDTH = COL_ZV + GM_WIDTH

LANES = 128
ROW_TILE = 1024
SUB_TILE = 256
VMEM_LIMIT = 48 * 1024 * 1024
UNIT = 16
LOCAL_ROWS = SUB_TILE * TOP_K + N_EXPERTS * UNIT
FFN_ROWS = 512
FFN_UNITS = FFN_ROWS // UNIT


def _cparams(*sem):
    return pltpu.CompilerParams(dimension_semantics=sem, vmem_limit_bytes=VMEM_LIMIT)


def _ln(x):
    mu = jnp.mean(x, -1, keepdims=True)
    xc = x - mu
    var = jnp.mean(xc * xc, -1, keepdims=True)
    return xc * lax.rsqrt(var + LN_EPS)


def _gelu(x):
    return 0.5 * x * (1.0 + jnp.tanh(math.sqrt(2.0 / math.pi) * (x + 0.044715 * (x * x * x))))


def _sigmoid(x):
    return 1.0 / (1.0 + jnp.exp(-x))


def _silu(x):
    return x * _sigmoid(x)


def _mod_kernel(c_ref, w_ref, b_ref, o_ref):
    s = _silu(c_ref[...])
    o_ref[...] = jnp.dot(s, w_ref[...], preferred_element_type=F32, precision=HIGHEST) + b_ref[...]


def _modulation(cond, w_mod, b_mod):
    depth, d, n = w_mod.shape
    tn = 1536
    return pl.pallas_call(
        _mod_kernel,
        grid=(depth, n // tn),
        in_specs=[pl.BlockSpec((8, d), lambda l, j: (0, 0)),
                  pl.BlockSpec((None, d, tn), lambda l, j: (l, 0, j)),
                  pl.BlockSpec((None, 1, tn), lambda l, j: (l, 0, j))],
        out_specs=pl.BlockSpec((None, 8, tn), lambda l, j: (l, 0, j)),
        out_shape=jax.ShapeDtypeStruct((depth, 8, n), F32),
        compiler_params=_cparams("arbitrary", "arbitrary"),
        name="modulation",
    )(cond, w_mod, b_mod.reshape(depth, 1, n))


def _inproj_kernel(h_ref, sh_ref, sc_ref, w_ref, o_ref, xn_ref):
    @pl.when(pl.program_id(1) == 0)
    def _():
        x = _ln(h_ref[...])
        xn_ref[...] = (x * (1.0 + sc_ref[...]) + sh_ref[...]).astype(BF16)

    o_ref[...] = jnp.dot(xn_ref[...], w_ref[...], preferred_element_type=F32).astype(BF16)


def _inproj(h, shift, scale, w, tiles_per_batch, n_batch):
    t, d = h.shape
    n = w.shape[1]
    tn = 768
    mod_row = lambda i, j: (jnp.minimum(i // tiles_per_batch, n_batch), 0, 0)
    return pl.pallas_call(
        _inproj_kernel,
        grid=(t // ROW_TILE, n // tn),
        in_specs=[pl.BlockSpec((ROW_TILE, d), lambda i, j: (i, 0)),
                  pl.BlockSpec((None, 1, d), mod_row),
                  pl.BlockSpec((None, 1, d), mod_row),
                  pl.BlockSpec((d, tn), lambda i, j: (0, j))],
        out_specs=pl.BlockSpec((ROW_TILE, tn), lambda i, j: (i, j)),
        out_shape=jax.ShapeDtypeStruct((t, n), BF16),
        scratch_shapes=[pltpu.VMEM((ROW_TILE, d), BF16)],
        compiler_params=_cparams("arbitrary", "arbitrary"),
        name="inproj",
    )(h, shift, scale, w)


def _rope_kernel(q_ref, k_ref, v_ref, cos_ref, sin_ref, qo_ref, k0_ref, k1_ref, vo_ref, *, n_lat_tiles):
    is_lat = pl.program_id(0) < n_lat_tiles
    shape = q_ref.shape
    lane = lax.broadcasted_iota(jnp.int32, shape, 1)
    upper16 = (lane % 32) >= 16
    map1 = (lane % LANES) >= DA_HEAD_DIM
    cos = jnp.where(is_lat, cos_ref[...], 1.0)
    sin = jnp.where(is_lat, sin_ref[...], 0.0)

    def rope(x):
        partner = jnp.where(upper16, pltpu.roll(x, 16, 1), pltpu.roll(x, shape[1] - 16, 1))
        return x * cos + partner * sin

    q = rope(q_ref[...].astype(F32)) * (DA_HEAD_DIM ** -0.5)
    k = rope(k_ref[...].astype(F32))
    qo_ref[...] = q.astype(BF16)
    k0_ref[...] = jnp.where(map1, 0.0, k).astype(BF16)
    k1_ref[...] = jnp.where(map1, k, 0.0).astype(BF16)
    vo_ref[...] = v_ref[...]


def _rope_pack(p, cos_t, sin_t, n_batch, seq, ctx):
    t = p.shape[0]
    nl = seq // SUB_TILE
    nc = ctx // SUB_TILE
    n_lat_tiles = n_batch * nl

    def kv_map(i):
        j = i - n_lat_tiles
        b = jnp.where(i < n_lat_tiles, i // nl, j // nc)
        blk = jnp.where(i < n_lat_tiles, nc + i % nl, j % nc)
        return (b, blk, 0)

    tab_map = lambda i: (jnp.where(i < n_lat_tiles, i % nl, 0), 0)
    col = lambda c: (lambda i: (i, c // DA_QK))
    kv_shape = jax.ShapeDtypeStruct((n_batch, ctx + seq, DA_QK), BF16)
    return pl.pallas_call(
        functools.partial(_rope_kernel, n_lat_tiles=n_lat_tiles),
        grid=(t // SUB_TILE,),
        in_specs=[pl.BlockSpec((SUB_TILE, DA_QK), col(COL_Q)),
                  pl.BlockSpec((SUB_TILE, DA_QK), col(COL_K)),
                  pl.BlockSpec((SUB_TILE, DA_V), col(COL_V)),
                  pl.BlockSpec((SUB_TILE, DA_QK), tab_map),
                  pl.BlockSpec((SUB_TILE, DA_QK), tab_map)],
        out_specs=[pl.BlockSpec((SUB_TILE, DA_QK), lambda i: (i, 0)),
                   pl.BlockSpec((None, SUB_TILE, DA_QK), kv_map),
                   pl.BlockSpec((None, SUB_TILE, DA_QK), kv_map),
                   pl.BlockSpec((None, SUB_TILE, DA_V), kv_map)],
        out_shape=[jax.ShapeDtypeStruct((t, DA_QK), BF16), kv_shape, kv_shape, kv_shape],
        compiler_params=_cparams("arbitrary"),
        name="rope_pack",
    )(p, p, p, cos_t, sin_t)


def _rope_tables(seq):
    pos = jnp.arange(seq)
    row = (pos // GRID_W).astype(F32)[:, None]
    colp = (pos % GRID_W).astype(F32)[:, None]
    axis_dim = DA_HEAD_DIM // 2
    inv_freq = ROPE_BASE ** (-jnp.arange(0, axis_dim, 2, dtype=F32) / axis_dim)
    ang_r = row * inv_freq
    ang_c = colp * inv_freq
    ang = jnp.concatenate([ang_r, ang_r, ang_c, ang_c], -1)
    sign = jnp.concatenate([-jnp.ones((16,), F32), jnp.ones((16,), F32)] * 2)
    reps = DA_QK // DA_HEAD_DIM
    return jnp.tile(jnp.cos(ang), (1, reps)), jnp.tile(jnp.sin(ang) * sign, (1, reps))


def _attn_kernel(lam_ref, q_ref, k0_ref, k1_ref, v_ref, g_ref, o_ref, *, head_len, n_tail, tk, out_scale):
    q = q_ref[...]
    tq = q.shape[0]
    nt = (((1,), (1,)), ((), ()))

    def online(state, kc, vc):
        m, l, acc = state
        s = lax.dot_general(q, kc, nt, preferred_element_type=F32)
        m_new = jnp.maximum(m, jnp.max(s, -1, keepdims=True))
        alpha = jnp.exp(m - m_new)
        p = jnp.exp(s - m_new)
        l = alpha * l + jnp.sum(p, -1, keepdims=True)
        acc = alpha * acc + jnp.dot(p.astype(BF16), vc, preferred_element_type=F32)
        return m_new, l, acc

    def step(carry, start, size):
        st0, st1 = carry
        vc = v_ref[pl.ds(start, size), :]
        st0 = online(st0, k0_ref[pl.ds(start, size), :], vc)
        st1 = online(st1, k1_ref[pl.ds(start, size), :], vc)
        return st0, st1

    init = (jnp.full((tq, 1), -jnp.inf, F32), jnp.zeros((tq, 1), F32), jnp.zeros((tq, LANES), F32))
    carry = step((init, init), 0, head_len)
    if n_tail:
        carry = lax.fori_loop(
            0, n_tail, lambda j, c: step(c, pl.multiple_of(head_len + j * tk, math.gcd(head_len, tk)), tk), carry)
    (_, l0, a0), (_, l1, a1) = carry
    o = a0 / l0 - lam_ref[0] * (a1 / l1)
    o = o * lax.rsqrt(jnp.mean(o * o, -1, keepdims=True) + LN_EPS)
    o_ref[...] = (o * g_ref[...] * out_scale).astype(BF16)


def _attention(lam, q, k0, k1, v, subln_g, *, n_rows, q_row0, tq, kv_len, head_len, tk, out_scale):
    n_batch = k0.shape[0]
    per_batch = n_rows // n_batch // tq
    q0 = q_row0 // tq
    n_tail = (kv_len - head_len) // tk
    kv_spec = pl.BlockSpec((None, kv_len, LANES), lambda b, h, i: (b, 0, h))
    return pl.pallas_call(
        functools.partial(_attn_kernel, head_len=head_len, n_tail=n_tail, tk=tk, out_scale=out_scale),
        grid=(n_batch, DA_HEADS, per_batch),
        in_specs=[pl.BlockSpec(memory_space=pltpu.SMEM),
                  pl.BlockSpec((tq, LANES), lambda b, h, i: (q0 + b * per_batch + i, h)),
                  kv_spec, kv_spec, kv_spec,
                  pl.BlockSpec((1, LANES), lambda b, h, i: (0, 0))],
        out_specs=pl.BlockSpec((tq, LANES), lambda b, h, i: (b * per_batch + i, h)),
        out_shape=jax.ShapeDtypeStruct((n_rows, DA_V), BF16),
        compiler_params=_cparams("arbitrary", "arbitrary", "arbitrary"),
        name="diff_attention",
    )(lam, q, k0, k1, v, subln_g)


def _s5_in_kernel(u_ref, b_ref, z_ref):
    z_ref[...] = jnp.dot(u_ref[...], b_ref[...], preferred_element_type=F32)


def _s5_scan_kernel(z_ref, a1_ref, a2_ref, p_ref, *, n_batch, lat_chunks, ctx_chunks):
    reverse = pl.program_id(0) == 1
    a1 = a1_ref[...]
    a2 = a2_ref[...]
    width = a1.shape[1]

    def swap_halves(s):
        return jnp.concatenate(
            [pltpu.roll(s[:, j * LANES:(j + 1) * LANES], LANES // 2, 1) for j in range(width // LANES)], axis=1)

    def run(base, count, carry):
        def body(i, st):
            s, ssw = st
            row = base + jnp.where(reverse, count - 1 - i, i)
            z = z_ref[pl.ds(row, 1), :]
            p_ref[pl.ds(row, 1), :] = s
            return a1 * s + a2 * ssw + z, a1 * ssw - a2 * s + swap_halves(z)
        return lax.fori_loop(0, count, body, carry)

    zero = jnp.zeros(a1.shape, F32)
    for b in range(n_batch):
        st = run(n_batch * lat_chunks + b * ctx_chunks, ctx_chunks, (zero, zero))
        run(b * lat_chunks, lat_chunks, st)


def _s5_out_kernel(u_ref, p_ref, t_ref, c_ref, y_ref):
    y = jnp.dot(u_ref[...], t_ref[...], preferred_element_type=F32)
    y = y + jnp.dot(p_ref[...].astype(BF16), c_ref[...], preferred_element_type=F32)
    y_ref[...] = _gelu(y).astype(BF16)


def _s5_mixer(u, mats, *, n_batch, lat_chunks, ctx_chunks):
    r, w = u.shape
    tr = r // 4
    state_w = mats["b"].shape[1]
    tn = 1024
    z = pl.pallas_call(
        _s5_in_kernel,
        grid=(state_w // tn, r // tr),
        in_specs=[pl.BlockSpec((tr, w), lambda j, i: (i, 0)),
                  pl.BlockSpec((w, tn), lambda j, i: (0, j))],
        out_specs=pl.BlockSpec((tr, tn), lambda j, i: (i, j)),
        out_shape=jax.ShapeDtypeStruct((r, state_w), F32),
        compiler_params=_cparams("arbitrary", "arbitrary"),
        name="s5_in",
    )(u, mats["b"])
    half = state_w // 2
    coef = pl.BlockSpec((None, 1, half), lambda d: (d, 0, 0))
    prev = pl.pallas_call(
        functools.partial(_s5_scan_kernel, n_batch=n_batch, lat_chunks=lat_chunks, ctx_chunks=ctx_chunks),
        grid=(2,),
        in_specs=[pl.BlockSpec((r, half), lambda d: (0, d)), coef, coef],
        out_specs=pl.BlockSpec((r, half), lambda d: (0, d)),
        out_shape=jax.ShapeDtypeStruct((r, state_w), F32),
        compiler_params=_cparams("arbitrary"),
        name="s5_scan",
    )(z, mats["a1"], mats["a2"])
    tn = 512
    return pl.pallas_call(
        _s5_out_kernel,
        grid=(w // tn, r // tr),
        in_specs=[pl.BlockSpec((tr, w), lambda j, i: (i, 0)),
                  pl.BlockSpec((tr, state_w), lambda j, i: (i, 0)),
                  pl.BlockSpec((w, tn), lambda j, i: (0, j)),
                  pl.BlockSpec((state_w, tn), lambda j, i: (0, j))],
        out_specs=pl.BlockSpec((tr, tn), lambda j, i: (i, j)),
        out_shape=jax.ShapeDtypeStruct((r, w), BF16),
        compiler_params=_cparams("arbitrary", "arbitrary"),
        name="s5_out",
    )(u, prev, mats["t"], mats["c"])


def _s5_matrices(lam_re, lam_im, log_step, b_re, b_im, c_re, c_im, d_skip):
    n = S5_CHUNK
    dt = jnp.exp(log_step.astype(F32))[..., None]
    lr, li = lam_re.astype(F32), lam_im.astype(F32)
    mag = jnp.exp(lr * dt)
    a_re, a_im = mag * jnp.cos(li * dt), mag * jnp.sin(li * dt)
    den = lr * lr + li * li
    n_re = a_re - 1.0
    z_re = (n_re * lr + a_im * li) / den
    z_im = (a_im * lr - n_re * li) / den
    br, bi = b_re.astype(F32), b_im.astype(F32)
    bb_re = z_re[..., None] * br - z_im[..., None] * bi
    bb_im = z_re[..., None] * bi + z_im[..., None] * br
    j = jnp.arange(n + 1, dtype=F32)[:, None, None, None]
    pmag = jnp.exp(lr * dt * j)
    pw_re, pw_im = pmag * jnp.cos(li * dt * j), pmag * jnp.sin(li * dt * j)
    cr, ci = c_re.astype(F32), c_im.astype(F32)
    g, p, c = S5_GROUPS, S5_STATE, S5_GROUP

    def cmul(xr, xi, yr, yi):
        return xr * yr - xi * yi, xr * yi + xi * yr

    def in_mat(direction, powers):
        er = pw_re[powers, direction][:, :, :, None]
        ei = pw_im[powers, direction][:, :, :, None]
        xr, xi = cmul(er, ei, bb_re[direction][None], bb_im[direction][None])
        m = jnp.concatenate([xr, xi], axis=2)
        return m.transpose(1, 0, 3, 2).reshape(g, n * c, 2 * p)

    def out_mat(direction, powers):
        er = pw_re[powers, direction][:, :, None, :]
        ei = pw_im[powers, direction][:, :, None, :]
        wr, wi = cmul(cr[direction][None], ci[direction][None], er, ei)
        m = jnp.concatenate([wr, -wi], axis=3)
        return m.transpose(1, 3, 0, 2).reshape(g, 2 * p, n * c)

    def toeplitz(direction):
        er = pw_re[:n, direction][:, :, None, :, None]
        ei = pw_im[:n, direction][:, :, None, :, None]
        wr, wi = cmul(cr[direction][None, :, :, :, None], ci[direction][None, :, :, :, None], er, ei)
        k = jnp.sum(wr * bb_re[direction][None, :, None] - wi * bb_im[direction][None, :, None], axis=3)
        return k

    steps = jnp.arange(n)
    lag = steps[None, :] - steps[:, None]
    kf, kb = toeplitz(0), toeplitz(1)
    tf = jnp.where((lag >= 0)[:, :, None, None, None], kf[jnp.clip(lag, 0, n - 1)], 0.0)
    tb = jnp.where((lag <= 0)[:, :, None, None, None], kb[jnp.clip(-lag, 0, n - 1)], 0.0)
    skip = d_skip.astype(F32).reshape(g, c)
    eye_t = jnp.eye(n, dtype=F32)[:, :, None, None, None]
    eye_c = jnp.eye(c, dtype=F32)[None, None, None]
    tf = tf + eye_t * eye_c * skip[None, None, :, :, None]
    to_mat = lambda t: t.transpose(2, 0, 4, 1, 3).reshape(g, n * c, n * c)

    def coef(direction):
        ar, ai = pw_re[n, direction], pw_im[n, direction]
        return (jnp.concatenate([ar, ar], -1).reshape(-1), jnp.concatenate([-ai, ai], -1).reshape(-1))

    eye_g = jnp.eye(g, dtype=F32)

    def dense_in(m):
        m = m.reshape(g, n, c, -1)
        return jnp.einsum("gscq,gh->sgchq", m, eye_g).reshape(n * g * c, -1)

    def dense_out(m):
        q = m.shape[1]
        return jnp.einsum("gqtc,gh->gqthc", m.reshape(g, q, n, c), eye_g).reshape(g * q, n * g * c)

    t_all = (to_mat(tf) + to_mat(tb)).reshape(g, n, c, n, c)
    t_dense = jnp.einsum("gsatc,gh->sgathc", t_all, eye_g).reshape(n * g * c, n * g * c)
    a1f, a2f = coef(0)
    a1b, a2b = coef(1)
    return {
        "b": jnp.concatenate([dense_in(in_mat(0, n - 1 - steps)), dense_in(in_mat(1, steps))], 1).astype(BF16),
        "c": jnp.concatenate([dense_out(out_mat(0, steps + 1)), dense_out(out_mat(1, n - steps))], 0).astype(BF16),
        "t": t_dense.astype(BF16),
        "a1": jnp.stack([a1f, a1b])[:, None, :], "a2": jnp.stack([a2f, a2b])[:, None, :],
    }


def _merge_kernel(ya_ref, zu_ref, zv_ref, yc_ref, ga_ref, gb_ref, gc_ref, h_ref,
                  g1_ref, sh2_ref, sc2_ref, lng_ref, lnb_ref,
                  wglu_ref, wgm_ref, wda_ref, wout_ref, ws_ref, bs_ref, wr_ref,
                  h1_ref, f_ref, lg_ref, *, alpha):
    f32 = lambda ref: ref[...].astype(F32)
    glu = jnp.dot(ya_ref[...], wglu_ref[...], preferred_element_type=F32)
    branch_a = glu[:, :D_MODEL] * _sigmoid(glu[:, D_MODEL:])

    u = _gelu(f32(zu_ref))
    v = _ln(_gelu(f32(zv_ref))).astype(BF16)
    head = lax.broadcasted_iota(jnp.int32, (GM_CHUNK, GM_WIDTH), 1) // (GM_WIDTH // GM_HEADS)
    parts = []
    for ck in range(v.shape[0] // GM_CHUNK):
        vc = v[ck * GM_CHUNK:(ck + 1) * GM_CHUNK]
        s = bs_ref[...]
        for hd in range(GM_HEADS):
            s = s + jnp.dot(ws_ref[hd], jnp.where(head == hd, vc, jnp.zeros_like(vc)),
                            preferred_element_type=F32)
        parts.append(s)
    yb = (u * jnp.concatenate(parts, axis=0)).astype(BF16)

    m = _sigmoid(f32(ga_ref)) * branch_a
    m = m + _sigmoid(f32(gb_ref)) * jnp.dot(yb, wgm_ref[...], preferred_element_type=F32)
    m = m + _sigmoid(f32(gc_ref)) * jnp.dot(yc_ref[...], wda_ref[...], preferred_element_type=F32)
    mix = jnp.dot(m.astype(BF16), wout_ref[...], preferred_element_type=F32)

    h1 = _ln(alpha * h_ref[...] + g1_ref[...] * mix) * lng_ref[...] + lnb_ref[...]
    h1_ref[...] = h1
    f = _ln(h1) * (1.0 + sc2_ref[...]) + sh2_ref[...]
    f_ref[...] = f.astype(BF16)
    lg_ref[...] = jnp.dot(f, wr_ref[...], preferred_element_type=F32, precision=HIGHEST)


def _merge(ya, p, yc, h, mods, ln_g, ln_b, w, *, n_rows, tiles_per_batch, n_batch, alpha):
    d = D_MODEL
    tm = SUB_TILE
    row = lambda i: (i, 0)
    mod_row = lambda i: (jnp.minimum(i // tiles_per_batch, n_batch), 0, 0)
    pcol = lambda c, width: pl.BlockSpec((tm, width), lambda i: (i, c // width))
    full = lambda a: pl.BlockSpec(a.shape, lambda i: (0,) * a.ndim)
    mod_spec = pl.BlockSpec((None, 1, d), mod_row)
    vec = lambda a: a.reshape(1, d)
    weights = (w["glu"], w["gm"], w["da"], w["out"], w["gm_ws"], w["gm_bs"], w["router"])
    return pl.pallas_call(
        functools.partial(_merge_kernel, alpha=alpha),
        grid=(n_rows // tm,),
        in_specs=[pl.BlockSpec((tm, S5_WIDTH), row),
                  pcol(COL_ZU, GM_WIDTH), pcol(COL_ZV, GM_WIDTH),
                  pl.BlockSpec((tm, DA_V), row),
                  pcol(COL_GATE, d), pcol(COL_GATE + d, d), pcol(COL_GATE + 2 * d, d),
                  pl.BlockSpec((tm, d), row),
                  mod_spec, mod_spec, mod_spec,
                  pl.BlockSpec((1, d), lambda i: (0, 0)), pl.BlockSpec((1, d), lambda i: (0, 0))]
                 + [full(a) for a in weights],
        out_specs=[pl.BlockSpec((tm, d), row), pl.BlockSpec((tm, d), row), pl.BlockSpec((tm, LANES), row)],
        out_shape=[jax.ShapeDtypeStruct((n_rows, d), F32), jax.ShapeDtypeStruct((n_rows, d), BF16),
                   jax.ShapeDtypeStruct((n_rows, LANES), F32)],
        compiler_params=_cparams("arbitrary"),
        name="merge",
    )(ya, p, p, yc, p, p, p, h, mods["g1"], mods["sh2"], mods["sc2"], vec(ln_g), vec(ln_b), *weights)


def _router_kernel(lg_ref, b_ref, before_ref, lrow_ref, w_ref, lrow_t_ref, units_ref):
    tm = lg_ref.shape[0]
    per_group = N_EXPERTS // N_GROUPS
    neg = -jnp.inf
    logits = lg_ref[...].T[:N_EXPERTS]
    scores = _sigmoid(logits).reshape(N_GROUPS, per_group, tm)
    sel = scores + b_ref[...].reshape(N_GROUPS, per_group, 1)

    in_group = lax.broadcasted_iota(jnp.int32, sel.shape, 1)
    top1 = jnp.max(sel, axis=1, keepdims=True)
    first = jnp.min(jnp.where(sel == top1, in_group, per_group), axis=1, keepdims=True)
    top2 = jnp.max(jnp.where(in_group == first, neg, sel), axis=1, keepdims=True)
    gscore = top1 + top2

    gidx = lax.broadcasted_iota(jnp.int32, gscore.shape, 0)
    gsel = jnp.zeros(gscore.shape, jnp.bool_)
    for _ in range(TOPK_GROUPS):
        best = jnp.max(gscore, axis=0, keepdims=True)
        hit = gidx == jnp.min(jnp.where(gscore == best, gidx, N_GROUPS), axis=0, keepdims=True)
        gsel = gsel | hit
        gscore = jnp.where(hit, neg, gscore)

    eidx = lax.broadcasted_iota(jnp.int32, sel.shape, 0) * per_group + in_group
    cand = jnp.where(gsel, sel, neg)
    chosen = jnp.zeros(sel.shape, jnp.bool_)
    hits = []
    for _ in range(TOP_K):
        best = jnp.max(jnp.max(cand, axis=1, keepdims=True), axis=0, keepdims=True)
        at = jnp.where(cand == best, eidx, N_EXPERTS)
        hit = eidx == jnp.min(jnp.min(at, axis=1, keepdims=True), axis=0, keepdims=True)
        hits.append(hit)
        chosen = chosen | hit
        cand = jnp.where(hit, neg, cand)

    w = jnp.where(chosen, scores, 0.0)
    total = jnp.sum(jnp.sum(w, axis=1, keepdims=True), axis=0, keepdims=True)
    w = w / total * ROUTED_SCALE

    onehot = jnp.where(chosen, 1.0, 0.0).reshape(N_EXPERTS, tm).astype(BF16)
    rank = jnp.dot(onehot, before_ref[...], preferred_element_type=F32)
    count = jnp.dot(onehot, jnp.ones((tm, LANES), BF16), preferred_element_type=F32)
    units = jnp.floor((count + (UNIT - 1)) * (1.0 / UNIT))
    ei = lax.broadcasted_iota(jnp.int32, (N_EXPERTS, N_EXPERTS), 0)
    ej = lax.broadcasted_iota(jnp.int32, (N_EXPERTS, N_EXPERTS), 1)
    first_unit = jnp.dot(jnp.where(ej < ei, 1.0, 0.0).astype(BF16), units.astype(BF16),
                         preferred_element_type=F32)
    base = jnp.concatenate([first_unit * UNIT] * (tm // LANES), axis=1)
    pos = (base + rank).reshape(sel.shape)

    pick = lambda hit, val: jnp.sum(jnp.sum(jnp.where(hit, val, 0.0), axis=1, keepdims=True), axis=0)
    pad = jnp.zeros((LANES - TOP_K, tm), F32)
    lrow_t = jnp.concatenate([pick(hit, pos) for hit in hits] + [pad], axis=0)
    w_t = jnp.concatenate([pick(hit, w) for hit in hits] + [pad], axis=0)
    lrow_t_ref[...] = lrow_t[:TOP_K].astype(jnp.int32)
    lrow_ref[...] = lrow_t.T.astype(jnp.int32)
    w_ref[...] = w_t.T
    units_ref[...] = units.astype(jnp.int32)


def _route(logits, b_router):
    t = logits.shape[0]
    tm = SUB_TILE
    n_sub = t // tm
    before = jnp.triu(jnp.ones((tm, tm), F32), 1).astype(BF16)
    tok = pl.BlockSpec((tm, LANES), lambda i: (i, 0))
    pick = pl.BlockSpec((TOP_K, tm), lambda i: (0, i))
    return pl.pallas_call(
        _router_kernel,
        grid=(n_sub,),
        in_specs=[tok, pl.BlockSpec((N_EXPERTS, 1), lambda i: (0, 0)), pl.BlockSpec((tm, tm), lambda i: (0, 0))],
        out_specs=[tok, tok, pick, pl.BlockSpec((N_EXPERTS, LANES), lambda i: (i, 0))],
        out_shape=[jax.ShapeDtypeStruct((t, LANES), jnp.int32), jax.ShapeDtypeStruct((t, LANES), F32),
                   jax.ShapeDtypeStruct((TOP_K, t), jnp.int32),
                   jax.ShapeDtypeStruct((n_sub * N_EXPERTS, LANES), jnp.int32)],
        compiler_params=_cparams("arbitrary"),
        name="router",
    )(logits, b_router.astype(F32).reshape(N_EXPERTS, 1), before)


def _moe_layout(units, n_blocks):
    per_expert = jnp.sum(units, axis=0)
    padded = (per_expert + FFN_UNITS - 1) // FFN_UNITS * FFN_UNITS
    ends = jnp.cumsum(padded)
    starts = ends - padded
    goff = starts[None, :] + jnp.cumsum(units, axis=0) - units
    n_used = ends[-1] // FFN_UNITS
    blk = jnp.arange(n_blocks, dtype=jnp.int32)
    blk = jnp.minimum(blk, n_used - 1)
    first = blk * FFN_UNITS
    expert = jnp.minimum(jnp.searchsorted(ends, first, side="right"), N_EXPERTS - 1)
    valid = jnp.clip(per_expert[expert] - (first - starts[expert]), 0, FFN_UNITS)
    i32 = lambda a: a.astype(jnp.int32)
    return i32(goff.reshape(-1)), i32(expert), i32(valid), i32(n_used.reshape(1))


def _unit_rows(ref, unit):
    return ref.at[pl.ds(pl.multiple_of(unit * UNIT, UNIT), UNIT), :]


def _for_each_unit(units_ref, goff_ref, tile, fn):
    def per_expert(e, lo):
        n = units_ref[tile * N_EXPERTS + e]
        g = goff_ref[tile * N_EXPERTS + e]

        def per_unit(j, carry):
            fn(lo + j, g + j)
            return carry

        lax.fori_loop(0, n, per_unit, 0)
        return lo + n

    return lax.fori_loop(0, N_EXPERTS, per_expert, 0)


def _dispatch_kernel(units_ref, goff_ref, f_ref, lrow_t_ref, xs_hbm, buf_ref, sem_ref, ndma_ref):
    s = pl.program_id(0)
    last = pl.num_programs(0) - 1
    slot = s % 2

    def copy(slot, lu, gu):
        return pltpu.make_async_copy(_unit_rows(buf_ref.at[slot], lu), _unit_rows(xs_hbm, gu), sem_ref.at[slot])

    def drain(slot):
        def body(i, carry):
            copy(slot, 0, 0).wait()
            return carry
        lax.fori_loop(0, ndma_ref[slot], body, 0)

    @pl.when(s >= 2)
    def _():
        drain(slot)

    lt = lrow_t_ref[...]
    f = f_ref[...]
    chunk = 512
    for r0 in range(0, LOCAL_ROWS, chunk):
        row = lax.broadcasted_iota(jnp.int32, (chunk, SUB_TILE), 0) + r0
        hit = row == lt[0:1, :]
        for k in range(1, TOP_K):
            hit = hit | (row == lt[k:k + 1, :])
        onehot = jnp.where(hit, 1.0, 0.0).astype(BF16)
        buf_ref[slot, r0:r0 + chunk, :] = jnp.dot(onehot, f, preferred_element_type=F32).astype(BF16)

    ndma_ref[slot] = _for_each_unit(units_ref, goff_ref, s, lambda lu, gu: copy(slot, lu, gu).start())

    @pl.when(s == last)
    def _():
        drain(slot)

        @pl.when(s >= 1)
        def _():
            drain(1 - slot)


def _dispatch(f, lrow_t, units, goff, n_rows_out):
    t, d = f.shape
    grid_spec = pltpu.PrefetchScalarGridSpec(
        num_scalar_prefetch=2,
        grid=(t // SUB_TILE,),
        in_specs=[pl.BlockSpec((SUB_TILE, d), lambda s, u, g: (s, 0)),
                  pl.BlockSpec((TOP_K, SUB_TILE), lambda s, u, g: (0, s))],
        out_specs=pl.BlockSpec(memory_space=pl.ANY),
        scratch_shapes=[pltpu.VMEM((2, LOCAL_ROWS, d), BF16), pltpu.SemaphoreType.DMA((2,)),
                        pltpu.SMEM((2,), jnp.int32)])
    return pl.pallas_call(
        _dispatch_kernel,
        grid_spec=grid_spec,
        out_shape=jax.ShapeDtypeStruct((n_rows_out, d), BF16),
        compiler_params=_cparams("arbitrary"),
        name="moe_dispatch",
    )(units, goff, f, lrow_t)


def _ffn_kernel(expert_ref, valid_ref, nused_ref, x_ref, wgu_ref, wdn_ref, y_ref):
    b = pl.program_id(0)

    @pl.when(b < nused_ref[0])
    def _():
        rows = lax.broadcasted_iota(jnp.int32, (FFN_ROWS, 1), 0)
        x = x_ref[...]
        x = jnp.where(rows < valid_ref[b] * UNIT, x, jnp.zeros_like(x))
        hgu = jnp.dot(x, wgu_ref[...], preferred_element_type=F32)
        hid = _silu(hgu[:, :EXPERT_HIDDEN]) * hgu[:, EXPERT_HIDDEN:]
        y_ref[...] = jnp.dot(hid.astype(BF16), wdn_ref[...], preferred_element_type=F32).astype(BF16)


def _expert_ffn(xs, wgu, wdn, expert, valid, n_used):
    n_rows, d = xs.shape
    row_blk = lambda b, e, v, n: (jnp.minimum(b, n[0] - 1), 0)
    grid_spec = pltpu.PrefetchScalarGridSpec(
        num_scalar_prefetch=3,
        grid=(n_rows // FFN_ROWS,),
        in_specs=[pl.BlockSpec((FFN_ROWS, d), row_blk),
                  pl.BlockSpec((None, d, 2 * EXPERT_HIDDEN), lambda b, e, v, n: (e[b], 0, 0)),
                  pl.BlockSpec((None, EXPERT_HIDDEN, d), lambda b, e, v, n: (e[b], 0, 0))],
        out_specs=pl.BlockSpec((FFN_ROWS, d), row_blk))
    return pl.pallas_call(
        _ffn_kernel,
        grid_spec=grid_spec,
        out_shape=jax.ShapeDtypeStruct((n_rows, d), BF16),
        compiler_params=_cparams("arbitrary"),
        name="moe_ffn",
    )(expert, valid, n_used, xs, wgu, wdn)


def _combine_kernel(units_ref, goff_ref, ys_hbm, lrow_ref, w_ref, f_ref, sgu_ref, sdn_ref, h1_ref, g2_ref,
                    lng_ref, lnb_ref, o_ref, buf_ref, sem_ref, *, alpha):
    s = pl.program_id(0)
    n_tiles = pl.num_programs(0)
    slot = s % 2

    def copy(slot, lu, gu):
        return pltpu.make_async_copy(_unit_rows(ys_hbm, gu), _unit_rows(buf_ref.at[slot], lu), sem_ref.at[slot])

    def fetch(tile, slot):
        _for_each_unit(units_ref, goff_ref, tile, lambda lu, gu: copy(slot, lu, gu).start())

    @pl.when(s == 0)
    def _():
        buf_ref[...] = jnp.zeros(buf_ref.shape, BF16)
        fetch(0, 0)

    @pl.when(s + 1 < n_tiles)
    def _():
        fetch(s + 1, 1 - slot)

    n_units = lax.fori_loop(0, N_EXPERTS, lambda e, c: c + units_ref[s * N_EXPERTS + e], 0)

    def wait_one(i, carry):
        copy(slot, 0, 0).wait()
        return carry

    lax.fori_loop(0, n_units, wait_one, 0)

    lrow = lrow_ref[...]
    w = w_ref[...]
    y = jnp.zeros((SUB_TILE, D_MODEL), F32)
    chunk = 512
    for r0 in range(0, LOCAL_ROWS, chunk):
        col = lax.broadcasted_iota(jnp.int32, (SUB_TILE, chunk), 1) + r0
        pw = jnp.zeros((SUB_TILE, chunk), F32)
        for k in range(TOP_K):
            pw = jnp.where(lrow[:, k:k + 1] == col, w[:, k:k + 1], pw)
        hi = pw.astype(BF16)
        lo = (pw - hi.astype(F32)).astype(BF16)
        ys = buf_ref[slot, r0:r0 + chunk, :]
        y = y + jnp.dot(hi, ys, preferred_element_type=F32) + jnp.dot(lo, ys, preferred_element_type=F32)

    hgu = jnp.dot(f_ref[...], sgu_ref[...], preferred_element_type=F32)
    hid = _silu(hgu[:, :EXPERT_HIDDEN]) * hgu[:, EXPERT_HIDDEN:]
    y = y + jnp.dot(hid.astype(BF16), sdn_ref[...], preferred_element_type=F32)
    o_ref[...] = _ln(alpha * h1_ref[...] + g2_ref[...] * y) * lng_ref[...] + lnb_ref[...]


def _combine(ys, lrow, w, f, sgu, sdn, h1, g2, ln_g, ln_b, units, goff, *, tiles_per_batch, n_batch, alpha):
    t, d = f.shape
    tm = SUB_TILE
    row = lambda s, u, g: (s, 0)
    fixed = lambda s, u, g: (0, 0)
    mod_row = lambda s, u, g: (jnp.minimum(s // tiles_per_batch, n_batch), 0, 0)
    grid_spec = pltpu.PrefetchScalarGridSpec(
        num_scalar_prefetch=2,
        grid=(t // tm,),
        in_specs=[pl.BlockSpec(memory_space=pl.ANY),
                  pl.BlockSpec((tm, LANES), row), pl.BlockSpec((tm, LANES), row), pl.BlockSpec((tm, d), row),
                  pl.BlockSpec(sgu.shape, fixed), pl.BlockSpec(sdn.shape, fixed),
                  pl.BlockSpec((tm, d), row), pl.BlockSpec((None, 1, d), mod_row),
                  pl.BlockSpec((1, d), fixed), pl.BlockSpec((1, d), fixed)],
        out_specs=pl.BlockSpec((tm, d), row),
        scratch_shapes=[pltpu.VMEM((2, LOCAL_ROWS, d), BF16), pltpu.SemaphoreType.DMA((2,))])
    return pl.pallas_call(
        functools.partial(_combine_kernel, alpha=alpha),
        grid_spec=grid_spec,
        out_shape=jax.ShapeDtypeStruct((t, d), F32),
        compiler_params=_cparams("arbitrary"),
        name="moe_combine",
    )(units, goff, ys, lrow, w, f, sgu, sdn, h1, g2, ln_g.reshape(1, d), ln_b.reshape(1, d))


def _forward(x, c, ctx, c_ctx, w_mod, b_mod, w_in, s5_lam_re, s5_lam_im, s5_log_step, s5_b_re, s5_b_im,
             s5_c_re, s5_c_im, s5_d, gm_w_s, gm_b_s, da_lam, da_subln_g, w_glu_val, w_glu_gate,
             w_proj_gm, w_proj_da, w_out, ln1_g, ln1_b, ln2_g, ln2_b, w_router, b_router,
             w_exp_gate, w_exp_up, w_exp_down, w_sh_gate, w_sh_up, w_sh_down):
    n_batch, seq, d = x.shape
    ctx_len = ctx.shape[1]
    depth = w_mod.shape[0]
    n_lat = n_batch * seq
    alpha = (2 * depth) ** 0.25
    tiles_per_batch = seq // ROW_TILE
    sub_per_batch = seq // SUB_TILE
    assert d == D_MODEL and seq % ROW_TILE == 0 and (n_batch * ctx_len) % ROW_TILE == 0
    assert ctx_len % SUB_TILE == 0 and n_batch == 4

    cond = jnp.concatenate([c, c_ctx[None], jnp.zeros((8 - n_batch - 1, d), F32)], axis=0)
    mod = _modulation(cond, w_mod, b_mod)
    cos_t, sin_t = _rope_tables(seq)
    h = jnp.concatenate([x.reshape(n_lat, d), ctx.reshape(n_batch * ctx_len, d)], axis=0)

    for l in range(depth):
        last = l == depth - 1
        lam_init = 0.8 - 0.6 * math.exp(-0.3 * l)
        names = ("sh1", "sc1", "g1", "sh2", "sc2", "g2")
        mods = {k: mod[l, :, i * d:(i + 1) * d].reshape(8, 1, d) for i, k in enumerate(names)}

        wi = w_in[l]
        a_w, z_w, q_w, k_w, v_w, g_w = jnp.split(wi, (256, 768, 1280, 1792, 2304), axis=1)
        w_in_l = jnp.concatenate([g_w, q_w, k_w, v_w, a_w, z_w], axis=1).astype(BF16)
        p = _inproj(h, mods["sh1"], mods["sc1"], w_in_l, tiles_per_batch, n_batch)

        lf = da_lam[l].astype(F32)
        lam = (jnp.exp(jnp.sum(lf[0] * lf[1])) - jnp.exp(jnp.sum(lf[2] * lf[3])) + lam_init).reshape(1)
        q, k0, k1, v = _rope_pack(p, cos_t, sin_t, n_batch, seq, ctx_len)
        subln = da_subln_g[l].astype(F32).reshape(1, LANES)
        attn = functools.partial(_attention, lam, q, k0, k1, v, subln, out_scale=1.0 - lam_init)
        yc = attn(n_rows=n_lat, q_row0=0, tq=512, kv_len=ctx_len + seq, head_len=ctx_len, tk=512)
        if not last:
            yc_ctx = attn(n_rows=n_batch * ctx_len, q_row0=n_lat, tq=ctx_len, kv_len=ctx_len,
                          head_len=ctx_len, tk=ctx_len)
            yc = jnp.concatenate([yc, yc_ctx], axis=0)

        mats = _s5_matrices(s5_lam_re[l], s5_lam_im[l], s5_log_step[l], s5_b_re[l], s5_b_im[l],
                            s5_c_re[l], s5_c_im[l], s5_d[l])
        t_all = h.shape[0]
        u16 = p[:, COL_S5:COL_S5 + S5_WIDTH].reshape(t_all // S5_CHUNK, S5_CHUNK * S5_WIDTH)
        ya = _s5_mixer(u16, mats, n_batch=n_batch, lat_chunks=seq // S5_CHUNK,
                       ctx_chunks=ctx_len // S5_CHUNK).reshape(t_all, S5_WIDTH)

        n_rows = n_lat if last else h.shape[0]
        router_w = jnp.concatenate([w_router[l].astype(F32), jnp.zeros((d, LANES - N_EXPERTS), F32)], axis=1)
        merge_w = {
            "glu": jnp.concatenate([w_glu_val[l], w_glu_gate[l]], axis=1).astype(BF16),
            "gm": w_proj_gm[l].astype(BF16), "da": w_proj_da[l].astype(BF16), "out": w_out[l].astype(BF16),
            "gm_ws": gm_w_s[l].astype(BF16),
            "gm_bs": jnp.repeat(gm_b_s[l].astype(F32).T, GM_WIDTH // GM_HEADS, axis=1),
            "router": router_w,
        }
        h1, f, logits = _merge(ya, p, yc, h, mods, ln1_g[l], ln1_b[l], merge_w, n_rows=n_rows,
                               tiles_per_batch=sub_per_batch, n_batch=n_batch, alpha=alpha)

        lrow, w_tok, lrow_t, units_lanes = _route(logits, b_router[l])
        n_tiles = n_rows // SUB_TILE
        units = units_lanes[:, 0].reshape(n_tiles, N_EXPERTS)
        max_units = n_rows * TOP_K // UNIT + n_tiles * N_EXPERTS + N_EXPERTS * (FFN_UNITS - 1)
        n_blocks = max_units // FFN_UNITS + 1
        goff, blk_expert, blk_valid, n_used = _moe_layout(units, n_blocks)
        units = units.reshape(-1)
        wgu = jnp.concatenate([w_exp_gate[l], w_exp_up[l]], axis=2).astype(BF16)
        xs = _dispatch(f, lrow_t, units, goff, n_blocks * FFN_ROWS)
        ys = _expert_ffn(xs, wgu, w_exp_down[l].astype(BF16), blk_expert, blk_valid, n_used)
        sgu = jnp.concatenate([w_sh_gate[l], w_sh_up[l]], axis=1).astype(BF16)
        h = _combine(ys, lrow, w_tok, f, sgu, w_sh_down[l].astype(BF16), h1, mods["g2"], ln2_g[l], ln2_b[l],
                     units, goff, tiles_per_batch=sub_per_batch, n_batch=n_batch, alpha=alpha)

    return h[:n_lat].reshape(n_batch, seq, d)


def kernel(x, c, ctx, c_ctx, w_mod, b_mod, w_in, s5_lam_re, s5_lam_im, s5_log_step, s5_b_re, s5_b_im, s5_c_re, s5_c_im, s5_d, gm_w_s, gm_b_s, da_lam, da_subln_g, w_glu_val, w_glu_gate, w_proj_gm, w_proj_da, w_out, ln1_g, ln1_b, ln2_g, ln2_b, w_router, b_router, w_exp_gate, w_exp_up, w_exp_down, w_sh_gate, w_sh_up, w_sh_down):
    return _forward(x, c, ctx, c_ctx, w_mod, b_mod, w_in, s5_lam_re, s5_lam_im, s5_log_step, s5_b_re, s5_b_im,
                    s5_c_re, s5_c_im, s5_d, gm_w_s, gm_b_s, da_lam, da_subln_g, w_glu_val, w_glu_gate,
                    w_proj_gm, w_proj_da, w_out, ln1_g, ln1_b, ln2_g, ln2_b, w_router, b_router,
                    w_exp_gate, w_exp_up, w_exp_down, w_sh_gate, w_sh_up, w_sh_down)
```

```python
import functools
import math

import jax
import jax.numpy as jnp
from jax import lax
from jax.experimental import pallas as pl
from jax.experimental.pallas import tpu as pltpu

F32 = jnp.float32
BF16 = jnp.bfloat16
HIGHEST = lax.Precision.HIGHEST

D_MODEL = 1024
GRID_W = 64
S5_WIDTH = 256
S5_GROUP = 16
S5_GROUPS = S5_WIDTH // S5_GROUP
S5_STATE = 64
S5_CHUNK = 16
GM_WIDTH = 256
GM_HEADS = 4
GM_CHUNK = 128
DA_HEADS = 4
DA_HEAD_DIM = 64
DA_QK = DA_HEADS * 2 * DA_HEAD_DIM
DA_V = DA_HEADS * 2 * DA_HEAD_DIM
ROPE_BASE = 10000.0
N_BRANCH = 3
N_EXPERTS = 64
TOP_K = 8
N_GROUPS = 8
TOPK_GROUPS = 4
EXPERT_HIDDEN = 256
ROUTED_SCALE = 2.5
LN_EPS = 1e-5

COL_GATE = 0
COL_Q = N_BRANCH * D_MODEL
COL_K = COL_Q + DA_QK
COL_V = COL_K + DA_QK
COL_S5 = COL_V + DA_V
COL_ZU = COL_S5 + S5_WIDTH
COL_ZV = COL_ZU + GM_WIDTH
IN_WIDTH = COL_ZV + GM_WIDTH

LANES = 128
ROW_TILE = 1024
SUB_TILE = 256
VMEM_LIMIT = 48 * 1024 * 1024
UNIT = 16
LOCAL_ROWS = SUB_TILE * TOP_K + N_EXPERTS * UNIT
FFN_ROWS = 512
FFN_UNITS = FFN_ROWS // UNIT


def _cparams(*sem):
    return pltpu.CompilerParams(dimension_semantics=sem, vmem_limit_bytes=VMEM_LIMIT)


def _ln(x):
    mu = jnp.mean(x, -1, keepdims=True)
    xc = x - mu
    var = jnp.mean(xc * xc, -1, keepdims=True)
    return xc * lax.rsqrt(var + LN_EPS)


def _gelu(x):
    return 0.5 * x * (1.0 + jnp.tanh(math.sqrt(2.0 / math.pi) * (x + 0.044715 * (x * x * x))))


def _sigmoid(x):
    return 1.0 / (1.0 + jnp.exp(-x))


def _silu(x):
    return x * _sigmoid(x)


def _mod_kernel(c_ref, w_ref, b_ref, o_ref):
    s = _silu(c_ref[...])
    o_ref[...] = jnp.dot(s, w_ref[...], preferred_element_type=F32, precision=HIGHEST) + b_ref[...]


def _modulation(cond, w_mod, b_mod):
    depth, d, n = w_mod.shape
    tn = 1536
    return pl.pallas_call(
        _mod_kernel,
        grid=(depth, n // tn),
        in_specs=[pl.BlockSpec((8, d), lambda l, j: (0, 0)),
                  pl.BlockSpec((None, d, tn), lambda l, j: (l, 0, j)),
                  pl.BlockSpec((None, 1, tn), lambda l, j: (l, 0, j))],
        out_specs=pl.BlockSpec((None, 8, tn), lambda l, j: (l, 0, j)),
        out_shape=jax.ShapeDtypeStruct((depth, 8, n), F32),
        compiler_params=_cparams("arbitrary", "arbitrary"),
        name="modulation",
    )(cond, w_mod, b_mod.reshape(depth, 1, n))


def _inproj_kernel(h_ref, sh_ref, sc_ref, w_ref, o_ref, xn_ref):
    @pl.when(pl.program_id(1) == 0)
    def _():
        x = _ln(h_ref[...])
        xn_ref[...] = (x * (1.0 + sc_ref[...]) + sh_ref[...]).astype(BF16)

    o_ref[...] = jnp.dot(xn_ref[...], w_ref[...], preferred_element_type=F32).astype(BF16)


def _inproj(h, shift, scale, w, tiles_per_batch, n_batch):
    t, d = h.shape
    n = w.shape[1]
    tn = 768
    mod_row = lambda i, j: (jnp.minimum(i // tiles_per_batch, n_batch), 0, 0)
    return pl.pallas_call(
        _inproj_kernel,
        grid=(t // ROW_TILE, n // tn),
        in_specs=[pl.BlockSpec((ROW_TILE, d), lambda i, j: (i, 0)),
                  pl.BlockSpec((None, 1, d), mod_row),
                  pl.BlockSpec((None, 1, d), mod_row),
                  pl.BlockSpec((d, tn), lambda i, j: (0, j))],
        out_specs=pl.BlockSpec((ROW_TILE, tn), lambda i, j: (i, j)),
        out_shape=jax.ShapeDtypeStruct((t, n), BF16),
        scratch_shapes=[pltpu.VMEM((ROW_TILE, d), BF16)],
        compiler_params=_cparams("arbitrary", "arbitrary"),
        name="inproj",
    )(h, shift, scale, w)


def _rope_kernel(q_ref, k_ref, v_ref, cos_ref, sin_ref, qo_ref, k0_ref, k1_ref, vo_ref, *, n_lat_tiles):
    is_lat = pl.program_id(0) < n_lat_tiles
    shape = q_ref.shape
    lane = lax.broadcasted_iota(jnp.int32, shape, 1)
    upper16 = (lane % 32) >= 16
    map1 = (lane % LANES) >= DA_HEAD_DIM
    cos = jnp.where(is_lat, cos_ref[...], 1.0)
    sin = jnp.where(is_lat, sin_ref[...], 0.0)

    def rope(x):
        partner = jnp.where(upper16, pltpu.roll(x, 16, 1), pltpu.roll(x, shape[1] - 16, 1))
        return x * cos + partner * sin

    q = rope(q_ref[...].astype(F32)) * (DA_HEAD_DIM ** -0.5)
    k = rope(k_ref[...].astype(F32))
    qo_ref[...] = q.astype(BF16)
    k0_ref[...] = jnp.where(map1, 0.0, k).astype(BF16)
    k1_ref[...] = jnp.where(map1, k, 0.0).astype(BF16)
    vo_ref[...] = v_ref[...]


def _rope_pack(p, cos_t, sin_t, n_batch, seq, ctx):
    t = p.shape[0]
    nl = seq // SUB_TILE
    nc = ctx // SUB_TILE
    n_lat_tiles = n_batch * nl

    def kv_map(i):
        j = i - n_lat_tiles
        b = jnp.where(i < n_lat_tiles, i // nl, j // nc)
        blk = jnp.where(i < n_lat_tiles, nc + i % nl, j % nc)
        return (b, blk, 0)

    tab_map = lambda i: (jnp.where(i < n_lat_tiles, i % nl, 0), 0)
    col = lambda c: (lambda i: (i, c // DA_QK))
    kv_shape = jax.ShapeDtypeStruct((n_batch, ctx + seq, DA_QK), BF16)
    return pl.pallas_call(
        functools.partial(_rope_kernel, n_lat_tiles=n_lat_tiles),
        grid=(t // SUB_TILE,),
        in_specs=[pl.BlockSpec((SUB_TILE, DA_QK), col(COL_Q)),
                  pl.BlockSpec((SUB_TILE, DA_QK), col(COL_K)),
                  pl.BlockSpec((SUB_TILE, DA_V), col(COL_V)),
                  pl.BlockSpec((SUB_TILE, DA_QK), tab_map),
                  pl.BlockSpec((SUB_TILE, DA_QK), tab_map)],
        out_specs=[pl.BlockSpec((SUB_TILE, DA_QK), lambda i: (i, 0)),
                   pl.BlockSpec((None, SUB_TILE, DA_QK), kv_map),
                   pl.BlockSpec((None, SUB_TILE, DA_QK), kv_map),
                   pl.BlockSpec((None, SUB_TILE, DA_V), kv_map)],
        out_shape=[jax.ShapeDtypeStruct((t, DA_QK), BF16), kv_shape, kv_shape, kv_shape],
        compiler_params=_cparams("arbitrary"),
        name="rope_pack",
    )(p, p, p, cos_t, sin_t)


def _rope_tables(seq):
    pos = jnp.arange(seq)
    row = (pos // GRID_W).astype(F32)[:, None]
    colp = (pos % GRID_W).astype(F32)[:, None]
    axis_dim = DA_HEAD_DIM // 2
    inv_freq = ROPE_BASE ** (-jnp.arange(0, axis_dim, 2, dtype=F32) / axis_dim)
    ang_r = row * inv_freq
    ang_c = colp * inv_freq
    ang = jnp.concatenate([ang_r, ang_r, ang_c, ang_c], -1)
    sign = jnp.concatenate([-jnp.ones((16,), F32), jnp.ones((16,), F32)] * 2)
    reps = DA_QK // DA_HEAD_DIM
    return jnp.tile(jnp.cos(ang), (1, reps)), jnp.tile(jnp.sin(ang) * sign, (1, reps))


def _attn_kernel(lam_ref, q_ref, k0_ref, k1_ref, v_ref, g_ref, o_ref, *, head_len, n_tail, tk, out_scale):
    q = q_ref[...]
    tq = q.shape[0]
    nt = (((1,), (1,)), ((), ()))

    def online(state, kc, vc):
        m, l, acc = state
        s = lax.dot_general(q, kc, nt, preferred_element_type=F32)
        m_new = jnp.maximum(m, jnp.max(s, -1, keepdims=True))
        alpha = jnp.exp(m - m_new)
        p = jnp.exp(s - m_new)
        l = alpha * l + jnp.sum(p, -1, keepdims=True)
        acc = alpha * acc + jnp.dot(p.astype(BF16), vc, preferred_element_type=F32)
        return m_new, l, acc

    def step(carry, start, size):
        st0, st1 = carry
        vc = v_ref[pl.ds(start, size), :]
        st0 = online(st0, k0_ref[pl.ds(start, size), :], vc)
        st1 = online(st1, k1_ref[pl.ds(start, size), :], vc)
        return st0, st1

    init = (jnp.full((tq, 1), -jnp.inf, F32), jnp.zeros((tq, 1), F32), jnp.zeros((tq, LANES), F32))
    carry = step((init, init), 0, head_len)
    if n_tail:
        carry = lax.fori_loop(
            0, n_tail, lambda j, c: step(c, pl.multiple_of(head_len + j * tk, math.gcd(head_len, tk)), tk), carry)
    (_, l0, a0), (_, l1, a1) = carry
    o = a0 / l0 - lam_ref[0] * (a1 / l1)
    o = o * lax.rsqrt(jnp.mean(o * o, -1, keepdims=True) + LN_EPS)
    o_ref[...] = (o * g_ref[...] * out_scale).astype(BF16)


def _attention(lam, q, k0, k1, v, subln_g, *, n_rows, q_row0, tq, kv_len, head_len, tk, out_scale):
    n_batch = k0.shape[0]
    per_batch = n_rows // n_batch // tq
    q0 = q_row0 // tq
    n_tail = (kv_len - head_len) // tk
    kv_spec = pl.BlockSpec((None, kv_len, LANES), lambda b, h, i: (b, 0, h))
    return pl.pallas_call(
        functools.partial(_attn_kernel, head_len=head_len, n_tail=n_tail, tk=tk, out_scale=out_scale),
        grid=(n_batch, DA_HEADS, per_batch),
        in_specs=[pl.BlockSpec(memory_space=pltpu.SMEM),
                  pl.BlockSpec((tq, LANES), lambda b, h, i: (q0 + b * per_batch + i, h)),
                  kv_spec, kv_spec, kv_spec,
                  pl.BlockSpec((1, LANES), lambda b, h, i: (0, 0))],
        out_specs=pl.BlockSpec((tq, LANES), lambda b, h, i: (b * per_batch + i, h)),
        out_shape=jax.ShapeDtypeStruct((n_rows, DA_V), BF16),
        compiler_params=_cparams("arbitrary", "arbitrary", "arbitrary"),
        name="diff_attention",
    )(lam, q, k0, k1, v, subln_g)


def _s5_in_kernel(x_ref, b_ref, u_ref, z_ref, stage_ref):
    @pl.when(pl.program_id(1) == 0)
    def _():
        x = x_ref[...].astype(F32)
        for half in range(S5_WIDTH // LANES):
            stage_ref[half] = x[:, half * LANES:(half + 1) * LANES]
        for s in range(S5_CHUNK):
            for half in range(S5_WIDTH // LANES):
                col = s * S5_WIDTH + half * LANES
                u_ref[:, col:col + LANES] = (
                    stage_ref[half, pl.ds(s, u_ref.shape[0], stride=S5_CHUNK), :].astype(BF16))

    z_ref[...] = jnp.dot(u_ref[...], b_ref[...], preferred_element_type=F32)


def _s5_scan_kernel(z_ref, a1_ref, a2_ref, p_ref, *, n_batch, lat_chunks, ctx_chunks):
    reverse = pl.program_id(0) == 1
    a1 = a1_ref[...]
    a2 = a2_ref[...]
    width = a1.shape[1]

    def swap_halves(s):
        return jnp.concatenate(
            [pltpu.roll(s[:, j * LANES:(j + 1) * LANES], LANES // 2, 1) for j in range(width // LANES)], axis=1)

    def run(base, count, carry):
        def body(i, st):
            s, ssw = st
            row = base + jnp.where(reverse, count - 1 - i, i)
            z = z_ref[pl.ds(row, 1), :]
            p_ref[pl.ds(row, 1), :] = s
            return a1 * s + a2 * ssw + z, a1 * ssw - a2 * s + swap_halves(z)
        return lax.fori_loop(0, count, body, carry, unroll=4)

    zero = jnp.zeros(a1.shape, F32)
    for b in range(n_batch):
        st = run(n_batch * lat_chunks + b * ctx_chunks, ctx_chunks, (zero, zero))
        run(b * lat_chunks, lat_chunks, st)


def _s5_out_kernel(u_ref, p_ref, t_ref, ct_ref, y_ref, stage_ref):
    y = jnp.dot(u_ref[...], t_ref[...], preferred_element_type=F32)
    y = y + lax.dot_general(p_ref[...].astype(BF16), ct_ref[...], (((1,), (1,)), ((), ())),
                            preferred_element_type=F32)
    y = _gelu(y)
    per_step = y.shape[1] // S5_WIDTH
    first = pl.program_id(1) * per_step
    halves = S5_WIDTH // LANES
    for k in range(per_step):
        for half in range(halves):
            col = k * S5_WIDTH + half * LANES
            stage_ref[half, pl.ds(first + k, y.shape[0], stride=S5_CHUNK), :] = y[:, col:col + LANES]

    @pl.when(pl.program_id(1) == pl.num_programs(1) - 1)
    def _():
        for half in range(halves):
            y_ref[:, half * LANES:(half + 1) * LANES] = stage_ref[half]


def _s5_mixer(p, mats, *, n_batch, lat_chunks, ctx_chunks):
    t_all = p.shape[0]
    r = t_all // S5_CHUNK
    w = S5_CHUNK * S5_WIDTH
    tr = r // 4
    state_w = mats["b"].shape[1]
    tn = 1024
    u, z = pl.pallas_call(
        _s5_in_kernel,
        grid=(r // tr, state_w // tn),
        in_specs=[pl.BlockSpec((tr * S5_CHUNK, S5_WIDTH), lambda i, j: (i, COL_S5 // S5_WIDTH)),
                  pl.BlockSpec((w, tn), lambda i, j: (0, j))],
        out_specs=[pl.BlockSpec((tr, w), lambda i, j: (i, 0)),
                   pl.BlockSpec((tr, tn), lambda i, j: (i, j))],
        out_shape=[jax.ShapeDtypeStruct((r, w), BF16), jax.ShapeDtypeStruct((r, state_w), F32)],
        scratch_shapes=[pltpu.VMEM((S5_WIDTH // LANES, tr * S5_CHUNK, LANES), F32)],
        compiler_params=_cparams("arbitrary", "arbitrary"),
        name="s5_in",
    )(p, mats["b"])
    half = state_w // 2
    coef = pl.BlockSpec((None, 1, half), lambda d: (d, 0, 0))
    prev = pl.pallas_call(
        functools.partial(_s5_scan_kernel, n_batch=n_batch, lat_chunks=lat_chunks, ctx_chunks=ctx_chunks),
        grid=(2,),
        in_specs=[pl.BlockSpec((r, half), lambda d: (0, d)), coef, coef],
        out_specs=pl.BlockSpec((r, half), lambda d: (0, d)),
        out_shape=jax.ShapeDtypeStruct((r, state_w), F32),
        compiler_params=_cparams("arbitrary"),
        name="s5_scan",
    )(z, mats["a1"], mats["a2"])
    tn = 512
    return pl.pallas_call(
        _s5_out_kernel,
        grid=(r // tr, w // tn),
        in_specs=[pl.BlockSpec((tr, w), lambda i, j: (i, 0)),
                  pl.BlockSpec((tr, state_w), lambda i, j: (i, 0)),
                  pl.BlockSpec((w, tn), lambda i, j: (0, j)),
                  pl.BlockSpec((tn, state_w), lambda i, j: (j, 0))],
        out_specs=pl.BlockSpec((tr * S5_CHUNK, S5_WIDTH), lambda i, j: (i, 0)),
        out_shape=jax.ShapeDtypeStruct((t_all, S5_WIDTH), F32),
        scratch_shapes=[pltpu.VMEM((S5_WIDTH // LANES, tr * S5_CHUNK, LANES), F32)],
        compiler_params=_cparams("arbitrary", "arbitrary"),
        name="s5_out",
    )(u, prev, mats["t"], mats["ct"])


def _s5_matrices(lam_re, lam_im, log_step, b_re, b_im, c_re, c_im, d_skip):
    n = S5_CHUNK
    dt = jnp.exp(log_step.astype(F32))[..., None]
    lr, li = lam_re.astype(F32), lam_im.astype(F32)
    mag = jnp.exp(lr * dt)
    a_re, a_im = mag * jnp.cos(li * dt), mag * jnp.sin(li * dt)
    den = lr * lr + li * li
    n_re = a_re - 1.0
    z_re = (n_re * lr + a_im * li) / den
    z_im = (a_im * lr - n_re * li) / den
    br, bi = b_re.astype(F32), b_im.astype(F32)
    bb_re = z_re[..., None] * br - z_im[..., None] * bi
    bb_im = z_re[..., None] * bi + z_im[..., None] * br
    j = jnp.arange(n + 1, dtype=F32)[:, None, None, None]
    pmag = jnp.exp(lr * dt * j)
    pw_re, pw_im = pmag * jnp.cos(li * dt * j), pmag * jnp.sin(li * dt * j)
    cr, ci = c_re.astype(F32), c_im.astype(F32)
    g, p, c = S5_GROUPS, S5_STATE, S5_GROUP

    def cmul(xr, xi, yr, yi):
        return xr * yr - xi * yi, xr * yi + xi * yr

    def in_mat(direction, powers):
        er = pw_re[powers, direction][:, :, :, None]
        ei = pw_im[powers, direction][:, :, :, None]
        xr, xi = cmul(er, ei, bb_re[direction][None], bb_im[direction][None])
        m = jnp.concatenate([xr, xi], axis=2)
        return m.transpose(1, 0, 3, 2).reshape(g, n * c, 2 * p)

    def out_mat(direction, powers):
        er = pw_re[powers, direction][:, :, None, :]
        ei = pw_im[powers, direction][:, :, None, :]
        wr, wi = cmul(cr[direction][None], ci[direction][None], er, ei)
        m = jnp.concatenate([wr, -wi], axis=3)
        return m.transpose(1, 3, 0, 2).reshape(g, 2 * p, n * c)

    def toeplitz(direction):
        er = pw_re[:n, direction][:, :, None, :, None]
        ei = pw_im[:n, direction][:, :, None, :, None]
        wr, wi = cmul(cr[direction][None, :, :, :, None], ci[direction][None, :, :, :, None], er, ei)
        k = jnp.sum(wr * bb_re[direction][None, :, None] - wi * bb_im[direction][None, :, None], axis=3)
        return k

    steps = jnp.arange(n)
    kf, kb = toeplitz(0), toeplitz(1)
    skip = d_skip.astype(F32).reshape(g, c)
    k0 = kf[0] + kb[0] + jnp.eye(c, dtype=F32)[None] * skip[:, :, None]
    by_lag = jnp.concatenate([kb[:0:-1], k0[None], kf[1:]], axis=0)
    eye_g = jnp.eye(g, dtype=F32)
    lag_blocks = jnp.einsum("mgca,gh->mgahc", by_lag, eye_g).reshape(2 * n - 1, g * c, g * c)
    t_dense = lag_blocks[steps[None, :] - steps[:, None] + n - 1]
    t_dense = t_dense.transpose(0, 2, 1, 3).reshape(n * g * c, n * g * c)

    def coef(direction):
        ar, ai = pw_re[n, direction], pw_im[n, direction]
        return (jnp.concatenate([ar, ar], -1).reshape(-1), jnp.concatenate([-ai, ai], -1).reshape(-1))

    def dense_in(m):
        m = m.reshape(g, n, c, -1)
        return jnp.einsum("gscq,gh->sgchq", m, eye_g).reshape(n * g * c, -1)

    def dense_out_t(m):
        q = m.shape[1]
        return jnp.einsum("gqtc,gh->thcgq", m.reshape(g, q, n, c), eye_g).reshape(n * g * c, g * q)

    a1f, a2f = coef(0)
    a1b, a2b = coef(1)
    return {
        "b": jnp.concatenate([dense_in(in_mat(0, n - 1 - steps)), dense_in(in_mat(1, steps))], 1).astype(BF16),
        "ct": jnp.concatenate([dense_out_t(out_mat(0, steps + 1)), dense_out_t(out_mat(1, n - steps))],
                              1).astype(BF16),
        "t": t_dense.astype(BF16),
        "a1": jnp.stack([a1f, a1b])[:, None, :], "a2": jnp.stack([a2f, a2b])[:, None, :],
    }


def _merge_kernel(ya_ref, zu_ref, zv_ref, yc_ref, ga_ref, gb_ref, gc_ref, h_ref,
                  g1_ref, sh2_ref, sc2_ref, lng_ref, lnb_ref,
                  wglu_ref, wgm_ref, wda_ref, wout_ref, ws_ref, bs_ref, wr_ref,
                  h1_ref, f_ref, lg_ref, *, alpha):
    f32 = lambda ref: ref[...].astype(F32)
    glu = jnp.dot(ya_ref[...].astype(BF16), wglu_ref[...], preferred_element_type=F32)
    branch_a = glu[:, :D_MODEL] * _sigmoid(glu[:, D_MODEL:])

    u = _gelu(f32(zu_ref))
    v = _ln(_gelu(f32(zv_ref))).astype(BF16)
    head = lax.broadcasted_iota(jnp.int32, (GM_CHUNK, GM_WIDTH), 1) // (GM_WIDTH // GM_HEADS)
    parts = []
    for ck in range(v.shape[0] // GM_CHUNK):
        vc = v[ck * GM_CHUNK:(ck + 1) * GM_CHUNK]
        s = bs_ref[...]
        for hd in range(GM_HEADS):
            s = s + jnp.dot(ws_ref[hd], jnp.where(head == hd, vc, jnp.zeros_like(vc)),
                            preferred_element_type=F32)
        parts.append(s)
    yb = (u * jnp.concatenate(parts, axis=0)).astype(BF16)

    m = _sigmoid(f32(ga_ref)) * branch_a
    m = m + _sigmoid(f32(gb_ref)) * jnp.dot(yb, wgm_ref[...], preferred_element_type=F32)
    m = m + _sigmoid(f32(gc_ref)) * jnp.dot(yc_ref[...], wda_ref[...], preferred_element_type=F32)
    mix = jnp.dot(m.astype(BF16), wout_ref[...], preferred_element_type=F32)

    h1 = _ln(alpha * h_ref[...] + g1_ref[...] * mix) * lng_ref[...] + lnb_ref[...]
    h1_ref[...] = h1
    f = _ln(h1) * (1.0 + sc2_ref[...]) + sh2_ref[...]
    f_ref[...] = f.astype(BF16)
    lg_ref[...] = jnp.dot(f, wr_ref[...], preferred_element_type=F32, precision=HIGHEST)


def _merge(ya, p, yc, h, mods, ln_g, ln_b, w, *, n_rows, tiles_per_batch, n_batch, alpha):
    d = D_MODEL
    tm = SUB_TILE
    row = lambda i: (i, 0)
    mod_row = lambda i: (jnp.minimum(i // tiles_per_batch, n_batch), 0, 0)
    pcol = lambda c, width: pl.BlockSpec((tm, width), lambda i: (i, c // width))
    full = lambda a: pl.BlockSpec(a.shape, lambda i: (0,) * a.ndim)
    mod_spec = pl.BlockSpec((None, 1, d), mod_row)
    vec = lambda a: a.reshape(1, d)
    weights = (w["glu"], w["gm"], w["da"], w["out"], w["gm_ws"], w["gm_bs"], w["router"])
    return pl.pallas_call(
        functools.partial(_merge_kernel, alpha=alpha),
        grid=(n_rows // tm,),
        in_specs=[pl.BlockSpec((tm, S5_WIDTH), row),
                  pcol(COL_ZU, GM_WIDTH), pcol(COL_ZV, GM_WIDTH),
                  pl.BlockSpec((tm, DA_V), row),
                  pcol(COL_GATE, d), pcol(COL_GATE + d, d), pcol(COL_GATE + 2 * d, d),
                  pl.BlockSpec((tm, d), row),
                  mod_spec, mod_spec, mod_spec,
                  pl.BlockSpec((1, d), lambda i: (0, 0)), pl.BlockSpec((1, d), lambda i: (0, 0))]
                 + [full(a) for a in weights],
        out_specs=[pl.BlockSpec((tm, d), row), pl.BlockSpec((tm, d), row), pl.BlockSpec((tm, LANES), row)],
        out_shape=[jax.ShapeDtypeStruct((n_rows, d), F32), jax.ShapeDtypeStruct((n_rows, d), BF16),
                   jax.ShapeDtypeStruct((n_rows, LANES), F32)],
        compiler_params=_cparams("arbitrary"),
        name="merge",
    )(ya, p, p, yc, p, p, p, h, mods["g1"], mods["sh2"], mods["sc2"], vec(ln_g), vec(ln_b), *weights)


def _router_kernel(lg_ref, b_ref, before_ref, lrow_ref, w_ref, lrow_t_ref, units_ref):
    tm = lg_ref.shape[0]
    per_group = N_EXPERTS // N_GROUPS
    neg = -jnp.inf
    logits = lg_ref[...].T[:N_EXPERTS]
    scores = _sigmoid(logits).reshape(N_GROUPS, per_group, tm)
    sel = scores + b_ref[...].reshape(N_GROUPS, per_group, 1)

    in_group = lax.broadcasted_iota(jnp.int32, sel.shape, 1)
    top1 = jnp.max(sel, axis=1, keepdims=True)
    first = jnp.min(jnp.where(sel == top1, in_group, per_group), axis=1, keepdims=True)
    top2 = jnp.max(jnp.where(in_group == first, neg, sel), axis=1, keepdims=True)
    gscore = top1 + top2

    gidx = lax.broadcasted_iota(jnp.int32, gscore.shape, 0)
    gsel = jnp.zeros(gscore.shape, jnp.bool_)
    for _ in range(TOPK_GROUPS):
        best = jnp.max(gscore, axis=0, keepdims=True)
        hit = gidx == jnp.min(jnp.where(gscore == best, gidx, N_GROUPS), axis=0, keepdims=True)
        gsel = gsel | hit
        gscore = jnp.where(hit, neg, gscore)

    eidx = lax.broadcasted_iota(jnp.int32, sel.shape, 0) * per_group + in_group
    cand = jnp.where(gsel, sel, neg)
    chosen = jnp.zeros(sel.shape, jnp.bool_)
    hits = []
    for _ in range(TOP_K):
        best = jnp.max(jnp.max(cand, axis=1, keepdims=True), axis=0, keepdims=True)
        at = jnp.where(cand == best, eidx, N_EXPERTS)
        hit = eidx == jnp.min(jnp.min(at, axis=1, keepdims=True), axis=0, keepdims=True)
        hits.append(hit)
        chosen = chosen | hit
        cand = jnp.where(hit, neg, cand)

    w = jnp.where(chosen, scores, 0.0)
    total = jnp.sum(jnp.sum(w, axis=1, keepdims=True), axis=0, keepdims=True)
    w = w / total * ROUTED_SCALE

    onehot = jnp.where(chosen, 1.0, 0.0).reshape(N_EXPERTS, tm).astype(BF16)
    rank = jnp.dot(onehot, before_ref[...], preferred_element_type=F32)
    count = jnp.dot(onehot, jnp.ones((tm, LANES), BF16), preferred_element_type=F32)
    units = jnp.floor((count + (UNIT - 1)) * (1.0 / UNIT))
    ei = lax.broadcasted_iota(jnp.int32, (N_EXPERTS, N_EXPERTS), 0)
    ej = lax.broadcasted_iota(jnp.int32, (N_EXPERTS, N_EXPERTS), 1)
    first_unit = jnp.dot(jnp.where(ej < ei, 1.0, 0.0).astype(BF16), units.astype(BF16),
                         preferred_element_type=F32)
    base = jnp.concatenate([first_unit * UNIT] * (tm // LANES), axis=1)
    pos = (base + rank).reshape(sel.shape)

    pick = lambda hit, val: jnp.sum(jnp.sum(jnp.where(hit, val, 0.0), axis=1, keepdims=True), axis=0)
    pad = jnp.zeros((LANES - TOP_K, tm), F32)
    lrow_t = jnp.concatenate([pick(hit, pos) for hit in hits] + [pad], axis=0)
    w_t = jnp.concatenate([pick(hit, w) for hit in hits] + [pad], axis=0)
    lrow_t_ref[...] = lrow_t[:TOP_K].astype(jnp.int32)
    lrow_ref[...] = lrow_t.T.astype(jnp.int32)
    w_ref[...] = w_t.T
    units_ref[...] = units.astype(jnp.int32)


def _route(logits, b_router):
    t = logits.shape[0]
    tm = SUB_TILE
    n_sub = t // tm
    before = jnp.triu(jnp.ones((tm, tm), F32), 1).astype(BF16)
    tok = pl.BlockSpec((tm, LANES), lambda i: (i, 0))
    pick = pl.BlockSpec((TOP_K, tm), lambda i: (0, i))
    return pl.pallas_call(
        _router_kernel,
        grid=(n_sub,),
        in_specs=[tok, pl.BlockSpec((N_EXPERTS, 1), lambda i: (0, 0)), pl.BlockSpec((tm, tm), lambda i: (0, 0))],
        out_specs=[tok, tok, pick, pl.BlockSpec((N_EXPERTS, LANES), lambda i: (i, 0))],
        out_shape=[jax.ShapeDtypeStruct((t, LANES), jnp.int32), jax.ShapeDtypeStruct((t, LANES), F32),
                   jax.ShapeDtypeStruct((TOP_K, t), jnp.int32),
                   jax.ShapeDtypeStruct((n_sub * N_EXPERTS, LANES), jnp.int32)],
        compiler_params=_cparams("arbitrary"),
        name="router",
    )(logits, b_router.astype(F32).reshape(N_EXPERTS, 1), before)


def _moe_layout(units, n_blocks):
    per_expert = jnp.sum(units, axis=0)
    padded = (per_expert + FFN_UNITS - 1) // FFN_UNITS * FFN_UNITS
    ends = jnp.cumsum(padded)
    starts = ends - padded
    goff = starts[None, :] + jnp.cumsum(units, axis=0) - units
    n_used = ends[-1] // FFN_UNITS
    blk = jnp.arange(n_blocks, dtype=jnp.int32)
    blk = jnp.minimum(blk, n_used - 1)
    first = blk * FFN_UNITS
    expert = jnp.minimum(jnp.sum(ends[None, :] <= first[:, None], axis=1), N_EXPERTS - 1)
    valid = jnp.clip(per_expert[expert] - (first - starts[expert]), 0, FFN_UNITS)
    i32 = lambda a: a.astype(jnp.int32)
    return i32(goff.reshape(-1)), i32(expert), i32(valid), i32(n_used.reshape(1))


def _unit_rows(ref, unit):
    return ref.at[pl.ds(pl.multiple_of(unit * UNIT, UNIT), UNIT), :]


def _for_each_unit(units_ref, goff_ref, tile, fn):
    def per_expert(e, lo):
        n = units_ref[tile * N_EXPERTS + e]
        g = goff_ref[tile * N_EXPERTS + e]

        def per_unit(j, carry):
            fn(lo + j, g + j)
            return carry

        lax.fori_loop(0, n, per_unit, 0)
        return lo + n

    return lax.fori_loop(0, N_EXPERTS, per_expert, 0)


def _dispatch_kernel(units_ref, goff_ref, f_ref, lrow_t_ref, xs_hbm, buf_ref, sem_ref, ndma_ref):
    s = pl.program_id(0)
    last = pl.num_programs(0) - 1
    slot = s % 2

    def copy(slot, lu, gu):
        return pltpu.make_async_copy(_unit_rows(buf_ref.at[slot], lu), _unit_rows(xs_hbm, gu), sem_ref.at[slot])

    def drain(slot):
        def body(i, carry):
            copy(slot, 0, 0).wait()
            return carry
        lax.fori_loop(0, ndma_ref[slot], body, 0)

    @pl.when(s >= 2)
    def _():
        drain(slot)

    lt = lrow_t_ref[...]
    f = f_ref[...]
    n_units = lax.fori_loop(0, N_EXPERTS, lambda e, c: c + units_ref[s * N_EXPERTS + e], 0)
    chunk = 512
    for r0 in range(0, LOCAL_ROWS, chunk):
        @pl.when(r0 < n_units * UNIT)
        def _():
            row = lax.broadcasted_iota(jnp.int32, (chunk, SUB_TILE), 0) + r0
            hit = row == lt[0:1, :]
            for k in range(1, TOP_K):
                hit = hit | (row == lt[k:k + 1, :])
            onehot = jnp.where(hit, 1.0, 0.0).astype(BF16)
            buf_ref[slot, r0:r0 + chunk, :] = jnp.dot(onehot, f, preferred_element_type=F32).astype(BF16)

    ndma_ref[slot] = _for_each_unit(units_ref, goff_ref, s, lambda lu, gu: copy(slot, lu, gu).start())

    @pl.when(s == last)
    def _():
        drain(slot)

        @pl.when(s >= 1)
        def _():
            drain(1 - slot)


def _dispatch(f, lrow_t, units, goff, n_rows_out):
    t, d = f.shape
    grid_spec = pltpu.PrefetchScalarGridSpec(
        num_scalar_prefetch=2,
        grid=(t // SUB_TILE,),
        in_specs=[pl.BlockSpec((SUB_TILE, d), lambda s, u, g: (s, 0)),
                  pl.BlockSpec((TOP_K, SUB_TILE), lambda s, u, g: (0, s))],
        out_specs=pl.BlockSpec(memory_space=pl.ANY),
        scratch_shapes=[pltpu.VMEM((2, LOCAL_ROWS, d), BF16), pltpu.SemaphoreType.DMA((2,)),
                        pltpu.SMEM((2,), jnp.int32)])
    return pl.pallas_call(
        _dispatch_kernel,
        grid_spec=grid_spec,
        out_shape=jax.ShapeDtypeStruct((n_rows_out, d), BF16),
        compiler_params=_cparams("arbitrary"),
        name="moe_dispatch",
    )(units, goff, f, lrow_t)


def _ffn_kernel(expert_ref, valid_ref, nused_ref, x_ref, wgu_ref, wdn_ref, y_ref):
    b = pl.program_id(0)

    @pl.when(b < nused_ref[0])
    def _():
        rows = lax.broadcasted_iota(jnp.int32, (FFN_ROWS, 1), 0)
        x = x_ref[...]
        x = jnp.where(rows < valid_ref[b] * UNIT, x, jnp.zeros_like(x))
        hgu = jnp.dot(x, wgu_ref[...], preferred_element_type=F32)
        hid = _silu(hgu[:, :EXPERT_HIDDEN]) * hgu[:, EXPERT_HIDDEN:]
        y_ref[...] = jnp.dot(hid.astype(BF16), wdn_ref[...], preferred_element_type=F32).astype(BF16)


def _expert_ffn(xs, wgu, wdn, expert, valid, n_used):
    n_rows, d = xs.shape
    row_blk = lambda b, e, v, n: (jnp.minimum(b, n[0] - 1), 0)
    grid_spec = pltpu.PrefetchScalarGridSpec(
        num_scalar_prefetch=3,
        grid=(n_rows // FFN_ROWS,),
        in_specs=[pl.BlockSpec((FFN_ROWS, d), row_blk),
                  pl.BlockSpec((None, d, 2 * EXPERT_HIDDEN), lambda b, e, v, n: (e[b], 0, 0)),
                  pl.BlockSpec((None, EXPERT_HIDDEN, d), lambda b, e, v, n: (e[b], 0, 0))],
        out_specs=pl.BlockSpec((FFN_ROWS, d), row_blk))
    return pl.pallas_call(
        _ffn_kernel,
        grid_spec=grid_spec,
        out_shape=jax.ShapeDtypeStruct((n_rows, d), BF16),
        compiler_params=_cparams("arbitrary"),
        name="moe_ffn",
    )(expert, valid, n_used, xs, wgu, wdn)


def _combine_kernel(units_ref, goff_ref, ys_hbm, lrow_ref, w_ref, f_ref, sgu_ref, sdn_ref, h1_ref, g2_ref,
                    lng_ref, lnb_ref, o_ref, buf_ref, sem_ref, acc_ref, *, alpha):
    s = pl.program_id(0)
    n_tiles = pl.num_programs(0)
    slot = s % 2

    def copy(slot, lu, gu):
        return pltpu.make_async_copy(_unit_rows(ys_hbm, gu), _unit_rows(buf_ref.at[slot], lu), sem_ref.at[slot])

    def fetch(tile, slot):
        _for_each_unit(units_ref, goff_ref, tile, lambda lu, gu: copy(slot, lu, gu).start())

    @pl.when(s == 0)
    def _():
        buf_ref[...] = jnp.zeros(buf_ref.shape, BF16)
        fetch(0, 0)

    @pl.when(s + 1 < n_tiles)
    def _():
        fetch(s + 1, 1 - slot)

    n_units = lax.fori_loop(0, N_EXPERTS, lambda e, c: c + units_ref[s * N_EXPERTS + e], 0)

    def wait_one(i, carry):
        copy(slot, 0, 0).wait()
        return carry

    lax.fori_loop(0, n_units, wait_one, 0)

    hgu = jnp.dot(f_ref[...], sgu_ref[...], preferred_element_type=F32)
    hid = _silu(hgu[:, :EXPERT_HIDDEN]) * hgu[:, EXPERT_HIDDEN:]
    acc_ref[...] = jnp.dot(hid.astype(BF16), sdn_ref[...], preferred_element_type=F32)

    lrow = lrow_ref[...]
    w = w_ref[...]
    chunk = 512
    for r0 in range(0, LOCAL_ROWS, chunk):
        @pl.when(r0 < n_units * UNIT)
        def _():
            col = lax.broadcasted_iota(jnp.int32, (SUB_TILE, chunk), 1) + r0
            pw = jnp.zeros((SUB_TILE, chunk), F32)
            for k in range(TOP_K):
                pw = jnp.where(lrow[:, k:k + 1] == col, w[:, k:k + 1], pw)
            acc_ref[...] += jnp.dot(pw.astype(BF16), buf_ref[slot, r0:r0 + chunk, :],
                                    preferred_element_type=F32)

    o_ref[...] = _ln(alpha * h1_ref[...] + g2_ref[...] * acc_ref[...]) * lng_ref[...] + lnb_ref[...]


def _combine(ys, lrow, w, f, sgu, sdn, h1, g2, ln_g, ln_b, units, goff, *, tiles_per_batch, n_batch, alpha):
    t, d = f.shape
    tm = SUB_TILE
    row = lambda s, u, g: (s, 0)
    fixed = lambda s, u, g: (0, 0)
    mod_row = lambda s, u, g: (jnp.minimum(s // tiles_per_batch, n_batch), 0, 0)
    grid_spec = pltpu.PrefetchScalarGridSpec(
        num_scalar_prefetch=2,
        grid=(t // tm,),
        in_specs=[pl.BlockSpec(memory_space=pl.ANY),
                  pl.BlockSpec((tm, LANES), row), pl.BlockSpec((tm, LANES), row), pl.BlockSpec((tm, d), row),
                  pl.BlockSpec(sgu.shape, fixed), pl.BlockSpec(sdn.shape, fixed),
                  pl.BlockSpec((tm, d), row), pl.BlockSpec((None, 1, d), mod_row),
                  pl.BlockSpec((1, d), fixed), pl.BlockSpec((1, d), fixed)],
        out_specs=pl.BlockSpec((tm, d), row),
        scratch_shapes=[pltpu.VMEM((2, LOCAL_ROWS, d), BF16), pltpu.SemaphoreType.DMA((2,)),
                        pltpu.VMEM((tm, d), F32)])
    return pl.pallas_call(
        functools.partial(_combine_kernel, alpha=alpha),
        grid_spec=grid_spec,
        out_shape=jax.ShapeDtypeStruct((t, d), F32),
        compiler_params=_cparams("arbitrary"),
        name="moe_combine",
    )(units, goff, ys, lrow, w, f, sgu, sdn, h1, g2, ln_g.reshape(1, d), ln_b.reshape(1, d))


def _forward(x, c, ctx, c_ctx, w_mod, b_mod, w_in, s5_lam_re, s5_lam_im, s5_log_step, s5_b_re, s5_b_im,
             s5_c_re, s5_c_im, s5_d, gm_w_s, gm_b_s, da_lam, da_subln_g, w_glu_val, w_glu_gate,
             w_proj_gm, w_proj_da, w_out, ln1_g, ln1_b, ln2_g, ln2_b, w_router, b_router,
             w_exp_gate, w_exp_up, w_exp_down, w_sh_gate, w_sh_up, w_sh_down):
    n_batch, seq, d = x.shape
    ctx_len = ctx.shape[1]
    depth = w_mod.shape[0]
    n_lat = n_batch * seq
    alpha = (2 * depth) ** 0.25
    tiles_per_batch = seq // ROW_TILE
    sub_per_batch = seq // SUB_TILE
    assert d == D_MODEL and seq % ROW_TILE == 0 and (n_batch * ctx_len) % ROW_TILE == 0
    assert ctx_len % SUB_TILE == 0 and n_batch == 4

    cond = jnp.concatenate([c, c_ctx[None], jnp.zeros((8 - n_batch - 1, d), F32)], axis=0)
    mod = _modulation(cond, w_mod, b_mod)
    cos_t, sin_t = _rope_tables(seq)
    h = jnp.concatenate([x.reshape(n_lat, d), ctx.reshape(n_batch * ctx_len, d)], axis=0)

    for l in range(depth):
        last = l == depth - 1
        lam_init = 0.8 - 0.6 * math.exp(-0.3 * l)
        names = ("sh1", "sc1", "g1", "sh2", "sc2", "g2")
        mods = {k: mod[l, :, i * d:(i + 1) * d].reshape(8, 1, d) for i, k in enumerate(names)}

        wi = w_in[l]
        a_w, z_w, q_w, k_w, v_w, g_w = jnp.split(wi, (256, 768, 1280, 1792, 2304), axis=1)
        w_in_l = jnp.concatenate([g_w, q_w, k_w, v_w, a_w, z_w], axis=1).astype(BF16)
        p = _inproj(h, mods["sh1"], mods["sc1"], w_in_l, tiles_per_batch, n_batch)

        lf = da_lam[l].astype(F32)
        lam = (jnp.exp(jnp.sum(lf[0] * lf[1])) - jnp.exp(jnp.sum(lf[2] * lf[3])) + lam_init).reshape(1)
        q, k0, k1, v = _rope_pack(p, cos_t, sin_t, n_batch, seq, ctx_len)
        subln = da_subln_g[l].astype(F32).reshape(1, LANES)
        attn = functools.partial(_attention, lam, q, k0, k1, v, subln, out_scale=1.0 - lam_init)
        yc = attn(n_rows=n_lat, q_row0=0, tq=512, kv_len=ctx_len + seq, head_len=ctx_len, tk=512)
        if not last:
            yc_ctx = attn(n_rows=n_batch * ctx_len, q_row0=n_lat, tq=ctx_len, kv_len=ctx_len,
                          head_len=ctx_len, tk=ctx_len)
            yc = jnp.concatenate([yc, yc_ctx], axis=0)

        mats = _s5_matrices(s5_lam_re[l], s5_lam_im[l], s5_log_step[l], s5_b_re[l], s5_b_im[l],
                            s5_c_re[l], s5_c_im[l], s5_d[l])
        ya = _s5_mixer(p, mats, n_batch=n_batch, lat_chunks=seq // S5_CHUNK, ctx_chunks=ctx_len // S5_CHUNK)

        n_rows = n_lat if last else h.shape[0]
        router_w = jnp.concatenate([w_router[l].astype(F32), jnp.zeros((d, LANES - N_EXPERTS), F32)], axis=1)
        merge_w = {
            "glu": jnp.concatenate([w_glu_val[l], w_glu_gate[l]], axis=1).astype(BF16),
            "gm": w_proj_gm[l].astype(BF16), "da": w_proj_da[l].astype(BF16), "out": w_out[l].astype(BF16),
            "gm_ws": gm_w_s[l].astype(BF16),
            "gm_bs": jnp.repeat(gm_b_s[l].astype(F32).T, GM_WIDTH // GM_HEADS, axis=1),
            "router": router_w,
        }
        h1, f, logits = _merge(ya, p, yc, h, mods, ln1_g[l], ln1_b[l], merge_w, n_rows=n_rows,
                               tiles_per_batch=sub_per_batch, n_batch=n_batch, alpha=alpha)

        lrow, w_tok, lrow_t, units_lanes = _route(logits, b_router[l])
        n_tiles = n_rows // SUB_TILE
        units = units_lanes[:, 0].reshape(n_tiles, N_EXPERTS)
        max_units = n_rows * TOP_K // UNIT + n_tiles * N_EXPERTS + N_EXPERTS * (FFN_UNITS - 1)
        n_blocks = max_units // FFN_UNITS + 1
        goff, blk_expert, blk_valid, n_used = _moe_layout(units, n_blocks)
        units = units.reshape(-1)
        wgu = jnp.concatenate([w_exp_gate[l], w_exp_up[l]], axis=2).astype(BF16)
        xs = _dispatch(f, lrow_t, units, goff, n_blocks * FFN_ROWS)
        ys = _expert_ffn(xs, wgu, w_exp_down[l].astype(BF16), blk_expert, blk_valid, n_used)
        sgu = jnp.concatenate([w_sh_gate[l], w_sh_up[l]], axis=1).astype(BF16)
        h = _combine(ys, lrow, w_tok, f, sgu, w_sh_down[l].astype(BF16), h1, mods["g2"], ln2_g[l], ln2_b[l],
                     units, goff, tiles_per_batch=sub_per_batch, n_batch=n_batch, alpha=alpha)

    return h[:n_lat].reshape(n_batch, seq, d)


def kernel(x, c, ctx, c_ctx, w_mod, b_mod, w_in, s5_lam_re, s5_lam_im, s5_log_step, s5_b_re, s5_b_im, s5_c_re, s5_c_im, s5_d, gm_w_s, gm_b_s, da_lam, da_subln_g, w_glu_val, w_glu_gate, w_proj_gm, w_proj_da, w_out, ln1_g, ln1_b, ln2_g, ln2_b, w_router, b_router, w_exp_gate, w_exp_up, w_exp_down, w_sh_gate, w_sh_up, w_sh_down):
    return _forward(x, c, ctx, c_ctx, w_mod, b_mod, w_in, s5_lam_re, s5_lam_im, s5_log_step, s5_b_re, s5_b_im,
                    s5_c_re, s5_c_im, s5_d, gm_w_s, gm_b_s, da_lam, da_subln_g, w_glu_val, w_glu_gate,
                    w_proj_gm, w_proj_da, w_out, ln1_g, ln1_b, ln2_g, ln2_b, w_router, b_router,
                    w_exp_gate, w_exp_up, w_exp_down, w_sh_gate, w_sh_up, w_sh_down)
```

```python
import functools
import math

import jax
import jax.numpy as jnp
from jax import lax
from jax.experimental import pallas as pl
from jax.experimental.pallas import tpu as pltpu

F32 = jnp.float32
BF16 = jnp.bfloat16
HIGHEST = lax.Precision.HIGHEST

D_MODEL = 1024
GRID_W = 64
S5_WIDTH = 256
S5_GROUP = 16
S5_GROUPS = S5_WIDTH // S5_GROUP
S5_STATE = 64
S5_CHUNK = 16
GM_WIDTH = 256
GM_HEADS = 4
GM_CHUNK = 128
DA_HEADS = 4
DA_HEAD_DIM = 64
DA_QK = DA_HEADS * 2 * DA_HEAD_DIM
DA_V = DA_HEADS * 2 * DA_HEAD_DIM
ROPE_BASE = 10000.0
N_BRANCH = 3
N_EXPERTS = 64
TOP_K = 8
N_GROUPS = 8
TOPK_GROUPS = 4
EXPERT_HIDDEN = 256
ROUTED_SCALE = 2.5
LN_EPS = 1e-5

COL_GATE = 0
COL_Q = N_BRANCH * D_MODEL
COL_K = COL_Q + DA_QK
COL_V = COL_K + DA_QK
COL_S5 = COL_V + DA_V
COL_ZU = COL_S5 + S5_WIDTH
COL_ZV = COL_ZU + GM_WIDTH
IN_WIDTH = COL_ZV + GM_WIDTH

LANES = 128
ROW_TILE = 1024
SUB_TILE = 256
VMEM_LIMIT = 48 * 1024 * 1024
UNIT = 16
LOCAL_ROWS = SUB_TILE * TOP_K + N_EXPERTS * UNIT
LOCAL_UNITS = LOCAL_ROWS // UNIT
FFN_ROWS = 512
FFN_UNITS = FFN_ROWS // UNIT
ATTN_ROW_BLOCK = 64
ONEHOT_BLOCK = 128


def _cparams(*sem):
    return pltpu.CompilerParams(dimension_semantics=sem, vmem_limit_bytes=VMEM_LIMIT)


def _ln(x):
    mu = jnp.mean(x, -1, keepdims=True)
    xc = x - mu
    var = jnp.mean(xc * xc, -1, keepdims=True)
    return xc * lax.rsqrt(var + LN_EPS)


def _gelu(x):
    return 0.5 * x * (1.0 + jnp.tanh(math.sqrt(2.0 / math.pi) * (x + 0.044715 * (x * x * x))))


def _sigmoid(x):
    return 1.0 / (1.0 + jnp.exp(-x))


def _silu(x):
    return x * _sigmoid(x)


def _mod_kernel(c_ref, w_ref, b_ref, o_ref):
    s = _silu(c_ref[...])
    o_ref[...] = jnp.dot(s, w_ref[...], preferred_element_type=F32, precision=HIGHEST) + b_ref[...]


def _modulation(cond, w_mod, b_mod):
    depth, d, n = w_mod.shape
    tn = 1536
    return pl.pallas_call(
        _mod_kernel,
        grid=(depth, n // tn),
        in_specs=[pl.BlockSpec((8, d), lambda l, j: (0, 0)),
                  pl.BlockSpec((None, d, tn), lambda l, j: (l, 0, j)),
                  pl.BlockSpec((None, 1, tn), lambda l, j: (l, 0, j))],
        out_specs=pl.BlockSpec((None, 8, tn), lambda l, j: (l, 0, j)),
        out_shape=jax.ShapeDtypeStruct((depth, 8, n), F32),
        compiler_params=_cparams("arbitrary", "arbitrary"),
        name="modulation",
    )(cond, w_mod, b_mod.reshape(depth, 1, n))


def _inproj_kernel(h_ref, sh_ref, sc_ref, w_ref, o_ref, xn_ref):
    @pl.when(pl.program_id(1) == 0)
    def _():
        x = _ln(h_ref[...])
        xn_ref[...] = (x * (1.0 + sc_ref[...]) + sh_ref[...]).astype(BF16)

    o_ref[...] = jnp.dot(xn_ref[...], w_ref[...], preferred_element_type=F32).astype(BF16)


def _inproj(h, shift, scale, w, tiles_per_batch, n_batch):
    t, d = h.shape
    n = w.shape[1]
    tn = 768
    mod_row = lambda i, j: (jnp.minimum(i // tiles_per_batch, n_batch), 0, 0)
    return pl.pallas_call(
        _inproj_kernel,
        grid=(t // ROW_TILE, n // tn),
        in_specs=[pl.BlockSpec((ROW_TILE, d), lambda i, j: (i, 0)),
                  pl.BlockSpec((None, 1, d), mod_row),
                  pl.BlockSpec((None, 1, d), mod_row),
                  pl.BlockSpec((d, tn), lambda i, j: (0, j))],
        out_specs=pl.BlockSpec((ROW_TILE, tn), lambda i, j: (i, j)),
        out_shape=jax.ShapeDtypeStruct((t, n), BF16),
        scratch_shapes=[pltpu.VMEM((ROW_TILE, d), BF16)],
        compiler_params=_cparams("arbitrary", "arbitrary"),
        name="inproj",
    )(h, shift, scale, w)


def _rope_kernel(q_ref, k_ref, v_ref, cos_ref, sin_ref, qo_ref, k0_ref, k1_ref, vo_ref, *, n_lat_tiles):
    is_lat = pl.program_id(0) < n_lat_tiles
    shape = q_ref.shape
    lane = lax.broadcasted_iota(jnp.int32, shape, 1)
    upper16 = (lane % 32) >= 16
    map1 = (lane % LANES) >= DA_HEAD_DIM
    cos = jnp.where(is_lat, cos_ref[...], 1.0)
    sin = jnp.where(is_lat, sin_ref[...], 0.0)

    def rope(x):
        partner = jnp.where(upper16, pltpu.roll(x, 16, 1), pltpu.roll(x, shape[1] - 16, 1))
        return x * cos + partner * sin

    q = rope(q_ref[...].astype(F32)) * (DA_HEAD_DIM ** -0.5)
    k = rope(k_ref[...].astype(F32))
    qo_ref[...] = q.astype(BF16)
    k0_ref[...] = jnp.where(map1, 0.0, k).astype(BF16)
    k1_ref[...] = jnp.where(map1, k, 0.0).astype(BF16)
    vo_ref[...] = v_ref[...]


def _rope_pack(p, cos_t, sin_t, n_batch, seq, ctx):
    t = p.shape[0]
    nl = seq // SUB_TILE
    nc = ctx // SUB_TILE
    n_lat_tiles = n_batch * nl

    def kv_map(i):
        j = i - n_lat_tiles
        b = jnp.where(i < n_lat_tiles, i // nl, j // nc)
        blk = jnp.where(i < n_lat_tiles, nc + i % nl, j % nc)
        return (b, blk, 0)

    tab_map = lambda i: (jnp.where(i < n_lat_tiles, i % nl, 0), 0)
    col = lambda c: (lambda i: (i, c // DA_QK))
    kv_shape = jax.ShapeDtypeStruct((n_batch, ctx + seq, DA_QK), BF16)
    return pl.pallas_call(
        functools.partial(_rope_kernel, n_lat_tiles=n_lat_tiles),
        grid=(t // SUB_TILE,),
        in_specs=[pl.BlockSpec((SUB_TILE, DA_QK), col(COL_Q)),
                  pl.BlockSpec((SUB_TILE, DA_QK), col(COL_K)),
                  pl.BlockSpec((SUB_TILE, DA_V), col(COL_V)),
                  pl.BlockSpec((SUB_TILE, DA_QK), tab_map),
                  pl.BlockSpec((SUB_TILE, DA_QK), tab_map)],
        out_specs=[pl.BlockSpec((SUB_TILE, DA_QK), lambda i: (i, 0)),
                   pl.BlockSpec((None, SUB_TILE, DA_QK), kv_map),
                   pl.BlockSpec((None, SUB_TILE, DA_QK), kv_map),
                   pl.BlockSpec((None, SUB_TILE, DA_V), kv_map)],
        out_shape=[jax.ShapeDtypeStruct((t, DA_QK), BF16), kv_shape, kv_shape, kv_shape],
        compiler_params=_cparams("arbitrary"),
        name="rope_pack",
    )(p, p, p, cos_t, sin_t)


def _rope_tables(seq):
    pos = jnp.arange(seq)
    row = (pos // GRID_W).astype(F32)[:, None]
    colp = (pos % GRID_W).astype(F32)[:, None]
    axis_dim = DA_HEAD_DIM // 2
    inv_freq = ROPE_BASE ** (-jnp.arange(0, axis_dim, 2, dtype=F32) / axis_dim)
    ang_r = row * inv_freq
    ang_c = colp * inv_freq
    ang = jnp.concatenate([ang_r, ang_r, ang_c, ang_c], -1)
    sign = jnp.concatenate([-jnp.ones((16,), F32), jnp.ones((16,), F32)] * 2)
    reps = DA_QK // DA_HEAD_DIM
    return jnp.tile(jnp.cos(ang), (1, reps)), jnp.tile(jnp.sin(ang) * sign, (1, reps))


def _attn_kernel(lam_ref, q_ref, k0_ref, k1_ref, v_ref, g_ref, o_ref, *scratch, n_chunks, tk, out_scale):
    q = q_ref[...]
    nt = (((1,), (1,)), ((), ()))
    k_refs = (k0_ref, k1_ref)
    s_refs = (scratch[0:2], scratch[2:4])
    m_refs, l_refs, acc_refs, p_refs = scratch[4:6], scratch[6:8], scratch[8:10], scratch[10:12]

    def chunk(j):
        return pl.ds(pl.multiple_of(j * tk, tk), tk)

    def scores(j, buf):
        for mp in range(2):
            s_refs[buf][mp][...] = lax.dot_general(q, k_refs[mp][chunk(j), :], nt, preferred_element_type=F32)

    def absorb(j, buf):
        vc = v_ref[chunk(j), :]
        for mp in range(2):
            for r0 in range(0, q.shape[0], ATTN_ROW_BLOCK):
                rows = slice(r0, r0 + ATTN_ROW_BLOCK)
                s = s_refs[buf][mp][rows, :]
                m = m_refs[mp][rows, :]
                m_new = jnp.maximum(m, jnp.broadcast_to(jnp.max(s, -1, keepdims=True), m.shape))
                alpha = jnp.exp(m - m_new)
                p = jnp.exp(s - jnp.concatenate([m_new] * (tk // LANES), axis=1))
                l_refs[mp][rows, :] = (alpha * l_refs[mp][rows, :]
                                       + jnp.broadcast_to(jnp.sum(p, -1, keepdims=True), m.shape))
                acc_refs[mp][rows, :] = alpha * acc_refs[mp][rows, :]
                m_refs[mp][rows, :] = m_new
                p_refs[mp][rows, :] = p.astype(BF16)
            acc_refs[mp][...] += jnp.dot(p_refs[mp][...], vc, preferred_element_type=F32)

    for mp in range(2):
        m_refs[mp][...] = jnp.full(m_refs[mp].shape, -jnp.inf, F32)
        l_refs[mp][...] = jnp.zeros(l_refs[mp].shape, F32)
        acc_refs[mp][...] = jnp.zeros(acc_refs[mp].shape, F32)
    scores(0, 0)

    def body(i, carry):
        scores(2 * i + 1, 1)
        absorb(2 * i, 0)
        scores(2 * i + 2, 0)
        absorb(2 * i + 1, 1)
        return carry

    lax.fori_loop(0, (n_chunks - 1) // 2, body, 0)
    absorb(n_chunks - 1, 0)
    o = acc_refs[0][...] / l_refs[0][...] - lam_ref[0] * (acc_refs[1][...] / l_refs[1][...])
    o = o * lax.rsqrt(jnp.mean(o * o, -1, keepdims=True) + LN_EPS)
    o_ref[...] = (o * g_ref[...] * out_scale).astype(BF16)


def _attention(lam, q, k0, k1, v, subln_g, *, n_rows, q_row0, tq, kv_len, tk, out_scale):
    n_batch = k0.shape[0]
    per_batch = n_rows // n_batch // tq
    q0 = q_row0 // tq
    kv_spec = pl.BlockSpec((None, kv_len, LANES), lambda b, h, i: (b, 0, h))
    return pl.pallas_call(
        functools.partial(_attn_kernel, n_chunks=kv_len // tk, tk=tk, out_scale=out_scale),
        grid=(n_batch, DA_HEADS, per_batch),
        scratch_shapes=([pltpu.VMEM((tq, tk), F32)] * 4 + [pltpu.VMEM((tq, LANES), F32)] * 6
                        + [pltpu.VMEM((tq, tk), BF16)] * 2),
        in_specs=[pl.BlockSpec(memory_space=pltpu.SMEM),
                  pl.BlockSpec((tq, LANES), lambda b, h, i: (q0 + b * per_batch + i, h)),
                  kv_spec, kv_spec, kv_spec,
                  pl.BlockSpec((1, LANES), lambda b, h, i: (0, 0))],
        out_specs=pl.BlockSpec((tq, LANES), lambda b, h, i: (b * per_batch + i, h)),
        out_shape=jax.ShapeDtypeStruct((n_rows, DA_V), BF16),
        compiler_params=_cparams("arbitrary", "arbitrary", "arbitrary"),
        name="diff_attention",
    )(lam, q, k0, k1, v, subln_g)


def _s5_in_kernel(x_ref, b_ref, u_ref, z_ref, stage_ref):
    @pl.when(pl.program_id(1) == 0)
    def _():
        x = x_ref[...].astype(F32)
        for half in range(S5_WIDTH // LANES):
            stage_ref[half] = x[:, half * LANES:(half + 1) * LANES]
        for s in range(S5_CHUNK):
            for half in range(S5_WIDTH // LANES):
                col = s * S5_WIDTH + half * LANES
                u_ref[:, col:col + LANES] = (
                    stage_ref[half, pl.ds(s, u_ref.shape[0], stride=S5_CHUNK), :].astype(BF16))

    z_ref[...] = jnp.dot(u_ref[...], b_ref[...], preferred_element_type=F32)


def _s5_scan_kernel(z_ref, a1_ref, a2_ref, p_ref, *, n_batch, lat_chunks, ctx_chunks):
    reverse = pl.program_id(0) == 1
    a1 = a1_ref[...]
    a2 = a2_ref[...]
    width = a1.shape[1]

    def swap_halves(s):
        return jnp.concatenate(
            [pltpu.roll(s[:, j * LANES:(j + 1) * LANES], LANES // 2, 1) for j in range(width // LANES)], axis=1)

    def run(base, count, carry):
        def body(i, st):
            s, ssw = st
            row = base + jnp.where(reverse, count - 1 - i, i)
            z = z_ref[pl.ds(row, 1), :]
            p_ref[pl.ds(row, 1), :] = s
            return a1 * s + a2 * ssw + z, a1 * ssw - a2 * s + swap_halves(z)
        return lax.fori_loop(0, count, body, carry, unroll=4)

    zero = jnp.zeros(a1.shape, F32)
    for b in range(n_batch):
        st = run(n_batch * lat_chunks + b * ctx_chunks, ctx_chunks, (zero, zero))
        run(b * lat_chunks, lat_chunks, st)


def _s5_out_kernel(u_ref, p_ref, t_ref, ct_ref, y_ref, stage_ref):
    y = jnp.dot(u_ref[...], t_ref[...], preferred_element_type=F32)
    y = y + lax.dot_general(p_ref[...].astype(BF16), ct_ref[...], (((1,), (1,)), ((), ())),
                            preferred_element_type=F32)
    y = _gelu(y)
    per_step = y.shape[1] // S5_WIDTH
    first = pl.program_id(1) * per_step
    halves = S5_WIDTH // LANES
    for k in range(per_step):
        for half in range(halves):
            col = k * S5_WIDTH + half * LANES
            stage_ref[half, pl.ds(first + k, y.shape[0], stride=S5_CHUNK), :] = y[:, col:col + LANES]

    @pl.when(pl.program_id(1) == pl.num_programs(1) - 1)
    def _():
        for half in range(halves):
            y_ref[:, half * LANES:(half + 1) * LANES] = stage_ref[half]


def _s5_mixer(p, mats, *, n_batch, lat_chunks, ctx_chunks):
    t_all = p.shape[0]
    r = t_all // S5_CHUNK
    w = S5_CHUNK * S5_WIDTH
    tr = r // 4
    state_w = mats["b"].shape[1]
    tn = 1024
    u, z = pl.pallas_call(
        _s5_in_kernel,
        grid=(r // tr, state_w // tn),
        in_specs=[pl.BlockSpec((tr * S5_CHUNK, S5_WIDTH), lambda i, j: (i, COL_S5 // S5_WIDTH)),
                  pl.BlockSpec((w, tn), lambda i, j: (0, j))],
        out_specs=[pl.BlockSpec((tr, w), lambda i, j: (i, 0)),
                   pl.BlockSpec((tr, tn), lambda i, j: (i, j))],
        out_shape=[jax.ShapeDtypeStruct((r, w), BF16), jax.ShapeDtypeStruct((r, state_w), F32)],
        scratch_shapes=[pltpu.VMEM((S5_WIDTH // LANES, tr * S5_CHUNK, LANES), F32)],
        compiler_params=_cparams("arbitrary", "arbitrary"),
        name="s5_in",
    )(p, mats["b"])
    half = state_w // 2
    coef = pl.BlockSpec((None, 1, half), lambda d: (d, 0, 0))
    prev = pl.pallas_call(
        functools.partial(_s5_scan_kernel, n_batch=n_batch, lat_chunks=lat_chunks, ctx_chunks=ctx_chunks),
        grid=(2,),
        in_specs=[pl.BlockSpec((r, half), lambda d: (0, d)), coef, coef],
        out_specs=pl.BlockSpec((r, half), lambda d: (0, d)),
        out_shape=jax.ShapeDtypeStruct((r, state_w), F32),
        compiler_params=_cparams("arbitrary"),
        name="s5_scan",
    )(z, mats["a1"], mats["a2"])
    tn = 512
    return pl.pallas_call(
        _s5_out_kernel,
        grid=(r // tr, w // tn),
        in_specs=[pl.BlockSpec((tr, w), lambda i, j: (i, 0)),
                  pl.BlockSpec((tr, state_w), lambda i, j: (i, 0)),
                  pl.BlockSpec((w, tn), lambda i, j: (0, j)),
                  pl.BlockSpec((tn, state_w), lambda i, j: (j, 0))],
        out_specs=pl.BlockSpec((tr * S5_CHUNK, S5_WIDTH), lambda i, j: (i, 0)),
        out_shape=jax.ShapeDtypeStruct((t_all, S5_WIDTH), F32),
        scratch_shapes=[pltpu.VMEM((S5_WIDTH // LANES, tr * S5_CHUNK, LANES), F32)],
        compiler_params=_cparams("arbitrary", "arbitrary"),
        name="s5_out",
    )(u, prev, mats["t"], mats["ct"])


def _s5_matrices(lam_re, lam_im, log_step, b_re, b_im, c_re, c_im, d_skip):
    n = S5_CHUNK
    dt = jnp.exp(log_step.astype(F32))[..., None]
    lr, li = lam_re.astype(F32), lam_im.astype(F32)
    mag = jnp.exp(lr * dt)
    a_re, a_im = mag * jnp.cos(li * dt), mag * jnp.sin(li * dt)
    den = lr * lr + li * li
    n_re = a_re - 1.0
    z_re = (n_re * lr + a_im * li) / den
    z_im = (a_im * lr - n_re * li) / den
    br, bi = b_re.astype(F32), b_im.astype(F32)
    bb_re = z_re[..., None] * br - z_im[..., None] * bi
    bb_im = z_re[..., None] * bi + z_im[..., None] * br
    j = jnp.arange(n + 1, dtype=F32)[:, None, None, None]
    pmag = jnp.exp(lr * dt * j)
    pw_re, pw_im = pmag * jnp.cos(li * dt * j), pmag * jnp.sin(li * dt * j)
    cr, ci = c_re.astype(F32), c_im.astype(F32)
    g, p, c = S5_GROUPS, S5_STATE, S5_GROUP

    def cmul(xr, xi, yr, yi):
        return xr * yr - xi * yi, xr * yi + xi * yr

    def in_mat(direction, powers):
        er = pw_re[powers, direction][:, :, :, None]
        ei = pw_im[powers, direction][:, :, :, None]
        xr, xi = cmul(er, ei, bb_re[direction][None], bb_im[direction][None])
        m = jnp.concatenate([xr, xi], axis=2)
        return m.transpose(1, 0, 3, 2).reshape(g, n * c, 2 * p)

    def out_mat(direction, powers):
        er = pw_re[powers, direction][:, :, None, :]
        ei = pw_im[powers, direction][:, :, None, :]
        wr, wi = cmul(cr[direction][None], ci[direction][None], er, ei)
        m = jnp.concatenate([wr, -wi], axis=3)
        return m.transpose(1, 3, 0, 2).reshape(g, 2 * p, n * c)

    def toeplitz(direction):
        er = pw_re[:n, direction][:, :, None, :, None]
        ei = pw_im[:n, direction][:, :, None, :, None]
        wr, wi = cmul(cr[direction][None, :, :, :, None], ci[direction][None, :, :, :, None], er, ei)
        k = jnp.sum(wr * bb_re[direction][None, :, None] - wi * bb_im[direction][None, :, None], axis=3)
        return k

    steps = jnp.arange(n)
    kf, kb = toeplitz(0), toeplitz(1)
    skip = d_skip.astype(F32).reshape(g, c)
    k0 = kf[0] + kb[0] + jnp.eye(c, dtype=F32)[None] * skip[:, :, None]
    by_lag = jnp.concatenate([kb[:0:-1], k0[None], kf[1:]], axis=0)
    eye_g = jnp.eye(g, dtype=F32)
    lag_blocks = jnp.einsum("mgca,gh->mgahc", by_lag, eye_g).reshape(2 * n - 1, g * c, g * c)
    t_dense = lag_blocks[steps[None, :] - steps[:, None] + n - 1]
    t_dense = t_dense.transpose(0, 2, 1, 3).reshape(n * g * c, n * g * c)

    def coef(direction):
        ar, ai = pw_re[n, direction], pw_im[n, direction]
        return (jnp.concatenate([ar, ar], -1).reshape(-1), jnp.concatenate([-ai, ai], -1).reshape(-1))

    def dense_in(m):
        m = m.reshape(g, n, c, -1)
        return jnp.einsum("gscq,gh->sgchq", m, eye_g).reshape(n * g * c, -1)

    def dense_out_t(m):
        q = m.shape[1]
        return jnp.einsum("gqtc,gh->thcgq", m.reshape(g, q, n, c), eye_g).reshape(n * g * c, g * q)

    a1f, a2f = coef(0)
    a1b, a2b = coef(1)
    return {
        "b": jnp.concatenate([dense_in(in_mat(0, n - 1 - steps)), dense_in(in_mat(1, steps))], 1).astype(BF16),
        "ct": jnp.concatenate([dense_out_t(out_mat(0, steps + 1)), dense_out_t(out_mat(1, n - steps))],
                              1).astype(BF16),
        "t": t_dense.astype(BF16),
        "a1": jnp.stack([a1f, a1b])[:, None, :], "a2": jnp.stack([a2f, a2b])[:, None, :],
    }


def _merge_kernel(ya_ref, zu_ref, zv_ref, yc_ref, ga_ref, gb_ref, gc_ref, h_ref,
                  g1_ref, sh2_ref, sc2_ref, lng_ref, lnb_ref,
                  wglu_ref, wgm_ref, wda_ref, wout_ref, ws_ref, bs_ref, wr_ref,
                  h1_ref, f_ref, lg_ref, *, alpha):
    f32 = lambda ref: ref[...].astype(F32)
    glu = jnp.dot(ya_ref[...].astype(BF16), wglu_ref[...], preferred_element_type=F32)
    branch_a = glu[:, :D_MODEL] * _sigmoid(glu[:, D_MODEL:])

    u = _gelu(f32(zu_ref))
    v = _ln(_gelu(f32(zv_ref))).astype(BF16)
    head = lax.broadcasted_iota(jnp.int32, (GM_CHUNK, GM_WIDTH), 1) // (GM_WIDTH // GM_HEADS)
    parts = []
    for ck in range(v.shape[0] // GM_CHUNK):
        vc = v[ck * GM_CHUNK:(ck + 1) * GM_CHUNK]
        s = bs_ref[...]
        for hd in range(GM_HEADS):
            s = s + jnp.dot(ws_ref[hd], jnp.where(head == hd, vc, jnp.zeros_like(vc)),
                            preferred_element_type=F32)
        parts.append(s)
    yb = (u * jnp.concatenate(parts, axis=0)).astype(BF16)

    m = _sigmoid(f32(ga_ref)) * branch_a
    m = m + _sigmoid(f32(gb_ref)) * jnp.dot(yb, wgm_ref[...], preferred_element_type=F32)
    m = m + _sigmoid(f32(gc_ref)) * jnp.dot(yc_ref[...], wda_ref[...], preferred_element_type=F32)
    mix = jnp.dot(m.astype(BF16), wout_ref[...], preferred_element_type=F32)

    h1 = _ln(alpha * h_ref[...] + g1_ref[...] * mix) * lng_ref[...] + lnb_ref[...]
    h1_ref[...] = h1
    f = _ln(h1) * (1.0 + sc2_ref[...]) + sh2_ref[...]
    f_ref[...] = f.astype(BF16)
    lg_ref[...] = jnp.dot(f, wr_ref[...], preferred_element_type=F32, precision=HIGHEST)


def _merge(ya, p, yc, h, mods, ln_g, ln_b, w, *, n_rows, tiles_per_batch, n_batch, alpha):
    d = D_MODEL
    tm = SUB_TILE
    row = lambda i: (i, 0)
    mod_row = lambda i: (jnp.minimum(i // tiles_per_batch, n_batch), 0, 0)
    pcol = lambda c, width: pl.BlockSpec((tm, width), lambda i: (i, c // width))
    full = lambda a: pl.BlockSpec(a.shape, lambda i: (0,) * a.ndim)
    mod_spec = pl.BlockSpec((None, 1, d), mod_row)
    vec = lambda a: a.reshape(1, d)
    weights = (w["glu"], w["gm"], w["da"], w["out"], w["gm_ws"], w["gm_bs"], w["router"])
    return pl.pallas_call(
        functools.partial(_merge_kernel, alpha=alpha),
        grid=(n_rows // tm,),
        in_specs=[pl.BlockSpec((tm, S5_WIDTH), row),
                  pcol(COL_ZU, GM_WIDTH), pcol(COL_ZV, GM_WIDTH),
                  pl.BlockSpec((tm, DA_V), row),
                  pcol(COL_GATE, d), pcol(COL_GATE + d, d), pcol(COL_GATE + 2 * d, d),
                  pl.BlockSpec((tm, d), row),
                  mod_spec, mod_spec, mod_spec,
                  pl.BlockSpec((1, d), lambda i: (0, 0)), pl.BlockSpec((1, d), lambda i: (0, 0))]
                 + [full(a) for a in weights],
        out_specs=[pl.BlockSpec((tm, d), row), pl.BlockSpec((tm, d), row), pl.BlockSpec((tm, LANES), row)],
        out_shape=[jax.ShapeDtypeStruct((n_rows, d), F32), jax.ShapeDtypeStruct((n_rows, d), BF16),
                   jax.ShapeDtypeStruct((n_rows, LANES), F32)],
        compiler_params=_cparams("arbitrary"),
        name="merge",
    )(ya, p, p, yc, p, p, p, h, mods["g1"], mods["sh2"], mods["sc2"], vec(ln_g), vec(ln_b), *weights)


def _router_kernel(lg_ref, b_ref, before_ref, lrow_ref, w_ref, lrow_t_ref, units_ref):
    tm = lg_ref.shape[0]
    per_group = N_EXPERTS // N_GROUPS
    neg = -jnp.inf
    logits = lg_ref[...].T[:N_EXPERTS]
    scores = _sigmoid(logits).reshape(N_GROUPS, per_group, tm)
    sel = scores + b_ref[...].reshape(N_GROUPS, per_group, 1)

    in_group = lax.broadcasted_iota(jnp.int32, sel.shape, 1)
    top1 = jnp.max(sel, axis=1, keepdims=True)
    first = jnp.min(jnp.where(sel == top1, in_group, per_group), axis=1, keepdims=True)
    top2 = jnp.max(jnp.where(in_group == first, neg, sel), axis=1, keepdims=True)
    gscore = top1 + top2

    gidx = lax.broadcasted_iota(jnp.int32, gscore.shape, 0)
    gsel = jnp.zeros(gscore.shape, jnp.bool_)
    for _ in range(TOPK_GROUPS):
        best = jnp.max(gscore, axis=0, keepdims=True)
        hit = gidx == jnp.min(jnp.where(gscore == best, gidx, N_GROUPS), axis=0, keepdims=True)
        gsel = gsel | hit
        gscore = jnp.where(hit, neg, gscore)

    eidx = lax.broadcasted_iota(jnp.int32, sel.shape, 0) * per_group + in_group
    cand = jnp.where(gsel, sel, neg)
    chosen = jnp.zeros(sel.shape, jnp.bool_)
    hits = []
    for _ in range(TOP_K):
        best = jnp.max(jnp.max(cand, axis=1, keepdims=True), axis=0, keepdims=True)
        at = jnp.where(cand == best, eidx, N_EXPERTS)
        hit = eidx == jnp.min(jnp.min(at, axis=1, keepdims=True), axis=0, keepdims=True)
        hits.append(hit)
        chosen = chosen | hit
        cand = jnp.where(hit, neg, cand)

    w = jnp.where(chosen, scores, 0.0)
    total = jnp.sum(jnp.sum(w, axis=1, keepdims=True), axis=0, keepdims=True)
    w = w / total * ROUTED_SCALE

    onehot = jnp.where(chosen, 1.0, 0.0).reshape(N_EXPERTS, tm).astype(BF16)
    rank = jnp.dot(onehot, before_ref[...], preferred_element_type=F32)
    count = jnp.dot(onehot, jnp.ones((tm, LANES), BF16), preferred_element_type=F32)
    units = jnp.floor((count + (UNIT - 1)) * (1.0 / UNIT))
    ei = lax.broadcasted_iota(jnp.int32, (N_EXPERTS, N_EXPERTS), 0)
    ej = lax.broadcasted_iota(jnp.int32, (N_EXPERTS, N_EXPERTS), 1)
    first_unit = jnp.dot(jnp.where(ej < ei, 1.0, 0.0).astype(BF16), units.astype(BF16),
                         preferred_element_type=F32)
    base = jnp.concatenate([first_unit * UNIT] * (tm // LANES), axis=1)
    pos = (base + rank).reshape(sel.shape)

    pick = lambda hit, val: jnp.sum(jnp.sum(jnp.where(hit, val, 0.0), axis=1, keepdims=True), axis=0)
    pad = jnp.zeros((LANES - TOP_K, tm), F32)
    lrow_t = jnp.concatenate([pick(hit, pos) for hit in hits] + [pad], axis=0)
    w_t = jnp.concatenate([pick(hit, w) for hit in hits] + [pad], axis=0)
    lrow_t_ref[...] = lrow_t[:TOP_K].astype(jnp.int32)
    lrow_ref[...] = lrow_t.T.astype(jnp.int32)
    w_ref[...] = w_t.T
    units_ref[...] = units.astype(jnp.int32)


def _route(logits, b_router):
    t = logits.shape[0]
    tm = SUB_TILE
    n_sub = t // tm
    before = jnp.triu(jnp.ones((tm, tm), F32), 1).astype(BF16)
    tok = pl.BlockSpec((tm, LANES), lambda i: (i, 0))
    pick = pl.BlockSpec((TOP_K, tm), lambda i: (0, i))
    return pl.pallas_call(
        _router_kernel,
        grid=(n_sub,),
        in_specs=[tok, pl.BlockSpec((N_EXPERTS, 1), lambda i: (0, 0)), pl.BlockSpec((tm, tm), lambda i: (0, 0))],
        out_specs=[tok, tok, pick, pl.BlockSpec((N_EXPERTS, LANES), lambda i: (i, 0))],
        out_shape=[jax.ShapeDtypeStruct((t, LANES), jnp.int32), jax.ShapeDtypeStruct((t, LANES), F32),
                   jax.ShapeDtypeStruct((TOP_K, t), jnp.int32),
                   jax.ShapeDtypeStruct((n_sub * N_EXPERTS, LANES), jnp.int32)],
        compiler_params=_cparams("arbitrary"),
        name="router",
    )(logits, b_router.astype(F32).reshape(N_EXPERTS, 1), before)


def _moe_layout(units, n_blocks):
    per_expert = jnp.sum(units, axis=0)
    padded = (per_expert + FFN_UNITS - 1) // FFN_UNITS * FFN_UNITS
    ends = jnp.cumsum(padded)
    starts = ends - padded
    goff = starts[None, :] + jnp.cumsum(units, axis=0) - units
    local_end = jnp.cumsum(units, axis=1)
    lu = jnp.arange(LOCAL_UNITS, dtype=jnp.int32)
    owner = jnp.minimum(jnp.sum(local_end[:, None, :] <= lu[None, :, None], axis=2), N_EXPERTS - 1)
    take = lambda a: jnp.take_along_axis(a, owner, axis=1)
    gmap = take(goff) + lu[None, :] - take(local_end - units)
    n_used = ends[-1] // FFN_UNITS
    blk = jnp.arange(n_blocks, dtype=jnp.int32)
    blk = jnp.minimum(blk, n_used - 1)
    first = blk * FFN_UNITS
    expert = jnp.minimum(jnp.sum(ends[None, :] <= first[:, None], axis=1), N_EXPERTS - 1)
    valid = jnp.clip(per_expert[expert] - (first - starts[expert]), 0, FFN_UNITS)
    i32 = lambda a: a.astype(jnp.int32)
    return i32(gmap.reshape(-1)), i32(local_end[:, -1]), i32(expert), i32(valid), i32(n_used.reshape(1))


def _unit_rows(ref, unit):
    return ref.at[pl.ds(pl.multiple_of(unit * UNIT, UNIT), UNIT), :]


def _for_each_unit(gmap_ref, count_ref, tile, fn):
    def body(lu, carry):
        fn(lu, gmap_ref[tile * LOCAL_UNITS + lu])
        return carry

    lax.fori_loop(0, count_ref[tile], body, 0)


def _dispatch_kernel(gmap_ref, count_ref, f_ref, lrow_t_ref, xs_hbm, buf_ref, sem_ref, onehot_ref):
    s = pl.program_id(0)
    last = pl.num_programs(0) - 1
    slot = s % 2

    def copy(slot, lu, gu):
        return pltpu.make_async_copy(_unit_rows(buf_ref.at[slot], lu), _unit_rows(xs_hbm, gu), sem_ref.at[slot])

    def drain(step):
        def body(i, carry):
            copy(step % 2, 0, 0).wait()
            return carry
        lax.fori_loop(0, count_ref[step], body, 0)

    @pl.when(s >= 2)
    def _():
        drain(s - 2)

    lt = lrow_t_ref[...]
    f = f_ref[...]
    chunk = 512
    for r0 in range(0, LOCAL_ROWS, chunk):
        @pl.when(r0 < count_ref[s] * UNIT)
        def _():
            for p0 in range(0, chunk, ONEHOT_BLOCK):
                row = lax.broadcasted_iota(jnp.int32, (ONEHOT_BLOCK, SUB_TILE), 0) + (r0 + p0)
                hit = row == lt[0:1, :]
                for k in range(1, TOP_K):
                    hit = hit | (row == lt[k:k + 1, :])
                onehot_ref[p0:p0 + ONEHOT_BLOCK, :] = jnp.where(hit, 1.0, 0.0).astype(BF16)
            buf_ref[slot, r0:r0 + chunk, :] = jnp.dot(onehot_ref[...], f, preferred_element_type=F32).astype(BF16)

    _for_each_unit(gmap_ref, count_ref, s, lambda lu, gu: copy(slot, lu, gu).start())

    @pl.when(s == last)
    def _():
        drain(s)

        @pl.when(s >= 1)
        def _():
            drain(s - 1)


def _dispatch(f, lrow_t, gmap, count, n_rows_out):
    t, d = f.shape
    grid_spec = pltpu.PrefetchScalarGridSpec(
        num_scalar_prefetch=2,
        grid=(t // SUB_TILE,),
        in_specs=[pl.BlockSpec((SUB_TILE, d), lambda s, u, g: (s, 0)),
                  pl.BlockSpec((TOP_K, SUB_TILE), lambda s, u, g: (0, s))],
        out_specs=pl.BlockSpec(memory_space=pl.ANY),
        scratch_shapes=[pltpu.VMEM((2, LOCAL_ROWS, d), BF16), pltpu.SemaphoreType.DMA((2,)),
                        pltpu.VMEM((512, SUB_TILE), BF16)])
    return pl.pallas_call(
        _dispatch_kernel,
        grid_spec=grid_spec,
        out_shape=jax.ShapeDtypeStruct((n_rows_out, d), BF16),
        compiler_params=_cparams("arbitrary"),
        name="moe_dispatch",
    )(gmap, count, f, lrow_t)


def _ffn_kernel(expert_ref, valid_ref, nused_ref, x_ref, wg_ref, wu_ref, wd_ref, y_ref, wgu_s, wd_s):
    b = pl.program_id(0)

    @pl.when(b < nused_ref[0])
    def _():
        @pl.when((b == 0) | (expert_ref[b] != expert_ref[jnp.maximum(b - 1, 0)]))
        def _():
            wgu_s[:, :EXPERT_HIDDEN] = wg_ref[...].astype(BF16)
            wgu_s[:, EXPERT_HIDDEN:] = wu_ref[...].astype(BF16)
            wd_s[...] = wd_ref[...].astype(BF16)

        rows = lax.broadcasted_iota(jnp.int32, (FFN_ROWS, 1), 0)
        x = x_ref[...]
        x = jnp.where(rows < valid_ref[b] * UNIT, x, jnp.zeros_like(x))
        hgu = jnp.dot(x, wgu_s[...], preferred_element_type=F32)
        hid = _silu(hgu[:, :EXPERT_HIDDEN]) * hgu[:, EXPERT_HIDDEN:]
        y_ref[...] = jnp.dot(hid.astype(BF16), wd_s[...], preferred_element_type=F32).astype(BF16)


def _expert_ffn(xs, w_gate, w_up, w_down, layer, expert, valid, n_used):
    n_rows, d = xs.shape
    row_blk = lambda b, e, v, n: (jnp.minimum(b, n[0] - 1), 0)
    of_expert = lambda b, e, v, n: (layer, e[b], 0, 0)
    grid_spec = pltpu.PrefetchScalarGridSpec(
        num_scalar_prefetch=3,
        grid=(n_rows // FFN_ROWS,),
        in_specs=[pl.BlockSpec((FFN_ROWS, d), row_blk),
                  pl.BlockSpec((None, None, d, EXPERT_HIDDEN), of_expert),
                  pl.BlockSpec((None, None, d, EXPERT_HIDDEN), of_expert),
                  pl.BlockSpec((None, None, EXPERT_HIDDEN, d), of_expert)],
        out_specs=pl.BlockSpec((FFN_ROWS, d), row_blk),
        scratch_shapes=[pltpu.VMEM((d, 2 * EXPERT_HIDDEN), BF16), pltpu.VMEM((EXPERT_HIDDEN, d), BF16)])
    return pl.pallas_call(
        _ffn_kernel,
        grid_spec=grid_spec,
        out_shape=jax.ShapeDtypeStruct((n_rows, d), BF16),
        compiler_params=_cparams("arbitrary"),
        name="moe_ffn",
    )(expert, valid, n_used, xs, w_gate, w_up, w_down)


def _combine_kernel(gmap_ref, count_ref, ys_hbm, lrow_ref, w_ref, f_ref, sgu_ref, sdn_ref, h1_ref, g2_ref,
                    lng_ref, lnb_ref, o_ref, buf_ref, sem_ref, acc_ref, *, alpha):
    s = pl.program_id(0)
    n_tiles = pl.num_programs(0)
    slot = s % 2

    def copy(slot, lu, gu):
        return pltpu.make_async_copy(_unit_rows(ys_hbm, gu), _unit_rows(buf_ref.at[slot], lu), sem_ref.at[slot])

    def fetch(tile, slot):
        _for_each_unit(gmap_ref, count_ref, tile, lambda lu, gu: copy(slot, lu, gu).start())

    @pl.when(s == 0)
    def _():
        buf_ref[...] = jnp.zeros(buf_ref.shape, BF16)
        fetch(0, 0)

    @pl.when(s + 1 < n_tiles)
    def _():
        fetch(s + 1, 1 - slot)

    n_units = count_ref[s]

    def wait_one(i, carry):
        copy(slot, 0, 0).wait()
        return carry

    lax.fori_loop(0, n_units, wait_one, 0)

    hgu = jnp.dot(f_ref[...], sgu_ref[...], preferred_element_type=F32)
    hid = _silu(hgu[:, :EXPERT_HIDDEN]) * hgu[:, EXPERT_HIDDEN:]
    acc_ref[...] = jnp.dot(hid.astype(BF16), sdn_ref[...], preferred_element_type=F32)

    lrow = lrow_ref[...]
    w = w_ref[...]
    chunk = 512
    for r0 in range(0, LOCAL_ROWS, chunk):
        @pl.when(r0 < n_units * UNIT)
        def _():
            col = lax.broadcasted_iota(jnp.int32, (SUB_TILE, chunk), 1) + r0
            pw = jnp.zeros((SUB_TILE, chunk), F32)
            for k in range(TOP_K):
                pw = jnp.where(lrow[:, k:k + 1] == col, w[:, k:k + 1], pw)
            acc_ref[...] += jnp.dot(pw.astype(BF16), buf_ref[slot, r0:r0 + chunk, :],
                                    preferred_element_type=F32)

    o_ref[...] = _ln(alpha * h1_ref[...] + g2_ref[...] * acc_ref[...]) * lng_ref[...] + lnb_ref[...]


def _combine(ys, lrow, w, f, sgu, sdn, h1, g2, ln_g, ln_b, gmap, count, *, tiles_per_batch, n_batch, alpha):
    t, d = f.shape
    tm = SUB_TILE
    row = lambda s, u, g: (s, 0)
    fixed = lambda s, u, g: (0, 0)
    mod_row = lambda s, u, g: (jnp.minimum(s // tiles_per_batch, n_batch), 0, 0)
    grid_spec = pltpu.PrefetchScalarGridSpec(
        num_scalar_prefetch=2,
        grid=(t // tm,),
        in_specs=[pl.BlockSpec(memory_space=pl.ANY),
                  pl.BlockSpec((tm, LANES), row), pl.BlockSpec((tm, LANES), row), pl.BlockSpec((tm, d), row),
                  pl.BlockSpec(sgu.shape, fixed), pl.BlockSpec(sdn.shape, fixed),
                  pl.BlockSpec((tm, d), row), pl.BlockSpec((None, 1, d), mod_row),
                  pl.BlockSpec((1, d), fixed), pl.BlockSpec((1, d), fixed)],
        out_specs=pl.BlockSpec((tm, d), row),
        scratch_shapes=[pltpu.VMEM((2, LOCAL_ROWS, d), BF16), pltpu.SemaphoreType.DMA((2,)),
                        pltpu.VMEM((tm, d), F32)])
    return pl.pallas_call(
        functools.partial(_combine_kernel, alpha=alpha),
        grid_spec=grid_spec,
        out_shape=jax.ShapeDtypeStruct((t, d), F32),
        compiler_params=_cparams("arbitrary"),
        name="moe_combine",
    )(gmap, count, ys, lrow, w, f, sgu, sdn, h1, g2, ln_g.reshape(1, d), ln_b.reshape(1, d))


def _forward(x, c, ctx, c_ctx, w_mod, b_mod, w_in, s5_lam_re, s5_lam_im, s5_log_step, s5_b_re, s5_b_im,
             s5_c_re, s5_c_im, s5_d, gm_w_s, gm_b_s, da_lam, da_subln_g, w_glu_val, w_glu_gate,
             w_proj_gm, w_proj_da, w_out, ln1_g, ln1_b, ln2_g, ln2_b, w_router, b_router,
             w_exp_gate, w_exp_up, w_exp_down, w_sh_gate, w_sh_up, w_sh_down):
    n_batch, seq, d = x.shape
    ctx_len = ctx.shape[1]
    depth = w_mod.shape[0]
    n_lat = n_batch * seq
    alpha = (2 * depth) ** 0.25
    tiles_per_batch = seq // ROW_TILE
    sub_per_batch = seq // SUB_TILE
    assert d == D_MODEL and seq % ROW_TILE == 0 and (n_batch * ctx_len) % ROW_TILE == 0
    assert ctx_len % SUB_TILE == 0 and n_batch == 4

    cond = jnp.concatenate([c, c_ctx[None], jnp.zeros((8 - n_batch - 1, d), F32)], axis=0)
    mod = _modulation(cond, w_mod, b_mod)
    cos_t, sin_t = _rope_tables(seq)
    h = jnp.concatenate([x.reshape(n_lat, d), ctx.reshape(n_batch * ctx_len, d)], axis=0)

    for l in range(depth):
        last = l == depth - 1
        lam_init = 0.8 - 0.6 * math.exp(-0.3 * l)
        names = ("sh1", "sc1", "g1", "sh2", "sc2", "g2")
        mods = {k: mod[l, :, i * d:(i + 1) * d].reshape(8, 1, d) for i, k in enumerate(names)}

        wi = w_in[l]
        a_w, z_w, q_w, k_w, v_w, g_w = jnp.split(wi, (256, 768, 1280, 1792, 2304), axis=1)
        w_in_l = jnp.concatenate([g_w, q_w, k_w, v_w, a_w, z_w], axis=1).astype(BF16)
        p = _inproj(h, mods["sh1"], mods["sc1"], w_in_l, tiles_per_batch, n_batch)

        lf = da_lam[l].astype(F32)
        lam = (jnp.exp(jnp.sum(lf[0] * lf[1])) - jnp.exp(jnp.sum(lf[2] * lf[3])) + lam_init).reshape(1)
        q, k0, k1, v = _rope_pack(p, cos_t, sin_t, n_batch, seq, ctx_len)
        subln = da_subln_g[l].astype(F32).reshape(1, LANES)
        attn = functools.partial(_attention, lam, q, k0, k1, v, subln, out_scale=1.0 - lam_init)
        yc = attn(n_rows=n_lat, q_row0=0, tq=512, kv_len=ctx_len + seq, tk=ctx_len)
        if not last:
            yc_ctx = attn(n_rows=n_batch * ctx_len, q_row0=n_lat, tq=ctx_len, kv_len=ctx_len, tk=ctx_len)
            yc = jnp.concatenate([yc, yc_ctx], axis=0)

        mats = _s5_matrices(s5_lam_re[l], s5_lam_im[l], s5_log_step[l], s5_b_re[l], s5_b_im[l],
                            s5_c_re[l], s5_c_im[l], s5_d[l])
        ya = _s5_mixer(p, mats, n_batch=n_batch, lat_chunks=seq // S5_CHUNK, ctx_chunks=ctx_len // S5_CHUNK)

        n_rows = n_lat if last else h.shape[0]
        router_w = jnp.concatenate([w_router[l].astype(F32), jnp.zeros((d, LANES - N_EXPERTS), F32)], axis=1)
        merge_w = {
            "glu": jnp.concatenate([w_glu_val[l], w_glu_gate[l]], axis=1).astype(BF16),
            "gm": w_proj_gm[l].astype(BF16), "da": w_proj_da[l].astype(BF16), "out": w_out[l].astype(BF16),
            "gm_ws": gm_w_s[l].astype(BF16),
            "gm_bs": jnp.repeat(gm_b_s[l].astype(F32).T, GM_WIDTH // GM_HEADS, axis=1),
            "router": router_w,
        }
        h1, f, logits = _merge(ya, p, yc, h, mods, ln1_g[l], ln1_b[l], merge_w, n_rows=n_rows,
                               tiles_per_batch=sub_per_batch, n_batch=n_batch, alpha=alpha)

        lrow, w_tok, lrow_t, units_lanes = _route(logits, b_router[l])
        n_tiles = n_rows // SUB_TILE
        units = units_lanes[:, 0].reshape(n_tiles, N_EXPERTS)
        max_units = n_rows * TOP_K // UNIT + n_tiles * N_EXPERTS + N_EXPERTS * (FFN_UNITS - 1)
        n_blocks = max_units // FFN_UNITS + 1
        gmap, tile_units, blk_expert, blk_valid, n_used = _moe_layout(units, n_blocks)
        xs = _dispatch(f, lrow_t, gmap, tile_units, n_blocks * FFN_ROWS)
        ys = _expert_ffn(xs, w_exp_gate, w_exp_up, w_exp_down, l, blk_expert, blk_valid, n_used)
        sgu = jnp.concatenate([w_sh_gate[l], w_sh_up[l]], axis=1).astype(BF16)
        h = _combine(ys, lrow, w_tok, f, sgu, w_sh_down[l].astype(BF16), h1, mods["g2"], ln2_g[l], ln2_b[l],
                     gmap, tile_units, tiles_per_batch=sub_per_batch, n_batch=n_batch, alpha=alpha)

    return h[:n_lat].reshape(n_batch, seq, d)


def kernel(x, c, ctx, c_ctx, w_mod, b_mod, w_in, s5_lam_re, s5_lam_im, s5_log_step, s5_b_re, s5_b_im, s5_c_re, s5_c_im, s5_d, gm_w_s, gm_b_s, da_lam, da_subln_g, w_glu_val, w_glu_gate, w_proj_gm, w_proj_da, w_out, ln1_g, ln1_b, ln2_g, ln2_b, w_router, b_router, w_exp_gate, w_exp_up, w_exp_down, w_sh_gate, w_sh_up, w_sh_down):
    return _forward(x, c, ctx, c_ctx, w_mod, b_mod, w_in, s5_lam_re, s5_lam_im, s5_log_step, s5_b_re, s5_b_im,
                    s5_c_re, s5_c_im, s5_d, gm_w_s, gm_b_s, da_lam, da_subln_g, w_glu_val, w_glu_gate,
                    w_proj_gm, w_proj_da, w_out, ln1_g, ln1_b, ln2_g, ln2_b, w_router, b_router,
                    w_exp_gate, w_exp_up, w_exp_down, w_sh_gate, w_sh_up, w_sh_down)
```

```python
import functools
import math

import jax
import jax.numpy as jnp
from jax import lax
from jax.experimental import pallas as pl
from jax.experimental.pallas import tpu as pltpu

F32 = jnp.float32
BF16 = jnp.bfloat16
HIGHEST = lax.Precision.HIGHEST

D_MODEL = 1024
GRID_W = 64
S5_WIDTH = 256
S5_GROUP = 16
S5_GROUPS = S5_WIDTH // S5_GROUP
S5_STATE = 64
S5_CHUNK = 16
GM_WIDTH = 256
GM_HEADS = 4
GM_CHUNK = 128
DA_HEADS = 4
DA_HEAD_DIM = 64
DA_QK = DA_HEADS * 2 * DA_HEAD_DIM
DA_V = DA_HEADS * 2 * DA_HEAD_DIM
ROPE_BASE = 10000.0
N_BRANCH = 3
N_EXPERTS = 64
TOP_K = 8
N_GROUPS = 8
TOPK_GROUPS = 4
EXPERT_HIDDEN = 256
ROUTED_SCALE = 2.5
LN_EPS = 1e-5

COL_GATE = 0
COL_Q = N_BRANCH * D_MODEL
COL_K = COL_Q + DA_QK
COL_V = COL_K + DA_QK
COL_S5 = COL_V + DA_V
COL_ZU = COL_S5 + S5_WIDTH
COL_ZV = COL_ZU + GM_WIDTH
IN_WIDTH = COL_ZV + GM_WIDTH

LANES = 128
ROW_TILE = 1024
SUB_TILE = 256
VMEM_LIMIT = 48 * 1024 * 1024
UNIT = 16
LOCAL_ROWS = SUB_TILE * TOP_K + N_EXPERTS * UNIT
LOCAL_UNITS = LOCAL_ROWS // UNIT
FFN_ROWS = 512
FFN_UNITS = FFN_ROWS // UNIT
ATTN_ROW_BLOCK = 64
ONEHOT_BLOCK = 128


def _cparams(*sem):
    return pltpu.CompilerParams(dimension_semantics=sem, vmem_limit_bytes=VMEM_LIMIT)


def _ln(x):
    mu = jnp.mean(x, -1, keepdims=True)
    xc = x - mu
    var = jnp.mean(xc * xc, -1, keepdims=True)
    return xc * lax.rsqrt(var + LN_EPS)


def _gelu(x):
    return 0.5 * x * (1.0 + jnp.tanh(math.sqrt(2.0 / math.pi) * (x + 0.044715 * (x * x * x))))


def _sigmoid(x):
    return 1.0 / (1.0 + jnp.exp(-x))


def _silu(x):
    return x * _sigmoid(x)


def _mod_kernel(c_ref, w_ref, b_ref, o_ref):
    s = _silu(c_ref[...])
    o_ref[...] = jnp.dot(s, w_ref[...], preferred_element_type=F32, precision=HIGHEST) + b_ref[...]


def _modulation(cond, w_mod, b_mod):
    depth, d, n = w_mod.shape
    tn = 1536
    return pl.pallas_call(
        _mod_kernel,
        grid=(depth, n // tn),
        in_specs=[pl.BlockSpec((8, d), lambda l, j: (0, 0)),
                  pl.BlockSpec((None, d, tn), lambda l, j: (l, 0, j)),
                  pl.BlockSpec((None, 1, tn), lambda l, j: (l, 0, j))],
        out_specs=pl.BlockSpec((None, 8, tn), lambda l, j: (l, 0, j)),
        out_shape=jax.ShapeDtypeStruct((depth, 8, n), F32),
        compiler_params=_cparams("arbitrary", "arbitrary"),
        name="modulation",
    )(cond, w_mod, b_mod.reshape(depth, 1, n))


def _inproj_kernel(h_ref, sh_ref, sc_ref, w_ref, o_ref, xn_ref):
    @pl.when(pl.program_id(1) == 0)
    def _():
        x = _ln(h_ref[...])
        xn_ref[...] = (x * (1.0 + sc_ref[...]) + sh_ref[...]).astype(BF16)

    o_ref[...] = jnp.dot(xn_ref[...], w_ref[...], preferred_element_type=F32).astype(BF16)


def _inproj(h, shift, scale, w, tiles_per_batch, n_batch):
    t, d = h.shape
    n = w.shape[1]
    tn = 768
    mod_row = lambda i, j: (jnp.minimum(i // tiles_per_batch, n_batch), 0, 0)
    return pl.pallas_call(
        _inproj_kernel,
        grid=(t // ROW_TILE, n // tn),
        in_specs=[pl.BlockSpec((ROW_TILE, d), lambda i, j: (i, 0)),
                  pl.BlockSpec((None, 1, d), mod_row),
                  pl.BlockSpec((None, 1, d), mod_row),
                  pl.BlockSpec((d, tn), lambda i, j: (0, j))],
        out_specs=pl.BlockSpec((ROW_TILE, tn), lambda i, j: (i, j)),
        out_shape=jax.ShapeDtypeStruct((t, n), BF16),
        scratch_shapes=[pltpu.VMEM((ROW_TILE, d), BF16)],
        compiler_params=_cparams("arbitrary", "arbitrary"),
        name="inproj",
    )(h, shift, scale, w)


def _rope_kernel(q_ref, k_ref, v_ref, cos_ref, sin_ref, qo_ref, k0_ref, k1_ref, vo_ref, *, n_lat_tiles):
    is_lat = pl.program_id(0) < n_lat_tiles
    shape = q_ref.shape
    lane = lax.broadcasted_iota(jnp.int32, shape, 1)
    upper16 = (lane % 32) >= 16
    map1 = (lane % LANES) >= DA_HEAD_DIM
    cos = jnp.where(is_lat, cos_ref[...], 1.0)
    sin = jnp.where(is_lat, sin_ref[...], 0.0)

    def rope(x):
        partner = jnp.where(upper16, pltpu.roll(x, 16, 1), pltpu.roll(x, shape[1] - 16, 1))
        return x * cos + partner * sin

    q = rope(q_ref[...].astype(F32)) * (DA_HEAD_DIM ** -0.5)
    k = rope(k_ref[...].astype(F32))
    qo_ref[...] = q.astype(BF16)
    k0_ref[...] = jnp.where(map1, 0.0, k).astype(BF16)
    k1_ref[...] = jnp.where(map1, k, 0.0).astype(BF16)
    vo_ref[...] = v_ref[...]


def _rope_pack(p, cos_t, sin_t, n_batch, seq, ctx):
    t = p.shape[0]
    nl = seq // SUB_TILE
    nc = ctx // SUB_TILE
    n_lat_tiles = n_batch * nl

    def kv_map(i):
        j = i - n_lat_tiles
        b = jnp.where(i < n_lat_tiles, i // nl, j // nc)
        blk = jnp.where(i < n_lat_tiles, nc + i % nl, j % nc)
        return (b, blk, 0)

    tab_map = lambda i: (jnp.where(i < n_lat_tiles, i % nl, 0), 0)
    col = lambda c: (lambda i: (i, c // DA_QK))
    kv_shape = jax.ShapeDtypeStruct((n_batch, ctx + seq, DA_QK), BF16)
    return pl.pallas_call(
        functools.partial(_rope_kernel, n_lat_tiles=n_lat_tiles),
        grid=(t // SUB_TILE,),
        in_specs=[pl.BlockSpec((SUB_TILE, DA_QK), col(COL_Q)),
                  pl.BlockSpec((SUB_TILE, DA_QK), col(COL_K)),
                  pl.BlockSpec((SUB_TILE, DA_V), col(COL_V)),
                  pl.BlockSpec((SUB_TILE, DA_QK), tab_map),
                  pl.BlockSpec((SUB_TILE, DA_QK), tab_map)],
        out_specs=[pl.BlockSpec((SUB_TILE, DA_QK), lambda i: (i, 0)),
                   pl.BlockSpec((None, SUB_TILE, DA_QK), kv_map),
                   pl.BlockSpec((None, SUB_TILE, DA_QK), kv_map),
                   pl.BlockSpec((None, SUB_TILE, DA_V), kv_map)],
        out_shape=[jax.ShapeDtypeStruct((t, DA_QK), BF16), kv_shape, kv_shape, kv_shape],
        compiler_params=_cparams("arbitrary"),
        name="rope_pack",
    )(p, p, p, cos_t, sin_t)


def _rope_tables(seq):
    pos = jnp.arange(seq)
    row = (pos // GRID_W).astype(F32)[:, None]
    colp = (pos % GRID_W).astype(F32)[:, None]
    axis_dim = DA_HEAD_DIM // 2
    inv_freq = ROPE_BASE ** (-jnp.arange(0, axis_dim, 2, dtype=F32) / axis_dim)
    ang_r = row * inv_freq
    ang_c = colp * inv_freq
    ang = jnp.concatenate([ang_r, ang_r, ang_c, ang_c], -1)
    sign = jnp.concatenate([-jnp.ones((16,), F32), jnp.ones((16,), F32)] * 2)
    reps = DA_QK // DA_HEAD_DIM
    return jnp.tile(jnp.cos(ang), (1, reps)), jnp.tile(jnp.sin(ang) * sign, (1, reps))


def _attn_kernel(lam_ref, q_ref, k0_ref, k1_ref, v_ref, g_ref, o_ref, *scratch, n_chunks, tk, out_scale):
    q = q_ref[...]
    nt = (((1,), (1,)), ((), ()))
    k_refs = (k0_ref, k1_ref)
    s_refs = (scratch[0:2], scratch[2:4])
    m_refs, l_refs, acc_refs, p_refs = scratch[4:6], scratch[6:8], scratch[8:10], scratch[10:12]

    def chunk(j):
        return pl.ds(pl.multiple_of(j * tk, tk), tk)

    def scores(j, buf):
        for mp in range(2):
            s_refs[buf][mp][...] = lax.dot_general(q, k_refs[mp][chunk(j), :], nt, preferred_element_type=F32)

    def absorb(j, buf):
        vc = v_ref[chunk(j), :]
        for mp in range(2):
            for r0 in range(0, q.shape[0], ATTN_ROW_BLOCK):
                rows = slice(r0, r0 + ATTN_ROW_BLOCK)
                s = s_refs[buf][mp][rows, :]
                m = m_refs[mp][rows, :]
                m_new = jnp.maximum(m, jnp.broadcast_to(jnp.max(s, -1, keepdims=True), m.shape))
                alpha = jnp.exp(m - m_new)
                p = jnp.exp(s - jnp.concatenate([m_new] * (tk // LANES), axis=1))
                l_refs[mp][rows, :] = (alpha * l_refs[mp][rows, :]
                                       + jnp.broadcast_to(jnp.sum(p, -1, keepdims=True), m.shape))
                acc_refs[mp][rows, :] = alpha * acc_refs[mp][rows, :]
                m_refs[mp][rows, :] = m_new
                p_refs[mp][rows, :] = p.astype(BF16)
            acc_refs[mp][...] += jnp.dot(p_refs[mp][...], vc, preferred_element_type=F32)

    for mp in range(2):
        m_refs[mp][...] = jnp.full(m_refs[mp].shape, -jnp.inf, F32)
        l_refs[mp][...] = jnp.zeros(l_refs[mp].shape, F32)
        acc_refs[mp][...] = jnp.zeros(acc_refs[mp].shape, F32)
    scores(0, 0)

    def body(i, carry):
        scores(2 * i + 1, 1)
        absorb(2 * i, 0)
        scores(2 * i + 2, 0)
        absorb(2 * i + 1, 1)
        return carry

    lax.fori_loop(0, (n_chunks - 1) // 2, body, 0)
    absorb(n_chunks - 1, 0)
    o = acc_refs[0][...] / l_refs[0][...] - lam_ref[0] * (acc_refs[1][...] / l_refs[1][...])
    o = o * lax.rsqrt(jnp.mean(o * o, -1, keepdims=True) + LN_EPS)
    o_ref[...] = (o * g_ref[...] * out_scale).astype(BF16)


def _attention(lam, q, k0, k1, v, subln_g, *, n_rows, q_row0, tq, kv_len, tk, out_scale):
    n_batch = k0.shape[0]
    per_batch = n_rows // n_batch // tq
    q0 = q_row0 // tq
    kv_spec = pl.BlockSpec((None, kv_len, LANES), lambda b, h, i: (b, 0, h))
    return pl.pallas_call(
        functools.partial(_attn_kernel, n_chunks=kv_len // tk, tk=tk, out_scale=out_scale),
        grid=(n_batch, DA_HEADS, per_batch),
        scratch_shapes=([pltpu.VMEM((tq, tk), F32)] * 4 + [pltpu.VMEM((tq, LANES), F32)] * 6
                        + [pltpu.VMEM((tq, tk), BF16)] * 2),
        in_specs=[pl.BlockSpec(memory_space=pltpu.SMEM),
                  pl.BlockSpec((tq, LANES), lambda b, h, i: (q0 + b * per_batch + i, h)),
                  kv_spec, kv_spec, kv_spec,
                  pl.BlockSpec((1, LANES), lambda b, h, i: (0, 0))],
        out_specs=pl.BlockSpec((tq, LANES), lambda b, h, i: (b * per_batch + i, h)),
        out_shape=jax.ShapeDtypeStruct((n_rows, DA_V), BF16),
        compiler_params=_cparams("arbitrary", "arbitrary", "arbitrary"),
        name="diff_attention",
    )(lam, q, k0, k1, v, subln_g)


def _s5_in_kernel(x_ref, b_ref, u_ref, z_ref, stage_ref):
    @pl.when(pl.program_id(1) == 0)
    def _():
        x = x_ref[...].astype(F32)
        for half in range(S5_WIDTH // LANES):
            stage_ref[half] = x[:, half * LANES:(half + 1) * LANES]
        for s in range(S5_CHUNK):
            for half in range(S5_WIDTH // LANES):
                col = s * S5_WIDTH + half * LANES
                u_ref[:, col:col + LANES] = (
                    stage_ref[half, pl.ds(s, u_ref.shape[0], stride=S5_CHUNK), :].astype(BF16))

    z_ref[...] = jnp.dot(u_ref[...], b_ref[...], preferred_element_type=F32)


def _s5_scan_kernel(z_ref, a1_ref, a2_ref, p_ref, *, n_batch, lat_chunks, ctx_chunks):
    reverse = pl.program_id(0) == 1
    a1 = a1_ref[...]
    a2 = a2_ref[...]
    width = a1.shape[1]

    def swap_halves(s):
        return jnp.concatenate(
            [pltpu.roll(s[:, j * LANES:(j + 1) * LANES], LANES // 2, 1) for j in range(width // LANES)], axis=1)

    def run(base, count, carry):
        def body(i, st):
            s, ssw = st
            row = base + jnp.where(reverse, count - 1 - i, i)
            z = z_ref[pl.ds(row, 1), :]
            p_ref[pl.ds(row, 1), :] = s
            return a1 * s + a2 * ssw + z, a1 * ssw - a2 * s + swap_halves(z)
        return lax.fori_loop(0, count, body, carry, unroll=4)

    zero = jnp.zeros(a1.shape, F32)
    for b in range(n_batch):
        st = run(n_batch * lat_chunks + b * ctx_chunks, ctx_chunks, (zero, zero))
        run(b * lat_chunks, lat_chunks, st)


def _s5_out_kernel(u_ref, p_ref, t_ref, ct_ref, y_ref, stage_ref):
    y = jnp.dot(u_ref[...], t_ref[...], preferred_element_type=F32)
    y = y + lax.dot_general(p_ref[...].astype(BF16), ct_ref[...], (((1,), (1,)), ((), ())),
                            preferred_element_type=F32)
    y = _gelu(y)
    per_step = y.shape[1] // S5_WIDTH
    first = pl.program_id(1) * per_step
    halves = S5_WIDTH // LANES
    for k in range(per_step):
        for half in range(halves):
            col = k * S5_WIDTH + half * LANES
            stage_ref[half, pl.ds(first + k, y.shape[0], stride=S5_CHUNK), :] = y[:, col:col + LANES]

    @pl.when(pl.program_id(1) == pl.num_programs(1) - 1)
    def _():
        for half in range(halves):
            y_ref[:, half * LANES:(half + 1) * LANES] = stage_ref[half]


def _s5_mixer(p, mats, *, n_batch, lat_chunks, ctx_chunks):
    t_all = p.shape[0]
    r = t_all // S5_CHUNK
    w = S5_CHUNK * S5_WIDTH
    tr = r // 4
    state_w = mats["b"].shape[1]
    tn = 1024
    u, z = pl.pallas_call(
        _s5_in_kernel,
        grid=(r // tr, state_w // tn),
        in_specs=[pl.BlockSpec((tr * S5_CHUNK, S5_WIDTH), lambda i, j: (i, COL_S5 // S5_WIDTH)),
                  pl.BlockSpec((w, tn), lambda i, j: (0, j))],
        out_specs=[pl.BlockSpec((tr, w), lambda i, j: (i, 0)),
                   pl.BlockSpec((tr, tn), lambda i, j: (i, j))],
        out_shape=[jax.ShapeDtypeStruct((r, w), BF16), jax.ShapeDtypeStruct((r, state_w), F32)],
        scratch_shapes=[pltpu.VMEM((S5_WIDTH // LANES, tr * S5_CHUNK, LANES), F32)],
        compiler_params=_cparams("arbitrary", "arbitrary"),
        name="s5_in",
    )(p, mats["b"])
    half = state_w // 2
    coef = pl.BlockSpec((None, 1, half), lambda d: (d, 0, 0))
    prev = pl.pallas_call(
        functools.partial(_s5_scan_kernel, n_batch=n_batch, lat_chunks=lat_chunks, ctx_chunks=ctx_chunks),
        grid=(2,),
        in_specs=[pl.BlockSpec((r, half), lambda d: (0, d)), coef, coef],
        out_specs=pl.BlockSpec((r, half), lambda d: (0, d)),
        out_shape=jax.ShapeDtypeStruct((r, state_w), F32),
        compiler_params=_cparams("arbitrary"),
        name="s5_scan",
    )(z, mats["a1"], mats["a2"])
    tn = 512
    return pl.pallas_call(
        _s5_out_kernel,
        grid=(r // tr, w // tn),
        in_specs=[pl.BlockSpec((tr, w), lambda i, j: (i, 0)),
                  pl.BlockSpec((tr, state_w), lambda i, j: (i, 0)),
                  pl.BlockSpec((w, tn), lambda i, j: (0, j)),
                  pl.BlockSpec((tn, state_w), lambda i, j: (j, 0))],
        out_specs=pl.BlockSpec((tr * S5_CHUNK, S5_WIDTH), lambda i, j: (i, 0)),
        out_shape=jax.ShapeDtypeStruct((t_all, S5_WIDTH), F32),
        scratch_shapes=[pltpu.VMEM((S5_WIDTH // LANES, tr * S5_CHUNK, LANES), F32)],
        compiler_params=_cparams("arbitrary", "arbitrary"),
        name="s5_out",
    )(u, prev, mats["t"], mats["ct"])


def _s5_matrices(lam_re, lam_im, log_step, b_re, b_im, c_re, c_im, d_skip):
    n = S5_CHUNK
    dt = jnp.exp(log_step.astype(F32))[..., None]
    lr, li = lam_re.astype(F32), lam_im.astype(F32)
    mag = jnp.exp(lr * dt)
    a_re, a_im = mag * jnp.cos(li * dt), mag * jnp.sin(li * dt)
    den = lr * lr + li * li
    n_re = a_re - 1.0
    z_re = (n_re * lr + a_im * li) / den
    z_im = (a_im * lr - n_re * li) / den
    br, bi = b_re.astype(F32), b_im.astype(F32)
    bb_re = z_re[..., None] * br - z_im[..., None] * bi
    bb_im = z_re[..., None] * bi + z_im[..., None] * br
    j = jnp.arange(n + 1, dtype=F32)[:, None, None, None]
    pmag = jnp.exp(lr * dt * j)
    pw_re, pw_im = pmag * jnp.cos(li * dt * j), pmag * jnp.sin(li * dt * j)
    cr, ci = c_re.astype(F32), c_im.astype(F32)
    g, p, c = S5_GROUPS, S5_STATE, S5_GROUP

    def cmul(xr, xi, yr, yi):
        return xr * yr - xi * yi, xr * yi + xi * yr

    def in_mat(direction, powers):
        er = pw_re[powers, direction][:, :, :, None]
        ei = pw_im[powers, direction][:, :, :, None]
        xr, xi = cmul(er, ei, bb_re[direction][None], bb_im[direction][None])
        m = jnp.concatenate([xr, xi], axis=2)
        return m.transpose(1, 0, 3, 2).reshape(g, n * c, 2 * p)

    def out_mat(direction, powers):
        er = pw_re[powers, direction][:, :, None, :]
        ei = pw_im[powers, direction][:, :, None, :]
        wr, wi = cmul(cr[direction][None], ci[direction][None], er, ei)
        m = jnp.concatenate([wr, -wi], axis=3)
        return m.transpose(1, 3, 0, 2).reshape(g, 2 * p, n * c)

    def toeplitz(direction):
        er = pw_re[:n, direction][:, :, None, :, None]
        ei = pw_im[:n, direction][:, :, None, :, None]
        wr, wi = cmul(cr[direction][None, :, :, :, None], ci[direction][None, :, :, :, None], er, ei)
        k = jnp.sum(wr * bb_re[direction][None, :, None] - wi * bb_im[direction][None, :, None], axis=3)
        return k

    up, down = slice(0, n), slice(n - 1, None, -1)
    up1, down1 = slice(1, n + 1), slice(n, 0, -1)
    kf, kb = toeplitz(0), toeplitz(1)
    skip = d_skip.astype(F32).reshape(g, c)
    k0 = kf[0] + kb[0] + jnp.eye(c, dtype=F32)[None] * skip[:, :, None]
    by_lag = jnp.concatenate([kb[:0:-1], k0[None], kf[1:]], axis=0)
    eye_g = jnp.eye(g, dtype=F32)
    lag_blocks = jnp.einsum("mgca,gh->mgahc", by_lag, eye_g).reshape(2 * n - 1, g * c, g * c)
    t_dense = jnp.stack([lag_blocks[n - 1 - s:2 * n - 1 - s] for s in range(n)])
    t_dense = t_dense.transpose(0, 2, 1, 3).reshape(n * g * c, n * g * c)

    def coef(direction):
        ar, ai = pw_re[n, direction], pw_im[n, direction]
        return (jnp.concatenate([ar, ar], -1).reshape(-1), jnp.concatenate([-ai, ai], -1).reshape(-1))

    def dense_in(m):
        m = m.reshape(g, n, c, -1)
        return jnp.einsum("gscq,gh->sgchq", m, eye_g).reshape(n * g * c, -1)

    def dense_out_t(m):
        q = m.shape[1]
        return jnp.einsum("gqtc,gh->thcgq", m.reshape(g, q, n, c), eye_g).reshape(n * g * c, g * q)

    a1f, a2f = coef(0)
    a1b, a2b = coef(1)
    return {
        "b": jnp.concatenate([dense_in(in_mat(0, down)), dense_in(in_mat(1, up))], 1).astype(BF16),
        "ct": jnp.concatenate([dense_out_t(out_mat(0, up1)), dense_out_t(out_mat(1, down1))], 1).astype(BF16),
        "t": t_dense.astype(BF16),
        "a1": jnp.stack([a1f, a1b])[:, None, :], "a2": jnp.stack([a2f, a2b])[:, None, :],
    }


def _merge_kernel(ya_ref, zu_ref, zv_ref, yc_ref, ga_ref, gb_ref, gc_ref, h_ref,
                  g1_ref, sh2_ref, sc2_ref, lng_ref, lnb_ref,
                  wglu_ref, wgm_ref, wda_ref, wout_ref, ws_ref, bs_ref, wr_ref,
                  h1_ref, f_ref, lg_ref, *, alpha):
    f32 = lambda ref: ref[...].astype(F32)
    glu = jnp.dot(ya_ref[...].astype(BF16), wglu_ref[...], preferred_element_type=F32)
    branch_a = glu[:, :D_MODEL] * _sigmoid(glu[:, D_MODEL:])

    u = _gelu(f32(zu_ref))
    v = _ln(_gelu(f32(zv_ref))).astype(BF16)
    head = lax.broadcasted_iota(jnp.int32, (GM_CHUNK, GM_WIDTH), 1) // (GM_WIDTH // GM_HEADS)
    parts = []
    for ck in range(v.shape[0] // GM_CHUNK):
        vc = v[ck * GM_CHUNK:(ck + 1) * GM_CHUNK]
        s = bs_ref[...]
        for hd in range(GM_HEADS):
            s = s + jnp.dot(ws_ref[hd], jnp.where(head == hd, vc, jnp.zeros_like(vc)),
                            preferred_element_type=F32)
        parts.append(s)
    yb = (u * jnp.concatenate(parts, axis=0)).astype(BF16)

    m = _sigmoid(f32(ga_ref)) * branch_a
    m = m + _sigmoid(f32(gb_ref)) * jnp.dot(yb, wgm_ref[...], preferred_element_type=F32)
    m = m + _sigmoid(f32(gc_ref)) * jnp.dot(yc_ref[...], wda_ref[...], preferred_element_type=F32)
    mix = jnp.dot(m.astype(BF16), wout_ref[...], preferred_element_type=F32)

    h1 = _ln(alpha * h_ref[...] + g1_ref[...] * mix) * lng_ref[...] + lnb_ref[...]
    h1_ref[...] = h1
    f = _ln(h1) * (1.0 + sc2_ref[...]) + sh2_ref[...]
    f_ref[...] = f.astype(BF16)
    lg_ref[...] = jnp.dot(f, wr_ref[...], preferred_element_type=F32, precision=HIGHEST)


def _merge(ya, p, yc, h, mods, ln_g, ln_b, w, *, n_rows, tiles_per_batch, n_batch, alpha):
    d = D_MODEL
    tm = SUB_TILE
    row = lambda i: (i, 0)
    mod_row = lambda i: (jnp.minimum(i // tiles_per_batch, n_batch), 0, 0)
    pcol = lambda c, width: pl.BlockSpec((tm, width), lambda i: (i, c // width))
    full = lambda a: pl.BlockSpec(a.shape, lambda i: (0,) * a.ndim)
    mod_spec = pl.BlockSpec((None, 1, d), mod_row)
    vec = lambda a: a.reshape(1, d)
    weights = (w["glu"], w["gm"], w["da"], w["out"], w["gm_ws"], w["gm_bs"], w["router"])
    return pl.pallas_call(
        functools.partial(_merge_kernel, alpha=alpha),
        grid=(n_rows // tm,),
        in_specs=[pl.BlockSpec((tm, S5_WIDTH), row),
                  pcol(COL_ZU, GM_WIDTH), pcol(COL_ZV, GM_WIDTH),
                  pl.BlockSpec((tm, DA_V), row),
                  pcol(COL_GATE, d), pcol(COL_GATE + d, d), pcol(COL_GATE + 2 * d, d),
                  pl.BlockSpec((tm, d), row),
                  mod_spec, mod_spec, mod_spec,
                  pl.BlockSpec((1, d), lambda i: (0, 0)), pl.BlockSpec((1, d), lambda i: (0, 0))]
                 + [full(a) for a in weights],
        out_specs=[pl.BlockSpec((tm, d), row), pl.BlockSpec((tm, d), row), pl.BlockSpec((tm, LANES), row)],
        out_shape=[jax.ShapeDtypeStruct((n_rows, d), F32), jax.ShapeDtypeStruct((n_rows, d), BF16),
                   jax.ShapeDtypeStruct((n_rows, LANES), F32)],
        compiler_params=_cparams("arbitrary"),
        name="merge",
    )(ya, p, p, yc, p, p, p, h, mods["g1"], mods["sh2"], mods["sc2"], vec(ln_g), vec(ln_b), *weights)


def _router_kernel(lg_ref, b_ref, before_ref, lrow_ref, w_ref, lrow_t_ref, units_ref):
    tm = lg_ref.shape[0]
    per_group = N_EXPERTS // N_GROUPS
    neg = -jnp.inf
    logits = lg_ref[...].T[:N_EXPERTS]
    scores = _sigmoid(logits).reshape(N_GROUPS, per_group, tm)
    sel = scores + b_ref[...].reshape(N_GROUPS, per_group, 1)

    in_group = lax.broadcasted_iota(jnp.int32, sel.shape, 1)
    top1 = jnp.max(sel, axis=1, keepdims=True)
    first = jnp.min(jnp.where(sel == top1, in_group, per_group), axis=1, keepdims=True)
    top2 = jnp.max(jnp.where(in_group == first, neg, sel), axis=1, keepdims=True)
    gscore = top1 + top2

    gidx = lax.broadcasted_iota(jnp.int32, gscore.shape, 0)
    gsel = jnp.zeros(gscore.shape, jnp.bool_)
    for _ in range(TOPK_GROUPS):
        best = jnp.max(gscore, axis=0, keepdims=True)
        hit = gidx == jnp.min(jnp.where(gscore == best, gidx, N_GROUPS), axis=0, keepdims=True)
        gsel = gsel | hit
        gscore = jnp.where(hit, neg, gscore)

    eidx = lax.broadcasted_iota(jnp.int32, sel.shape, 0) * per_group + in_group
    cand = jnp.where(gsel, sel, neg)
    chosen = jnp.zeros(sel.shape, jnp.bool_)
    hits = []
    for _ in range(TOP_K):
        best = jnp.max(jnp.max(cand, axis=1, keepdims=True), axis=0, keepdims=True)
        at = jnp.where(cand == best, eidx, N_EXPERTS)
        hit = eidx == jnp.min(jnp.min(at, axis=1, keepdims=True), axis=0, keepdims=True)
        hits.append(hit)
        chosen = chosen | hit
        cand = jnp.where(hit, neg, cand)

    w = jnp.where(chosen, scores, 0.0)
    total = jnp.sum(jnp.sum(w, axis=1, keepdims=True), axis=0, keepdims=True)
    w = w / total * ROUTED_SCALE

    onehot = jnp.where(chosen, 1.0, 0.0).reshape(N_EXPERTS, tm).astype(BF16)
    rank = jnp.dot(onehot, before_ref[...], preferred_element_type=F32)
    count = jnp.dot(onehot, jnp.ones((tm, LANES), BF16), preferred_element_type=F32)
    units = jnp.floor((count + (UNIT - 1)) * (1.0 / UNIT))
    ei = lax.broadcasted_iota(jnp.int32, (N_EXPERTS, N_EXPERTS), 0)
    ej = lax.broadcasted_iota(jnp.int32, (N_EXPERTS, N_EXPERTS), 1)
    first_unit = jnp.dot(jnp.where(ej < ei, 1.0, 0.0).astype(BF16), units.astype(BF16),
                         preferred_element_type=F32)
    base = jnp.concatenate([first_unit * UNIT] * (tm // LANES), axis=1)
    pos = (base + rank).reshape(sel.shape)

    pick = lambda hit, val: jnp.sum(jnp.sum(jnp.where(hit, val, 0.0), axis=1, keepdims=True), axis=0)
    pad = jnp.zeros((LANES - TOP_K, tm), F32)
    lrow_t = jnp.concatenate([pick(hit, pos) for hit in hits] + [pad], axis=0)
    w_t = jnp.concatenate([pick(hit, w) for hit in hits] + [pad], axis=0)
    lrow_t_ref[...] = lrow_t[:TOP_K].astype(jnp.int32)
    lrow_ref[...] = lrow_t.T.astype(jnp.int32)
    w_ref[...] = w_t.T
    units_ref[...] = units.astype(jnp.int32)


def _route(logits, b_router):
    t = logits.shape[0]
    tm = SUB_TILE
    n_sub = t // tm
    before = jnp.triu(jnp.ones((tm, tm), F32), 1).astype(BF16)
    tok = pl.BlockSpec((tm, LANES), lambda i: (i, 0))
    pick = pl.BlockSpec((TOP_K, tm), lambda i: (0, i))
    return pl.pallas_call(
        _router_kernel,
        grid=(n_sub,),
        in_specs=[tok, pl.BlockSpec((N_EXPERTS, 1), lambda i: (0, 0)), pl.BlockSpec((tm, tm), lambda i: (0, 0))],
        out_specs=[tok, tok, pick, pl.BlockSpec((N_EXPERTS, LANES), lambda i: (i, 0))],
        out_shape=[jax.ShapeDtypeStruct((t, LANES), jnp.int32), jax.ShapeDtypeStruct((t, LANES), F32),
                   jax.ShapeDtypeStruct((TOP_K, t), jnp.int32),
                   jax.ShapeDtypeStruct((n_sub * N_EXPERTS, LANES), jnp.int32)],
        compiler_params=_cparams("arbitrary"),
        name="router",
    )(logits, b_router.astype(F32).reshape(N_EXPERTS, 1), before)


def _moe_layout(units, n_blocks):
    per_expert = jnp.sum(units, axis=0)
    padded = (per_expert + FFN_UNITS - 1) // FFN_UNITS * FFN_UNITS
    ends = jnp.cumsum(padded)
    starts = ends - padded
    goff = starts[None, :] + jnp.cumsum(units, axis=0) - units
    local_end = jnp.cumsum(units, axis=1)
    lu = jnp.arange(LOCAL_UNITS, dtype=jnp.int32)
    owner = jnp.minimum(jnp.sum(local_end[:, None, :] <= lu[None, :, None], axis=2), N_EXPERTS - 1)
    is_owner = owner[:, :, None] == jnp.arange(N_EXPERTS, dtype=jnp.int32)[None, None, :]
    shift = goff - (local_end - units)
    gmap = lu[None, :] + jnp.sum(jnp.where(is_owner, shift[:, None, :], 0), axis=2)
    n_used = ends[-1] // FFN_UNITS
    blk = jnp.arange(n_blocks, dtype=jnp.int32)
    blk = jnp.minimum(blk, n_used - 1)
    first = blk * FFN_UNITS
    expert = jnp.minimum(jnp.sum(ends[None, :] <= first[:, None], axis=1), N_EXPERTS - 1)
    of_block = expert[:, None] == jnp.arange(N_EXPERTS, dtype=jnp.int32)[None, :]
    range_end = jnp.sum(jnp.where(of_block, (starts + per_expert)[None, :], 0), axis=1)
    valid = jnp.clip(range_end - first, 0, FFN_UNITS)
    i32 = lambda a: a.astype(jnp.int32)
    return i32(gmap.reshape(-1)), i32(local_end[:, -1]), i32(expert), i32(valid), i32(n_used.reshape(1))


def _unit_rows(ref, unit):
    return ref.at[pl.ds(pl.multiple_of(unit * UNIT, UNIT), UNIT), :]


def _for_each_unit(gmap_ref, count_ref, tile, fn):
    def body(lu, carry):
        fn(lu, gmap_ref[tile * LOCAL_UNITS + lu])
        return carry

    lax.fori_loop(0, count_ref[tile], body, 0)


def _dispatch_kernel(gmap_ref, count_ref, f_ref, lrow_t_ref, xs_hbm, buf_ref, sem_ref, onehot_ref):
    s = pl.program_id(0)
    last = pl.num_programs(0) - 1
    slot = s % 2

    def copy(slot, lu, gu):
        return pltpu.make_async_copy(_unit_rows(buf_ref.at[slot], lu), _unit_rows(xs_hbm, gu), sem_ref.at[slot])

    def drain(step):
        def body(i, carry):
            copy(step % 2, 0, 0).wait()
            return carry
        lax.fori_loop(0, count_ref[step], body, 0)

    @pl.when(s >= 2)
    def _():
        drain(s - 2)

    lt = lrow_t_ref[...]
    f = f_ref[...]
    chunk = 512
    for r0 in range(0, LOCAL_ROWS, chunk):
        @pl.when(r0 < count_ref[s] * UNIT)
        def _():
            for p0 in range(0, chunk, ONEHOT_BLOCK):
                row = lax.broadcasted_iota(jnp.int32, (ONEHOT_BLOCK, SUB_TILE), 0) + (r0 + p0)
                hit = row == lt[0:1, :]
                for k in range(1, TOP_K):
                    hit = hit | (row == lt[k:k + 1, :])
                onehot_ref[p0:p0 + ONEHOT_BLOCK, :] = jnp.where(hit, 1.0, 0.0).astype(BF16)
            buf_ref[slot, r0:r0 + chunk, :] = jnp.dot(onehot_ref[...], f, preferred_element_type=F32).astype(BF16)

    _for_each_unit(gmap_ref, count_ref, s, lambda lu, gu: copy(slot, lu, gu).start())

    @pl.when(s == last)
    def _():
        drain(s)

        @pl.when(s >= 1)
        def _():
            drain(s - 1)


def _dispatch(f, lrow_t, gmap, count, n_rows_out):
    t, d = f.shape
    grid_spec = pltpu.PrefetchScalarGridSpec(
        num_scalar_prefetch=2,
        grid=(t // SUB_TILE,),
        in_specs=[pl.BlockSpec((SUB_TILE, d), lambda s, u, g: (s, 0)),
                  pl.BlockSpec((TOP_K, SUB_TILE), lambda s, u, g: (0, s))],
        out_specs=pl.BlockSpec(memory_space=pl.ANY),
        scratch_shapes=[pltpu.VMEM((2, LOCAL_ROWS, d), BF16), pltpu.SemaphoreType.DMA((2,)),
                        pltpu.VMEM((512, SUB_TILE), BF16)])
    return pl.pallas_call(
        _dispatch_kernel,
        grid_spec=grid_spec,
        out_shape=jax.ShapeDtypeStruct((n_rows_out, d), BF16),
        compiler_params=_cparams("arbitrary"),
        name="moe_dispatch",
    )(gmap, count, f, lrow_t)


def _ffn_kernel(expert_ref, valid_ref, nused_ref, x_ref, wg_ref, wu_ref, wd_ref, y_ref, wgu_s, wd_s):
    b = pl.program_id(0)

    @pl.when(b < nused_ref[0])
    def _():
        @pl.when((b == 0) | (expert_ref[b] != expert_ref[jnp.maximum(b - 1, 0)]))
        def _():
            wgu_s[:, :EXPERT_HIDDEN] = wg_ref[...].astype(BF16)
            wgu_s[:, EXPERT_HIDDEN:] = wu_ref[...].astype(BF16)
            wd_s[...] = wd_ref[...].astype(BF16)

        rows = lax.broadcasted_iota(jnp.int32, (FFN_ROWS, 1), 0)
        x = x_ref[...]
        x = jnp.where(rows < valid_ref[b] * UNIT, x, jnp.zeros_like(x))
        hgu = jnp.dot(x, wgu_s[...], preferred_element_type=F32)
        hid = _silu(hgu[:, :EXPERT_HIDDEN]) * hgu[:, EXPERT_HIDDEN:]
        y_ref[...] = jnp.dot(hid.astype(BF16), wd_s[...], preferred_element_type=F32).astype(BF16)


def _expert_ffn(xs, w_gate, w_up, w_down, layer, expert, valid, n_used):
    n_rows, d = xs.shape
    row_blk = lambda b, e, v, n: (jnp.minimum(b, n[0] - 1), 0)
    of_expert = lambda b, e, v, n: (layer, e[b], 0, 0)
    grid_spec = pltpu.PrefetchScalarGridSpec(
        num_scalar_prefetch=3,
        grid=(n_rows // FFN_ROWS,),
        in_specs=[pl.BlockSpec((FFN_ROWS, d), row_blk),
                  pl.BlockSpec((None, None, d, EXPERT_HIDDEN), of_expert),
                  pl.BlockSpec((None, None, d, EXPERT_HIDDEN), of_expert),
                  pl.BlockSpec((None, None, EXPERT_HIDDEN, d), of_expert)],
        out_specs=pl.BlockSpec((FFN_ROWS, d), row_blk),
        scratch_shapes=[pltpu.VMEM((d, 2 * EXPERT_HIDDEN), BF16), pltpu.VMEM((EXPERT_HIDDEN, d), BF16)])
    return pl.pallas_call(
        _ffn_kernel,
        grid_spec=grid_spec,
        out_shape=jax.ShapeDtypeStruct((n_rows, d), BF16),
        compiler_params=_cparams("arbitrary"),
        name="moe_ffn",
    )(expert, valid, n_used, xs, w_gate, w_up, w_down)


def _combine_kernel(gmap_ref, count_ref, ys_hbm, lrow_ref, w_ref, f_ref, sgu_ref, sdn_ref, h1_ref, g2_ref,
                    lng_ref, lnb_ref, o_ref, buf_ref, sem_ref, acc_ref, *, alpha):
    s = pl.program_id(0)
    n_tiles = pl.num_programs(0)
    slot = s % 2

    def copy(slot, lu, gu):
        return pltpu.make_async_copy(_unit_rows(ys_hbm, gu), _unit_rows(buf_ref.at[slot], lu), sem_ref.at[slot])

    def fetch(tile, slot):
        _for_each_unit(gmap_ref, count_ref, tile, lambda lu, gu: copy(slot, lu, gu).start())

    @pl.when(s == 0)
    def _():
        buf_ref[...] = jnp.zeros(buf_ref.shape, BF16)
        fetch(0, 0)

    @pl.when(s + 1 < n_tiles)
    def _():
        fetch(s + 1, 1 - slot)

    n_units = count_ref[s]

    def wait_one(i, carry):
        copy(slot, 0, 0).wait()
        return carry

    lax.fori_loop(0, n_units, wait_one, 0)

    hgu = jnp.dot(f_ref[...], sgu_ref[...], preferred_element_type=F32)
    hid = _silu(hgu[:, :EXPERT_HIDDEN]) * hgu[:, EXPERT_HIDDEN:]
    acc_ref[...] = jnp.dot(hid.astype(BF16), sdn_ref[...], preferred_element_type=F32)

    lrow = lrow_ref[...]
    w = w_ref[...]
    chunk = 512
    for r0 in range(0, LOCAL_ROWS, chunk):
        @pl.when(r0 < n_units * UNIT)
        def _():
            col = lax.broadcasted_iota(jnp.int32, (SUB_TILE, chunk), 1) + r0
            pw = jnp.zeros((SUB_TILE, chunk), F32)
            for k in range(TOP_K):
                pw = jnp.where(lrow[:, k:k + 1] == col, w[:, k:k + 1], pw)
            acc_ref[...] += jnp.dot(pw.astype(BF16), buf_ref[slot, r0:r0 + chunk, :],
                                    preferred_element_type=F32)

    o_ref[...] = _ln(alpha * h1_ref[...] + g2_ref[...] * acc_ref[...]) * lng_ref[...] + lnb_ref[...]


def _combine(ys, lrow, w, f, sgu, sdn, h1, g2, ln_g, ln_b, gmap, count, *, tiles_per_batch, n_batch, alpha):
    t, d = f.shape
    tm = SUB_TILE
    row = lambda s, u, g: (s, 0)
    fixed = lambda s, u, g: (0, 0)
    mod_row = lambda s, u, g: (jnp.minimum(s // tiles_per_batch, n_batch), 0, 0)
    grid_spec = pltpu.PrefetchScalarGridSpec(
        num_scalar_prefetch=2,
        grid=(t // tm,),
        in_specs=[pl.BlockSpec(memory_space=pl.ANY),
                  pl.BlockSpec((tm, LANES), row), pl.BlockSpec((tm, LANES), row), pl.BlockSpec((tm, d), row),
                  pl.BlockSpec(sgu.shape, fixed), pl.BlockSpec(sdn.shape, fixed),
                  pl.BlockSpec((tm, d), row), pl.BlockSpec((None, 1, d), mod_row),
                  pl.BlockSpec((1, d), fixed), pl.BlockSpec((1, d), fixed)],
        out_specs=pl.BlockSpec((tm, d), row),
        scratch_shapes=[pltpu.VMEM((2, LOCAL_ROWS, d), BF16), pltpu.SemaphoreType.DMA((2,)),
                        pltpu.VMEM((tm, d), F32)])
    return pl.pallas_call(
        functools.partial(_combine_kernel, alpha=alpha),
        grid_spec=grid_spec,
        out_shape=jax.ShapeDtypeStruct((t, d), F32),
        compiler_params=_cparams("arbitrary"),
        name="moe_combine",
    )(gmap, count, ys, lrow, w, f, sgu, sdn, h1, g2, ln_g.reshape(1, d), ln_b.reshape(1, d))


def _forward(x, c, ctx, c_ctx, w_mod, b_mod, w_in, s5_lam_re, s5_lam_im, s5_log_step, s5_b_re, s5_b_im,
             s5_c_re, s5_c_im, s5_d, gm_w_s, gm_b_s, da_lam, da_subln_g, w_glu_val, w_glu_gate,
             w_proj_gm, w_proj_da, w_out, ln1_g, ln1_b, ln2_g, ln2_b, w_router, b_router,
             w_exp_gate, w_exp_up, w_exp_down, w_sh_gate, w_sh_up, w_sh_down):
    n_batch, seq, d = x.shape
    ctx_len = ctx.shape[1]
    depth = w_mod.shape[0]
    n_lat = n_batch * seq
    alpha = (2 * depth) ** 0.25
    tiles_per_batch = seq // ROW_TILE
    sub_per_batch = seq // SUB_TILE
    assert d == D_MODEL and seq % ROW_TILE == 0 and (n_batch * ctx_len) % ROW_TILE == 0
    assert ctx_len % SUB_TILE == 0 and n_batch == 4

    cond = jnp.concatenate([c, c_ctx[None], jnp.zeros((8 - n_batch - 1, d), F32)], axis=0)
    mod = _modulation(cond, w_mod, b_mod)
    cos_t, sin_t = _rope_tables(seq)
    h = jnp.concatenate([x.reshape(n_lat, d), ctx.reshape(n_batch * ctx_len, d)], axis=0)

    for l in range(depth):
        last = l == depth - 1
        lam_init = 0.8 - 0.6 * math.exp(-0.3 * l)
        names = ("sh1", "sc1", "g1", "sh2", "sc2", "g2")
        mods = {k: mod[l, :, i * d:(i + 1) * d].reshape(8, 1, d) for i, k in enumerate(names)}

        wi = w_in[l]
        a_w, z_w, q_w, k_w, v_w, g_w = jnp.split(wi, (256, 768, 1280, 1792, 2304), axis=1)
        w_in_l = jnp.concatenate([g_w, q_w, k_w, v_w, a_w, z_w], axis=1).astype(BF16)
        p = _inproj(h, mods["sh1"], mods["sc1"], w_in_l, tiles_per_batch, n_batch)

        lf = da_lam[l].astype(F32)
        lam = (jnp.exp(jnp.sum(lf[0] * lf[1])) - jnp.exp(jnp.sum(lf[2] * lf[3])) + lam_init).reshape(1)
        q, k0, k1, v = _rope_pack(p, cos_t, sin_t, n_batch, seq, ctx_len)
        subln = da_subln_g[l].astype(F32).reshape(1, LANES)
        attn = functools.partial(_attention, lam, q, k0, k1, v, subln, out_scale=1.0 - lam_init)
        yc = attn(n_rows=n_lat, q_row0=0, tq=512, kv_len=ctx_len + seq, tk=ctx_len)
        if not last:
            yc_ctx = attn(n_rows=n_batch * ctx_len, q_row0=n_lat, tq=ctx_len, kv_len=ctx_len, tk=ctx_len)
            yc = jnp.concatenate([yc, yc_ctx], axis=0)

        mats = _s5_matrices(s5_lam_re[l], s5_lam_im[l], s5_log_step[l], s5_b_re[l], s5_b_im[l],
                            s5_c_re[l], s5_c_im[l], s5_d[l])
        ya = _s5_mixer(p, mats, n_batch=n_batch, lat_chunks=seq // S5_CHUNK, ctx_chunks=ctx_len // S5_CHUNK)

        n_rows = n_lat if last else h.shape[0]
        router_w = jnp.concatenate([w_router[l].astype(F32), jnp.zeros((d, LANES - N_EXPERTS), F32)], axis=1)
        merge_w = {
            "glu": jnp.concatenate([w_glu_val[l], w_glu_gate[l]], axis=1).astype(BF16),
            "gm": w_proj_gm[l].astype(BF16), "da": w_proj_da[l].astype(BF16), "out": w_out[l].astype(BF16),
            "gm_ws": gm_w_s[l].astype(BF16),
            "gm_bs": jnp.repeat(gm_b_s[l].astype(F32).T, GM_WIDTH // GM_HEADS, axis=1),
            "router": router_w,
        }
        h1, f, logits = _merge(ya, p, yc, h, mods, ln1_g[l], ln1_b[l], merge_w, n_rows=n_rows,
                               tiles_per_batch=sub_per_batch, n_batch=n_batch, alpha=alpha)

        lrow, w_tok, lrow_t, units_lanes = _route(logits, b_router[l])
        n_tiles = n_rows // SUB_TILE
        units = units_lanes[:, 0].reshape(n_tiles, N_EXPERTS)
        max_units = n_rows * TOP_K // UNIT + n_tiles * N_EXPERTS + N_EXPERTS * (FFN_UNITS - 1)
        n_blocks = max_units // FFN_UNITS + 1
        gmap, tile_units, blk_expert, blk_valid, n_used = _moe_layout(units, n_blocks)
        xs = _dispatch(f, lrow_t, gmap, tile_units, n_blocks * FFN_ROWS)
        ys = _expert_ffn(xs, w_exp_gate, w_exp_up, w_exp_down, l, blk_expert, blk_valid, n_used)
        sgu = jnp.concatenate([w_sh_gate[l], w_sh_up[l]], axis=1).astype(BF16)
        h = _combine(ys, lrow, w_tok, f, sgu, w_sh_down[l].astype(BF16), h1, mods["g2"], ln2_g[l], ln2_b[l],
                     gmap, tile_units, tiles_per_batch=sub_per_batch, n_batch=n_batch, alpha=alpha)

    return h[:n_lat].reshape(n_batch, seq, d)


def kernel(x, c, ctx, c_ctx, w_mod, b_mod, w_in, s5_lam_re, s5_lam_im, s5_log_step, s5_b_re, s5_b_im, s5_c_re, s5_c_im, s5_d, gm_w_s, gm_b_s, da_lam, da_subln_g, w_glu_val, w_glu_gate, w_proj_gm, w_proj_da, w_out, ln1_g, ln1_b, ln2_g, ln2_b, w_router, b_router, w_exp_gate, w_exp_up, w_exp_down, w_sh_gate, w_sh_up, w_sh_down):
    return _forward(x, c, ctx, c_ctx, w_mod, b_mod, w_in, s5_lam_re, s5_lam_im, s5_log_step, s5_b_re, s5_b_im,
                    s5_c_re, s5_c_im, s5_d, gm_w_s, gm_b_s, da_lam, da_subln_g, w_glu_val, w_glu_gate,
                    w_proj_gm, w_proj_da, w_out, ln1_g, ln1_b, ln2_g, ln2_b, w_router, b_router,
                    w_exp_gate, w_exp_up, w_exp_down, w_sh_gate, w_sh_up, w_sh_down)
```

```python
import functools
import math

import jax
import jax.numpy as jnp
from jax import lax
from jax.experimental import pallas as pl
from jax.experimental.pallas import tpu as pltpu

F32 = jnp.float32
BF16 = jnp.bfloat16
HIGHEST = lax.Precision.HIGHEST

D_MODEL = 1024
GRID_W = 64
S5_WIDTH = 256
S5_GROUP = 16
S5_GROUPS = S5_WIDTH // S5_GROUP
S5_STATE = 64
S5_CHUNK = 16
GM_WIDTH = 256
GM_HEADS = 4
GM_CHUNK = 128
DA_HEADS = 4
DA_HEAD_DIM = 64
DA_QK = DA_HEADS * 2 * DA_HEAD_DIM
DA_V = DA_HEADS * 2 * DA_HEAD_DIM
ROPE_BASE = 10000.0
N_BRANCH = 3
N_EXPERTS = 64
TOP_K = 8
N_GROUPS = 8
TOPK_GROUPS = 4
EXPERT_HIDDEN = 256
ROUTED_SCALE = 2.5
LN_EPS = 1e-5

COL_GATE = 0
COL_Q = N_BRANCH * D_MODEL
COL_K = COL_Q + DA_QK
COL_V = COL_K + DA_QK
COL_S5 = COL_V + DA_V
COL_ZU = COL_S5 + S5_WIDTH
COL_ZV = COL_ZU + GM_WIDTH
IN_WIDTH = COL_ZV + GM_WIDTH

LANES = 128
ROW_TILE = 1024
SUB_TILE = 256
MERGE_TILE = 512
VMEM_LIMIT = 48 * 1024 * 1024
UNIT = 16
LOCAL_ROWS = SUB_TILE * TOP_K + N_EXPERTS * UNIT
LOCAL_UNITS = LOCAL_ROWS // UNIT
FFN_ROWS = 512
FFN_UNITS = FFN_ROWS // UNIT
ATTN_ROW_BLOCK = 64
ONEHOT_BLOCK = 128
MOE_CHUNK = 512


def _cparams(*sem):
    return pltpu.CompilerParams(dimension_semantics=sem, vmem_limit_bytes=VMEM_LIMIT)


def _ln(x):
    mu = jnp.mean(x, -1, keepdims=True)
    xc = x - mu
    var = jnp.mean(xc * xc, -1, keepdims=True)
    return xc * lax.rsqrt(var + LN_EPS)


def _gelu(x):
    return 0.5 * x * (1.0 + jnp.tanh(math.sqrt(2.0 / math.pi) * (x + 0.044715 * (x * x * x))))


def _sigmoid(x):
    return 1.0 / (1.0 + jnp.exp(-x))


def _silu(x):
    return x * _sigmoid(x)


def _mod_kernel(c_ref, w_ref, b_ref, o_ref):
    s = _silu(c_ref[...])
    o_ref[...] = jnp.dot(s, w_ref[...], preferred_element_type=F32, precision=HIGHEST) + b_ref[...]


def _modulation(cond, w_mod, b_mod):
    depth, d, n = w_mod.shape
    tn = 1536
    return pl.pallas_call(
        _mod_kernel,
        grid=(depth, n // tn),
        in_specs=[pl.BlockSpec((8, d), lambda l, j: (0, 0)),
                  pl.BlockSpec((None, d, tn), lambda l, j: (l, 0, j)),
                  pl.BlockSpec((None, 1, tn), lambda l, j: (l, 0, j))],
        out_specs=pl.BlockSpec((None, 8, tn), lambda l, j: (l, 0, j)),
        out_shape=jax.ShapeDtypeStruct((depth, 8, n), F32),
        compiler_params=_cparams("arbitrary", "arbitrary"),
        name="modulation",
    )(cond, w_mod, b_mod.reshape(depth, 1, n))


def _inproj_kernel(h_ref, sh_ref, sc_ref, w_ref, o_ref, xn_ref):
    @pl.when(pl.program_id(1) == 0)
    def _():
        x = _ln(h_ref[...])
        xn_ref[...] = (x * (1.0 + sc_ref[...]) + sh_ref[...]).astype(BF16)

    o_ref[...] = jnp.dot(xn_ref[...], w_ref[...], preferred_element_type=F32).astype(BF16)


def _inproj(h, shift, scale, w, tiles_per_batch, n_batch):
    t, d = h.shape
    n = w.shape[1]
    tn = n // 2
    mod_row = lambda i, j: (jnp.minimum(i // tiles_per_batch, n_batch), 0, 0)
    return pl.pallas_call(
        _inproj_kernel,
        grid=(t // ROW_TILE, n // tn),
        in_specs=[pl.BlockSpec((ROW_TILE, d), lambda i, j: (i, 0)),
                  pl.BlockSpec((None, 1, d), mod_row),
                  pl.BlockSpec((None, 1, d), mod_row),
                  pl.BlockSpec((d, tn), lambda i, j: (0, j))],
        out_specs=pl.BlockSpec((ROW_TILE, tn), lambda i, j: (i, j)),
        out_shape=jax.ShapeDtypeStruct((t, n), BF16),
        scratch_shapes=[pltpu.VMEM((ROW_TILE, d), BF16)],
        compiler_params=_cparams("arbitrary", "arbitrary"),
        name="inproj",
    )(h, shift, scale, w)


def _rope_kernel(q_ref, k_ref, v_ref, cos_ref, sin_ref, qo_ref, k0_ref, k1_ref, vo_ref, *, n_lat_tiles):
    is_lat = pl.program_id(0) < n_lat_tiles
    shape = q_ref.shape
    lane = lax.broadcasted_iota(jnp.int32, shape, 1)
    upper16 = (lane % 32) >= 16
    map1 = (lane % LANES) >= DA_HEAD_DIM
    cos = jnp.where(is_lat, cos_ref[...], 1.0)
    sin = jnp.where(is_lat, sin_ref[...], 0.0)

    def rope(x):
        partner = jnp.where(upper16, pltpu.roll(x, 16, 1), pltpu.roll(x, shape[1] - 16, 1))
        return x * cos + partner * sin

    q = rope(q_ref[...].astype(F32)) * (DA_HEAD_DIM ** -0.5)
    k = rope(k_ref[...].astype(F32))
    qo_ref[...] = q.astype(BF16)
    k0_ref[...] = jnp.where(map1, 0.0, k).astype(BF16)
    k1_ref[...] = jnp.where(map1, k, 0.0).astype(BF16)
    vo_ref[...] = v_ref[...]


def _rope_pack(p, cos_t, sin_t, n_batch, seq, ctx):
    t = p.shape[0]
    nl = seq // SUB_TILE
    nc = ctx // SUB_TILE
    n_lat_tiles = n_batch * nl

    def kv_map(i):
        j = i - n_lat_tiles
        b = jnp.where(i < n_lat_tiles, i // nl, j // nc)
        blk = jnp.where(i < n_lat_tiles, nc + i % nl, j % nc)
        return (b, blk, 0)

    tab_map = lambda i: (jnp.where(i < n_lat_tiles, i % nl, 0), 0)
    col = lambda c: (lambda i: (i, c // DA_QK))
    kv_shape = jax.ShapeDtypeStruct((n_batch, ctx + seq, DA_QK), BF16)
    return pl.pallas_call(
        functools.partial(_rope_kernel, n_lat_tiles=n_lat_tiles),
        grid=(t // SUB_TILE,),
        in_specs=[pl.BlockSpec((SUB_TILE, DA_QK), col(COL_Q)),
                  pl.BlockSpec((SUB_TILE, DA_QK), col(COL_K)),
                  pl.BlockSpec((SUB_TILE, DA_V), col(COL_V)),
                  pl.BlockSpec((SUB_TILE, DA_QK), tab_map),
                  pl.BlockSpec((SUB_TILE, DA_QK), tab_map)],
        out_specs=[pl.BlockSpec((SUB_TILE, DA_QK), lambda i: (i, 0)),
                   pl.BlockSpec((None, SUB_TILE, DA_QK), kv_map),
                   pl.BlockSpec((None, SUB_TILE, DA_QK), kv_map),
                   pl.BlockSpec((None, SUB_TILE, DA_V), kv_map)],
        out_shape=[jax.ShapeDtypeStruct((t, DA_QK), BF16), kv_shape, kv_shape, kv_shape],
        compiler_params=_cparams("arbitrary"),
        name="rope_pack",
    )(p, p, p, cos_t, sin_t)


def _rope_tables(seq):
    pos = jnp.arange(seq)
    row = (pos // GRID_W).astype(F32)[:, None]
    colp = (pos % GRID_W).astype(F32)[:, None]
    axis_dim = DA_HEAD_DIM // 2
    inv_freq = ROPE_BASE ** (-jnp.arange(0, axis_dim, 2, dtype=F32) / axis_dim)
    ang_r = row * inv_freq
    ang_c = colp * inv_freq
    ang = jnp.concatenate([ang_r, ang_r, ang_c, ang_c], -1)
    sign = jnp.concatenate([-jnp.ones((16,), F32), jnp.ones((16,), F32)] * 2)
    reps = DA_QK // DA_HEAD_DIM
    return jnp.tile(jnp.cos(ang), (1, reps)), jnp.tile(jnp.sin(ang) * sign, (1, reps))


def _attn_kernel(lam_ref, q_ref, k0_ref, k1_ref, v_ref, g_ref, o_ref, *scratch, n_chunks, tk, out_scale):
    q = q_ref[...]
    nt = (((1,), (1,)), ((), ()))
    k_refs = (k0_ref, k1_ref)
    s_refs = (scratch[0:2], scratch[2:4])
    m_refs, l_refs, acc_refs, p_refs = scratch[4:6], scratch[6:8], scratch[8:10], scratch[10:12]

    def chunk(j):
        return pl.ds(pl.multiple_of(j * tk, tk), tk)

    def scores(j, buf):
        for mp in range(2):
            s_refs[buf][mp][...] = lax.dot_general(q, k_refs[mp][chunk(j), :], nt, preferred_element_type=F32)

    def absorb(j, buf):
        vc = v_ref[chunk(j), :]
        for mp in range(2):
            for r0 in range(0, q.shape[0], ATTN_ROW_BLOCK):
                rows = slice(r0, r0 + ATTN_ROW_BLOCK)
                s = s_refs[buf][mp][rows, :]
                m = m_refs[mp][rows, :]
                m_new = jnp.maximum(m, jnp.broadcast_to(jnp.max(s, -1, keepdims=True), m.shape))
                alpha = jnp.exp(m - m_new)
                p = jnp.exp(s - jnp.concatenate([m_new] * (tk // LANES), axis=1))
                l_refs[mp][rows, :] = (alpha * l_refs[mp][rows, :]
                                       + jnp.broadcast_to(jnp.sum(p, -1, keepdims=True), m.shape))
                acc_refs[mp][rows, :] = alpha * acc_refs[mp][rows, :]
                m_refs[mp][rows, :] = m_new
                p_refs[mp][rows, :] = p.astype(BF16)
            acc_refs[mp][...] += jnp.dot(p_refs[mp][...], vc, preferred_element_type=F32)

    for mp in range(2):
        m_refs[mp][...] = jnp.full(m_refs[mp].shape, -jnp.inf, F32)
        l_refs[mp][...] = jnp.zeros(l_refs[mp].shape, F32)
        acc_refs[mp][...] = jnp.zeros(acc_refs[mp].shape, F32)
    scores(0, 0)

    def body(i, carry):
        scores(2 * i + 1, 1)
        absorb(2 * i, 0)
        scores(2 * i + 2, 0)
        absorb(2 * i + 1, 1)
        return carry

    lax.fori_loop(0, (n_chunks - 1) // 2, body, 0)
    absorb(n_chunks - 1, 0)
    o = acc_refs[0][...] / l_refs[0][...] - lam_ref[0] * (acc_refs[1][...] / l_refs[1][...])
    o = o * lax.rsqrt(jnp.mean(o * o, -1, keepdims=True) + LN_EPS)
    o_ref[...] = (o * g_ref[...] * out_scale).astype(BF16)


def _attention(lam, q, k0, k1, v, subln_g, *, n_rows, q_row0, tq, kv_len, tk, out_scale):
    n_batch = k0.shape[0]
    per_batch = n_rows // n_batch // tq
    q0 = q_row0 // tq
    kv_spec = pl.BlockSpec((None, kv_len, LANES), lambda b, h, i: (b, 0, h))
    return pl.pallas_call(
        functools.partial(_attn_kernel, n_chunks=kv_len // tk, tk=tk, out_scale=out_scale),
        grid=(n_batch, DA_HEADS, per_batch),
        scratch_shapes=([pltpu.VMEM((tq, tk), F32)] * 4 + [pltpu.VMEM((tq, LANES), F32)] * 6
                        + [pltpu.VMEM((tq, tk), BF16)] * 2),
        in_specs=[pl.BlockSpec(memory_space=pltpu.SMEM),
                  pl.BlockSpec((tq, LANES), lambda b, h, i: (q0 + b * per_batch + i, h)),
                  kv_spec, kv_spec, kv_spec,
                  pl.BlockSpec((1, LANES), lambda b, h, i: (0, 0))],
        out_specs=pl.BlockSpec((tq, LANES), lambda b, h, i: (b * per_batch + i, h)),
        out_shape=jax.ShapeDtypeStruct((n_rows, DA_V), BF16),
        compiler_params=_cparams("arbitrary", "arbitrary", "arbitrary"),
        name="diff_attention",
    )(lam, q, k0, k1, v, subln_g)


def _s5_in_kernel(x_ref, b_ref, u_ref, z_ref, stage_ref):
    @pl.when(pl.program_id(1) == 0)
    def _():
        x = x_ref[...].astype(F32)
        for half in range(S5_WIDTH // LANES):
            stage_ref[half] = x[:, half * LANES:(half + 1) * LANES]
        for s in range(S5_CHUNK):
            for half in range(S5_WIDTH // LANES):
                col = s * S5_WIDTH + half * LANES
                u_ref[:, col:col + LANES] = (
                    stage_ref[half, pl.ds(s, u_ref.shape[0], stride=S5_CHUNK), :].astype(BF16))

    z_ref[...] = jnp.dot(u_ref[...], b_ref[...], preferred_element_type=F32)


def _s5_scan_kernel(z_ref, a1_ref, a2_ref, p_ref, *, n_batch, lat_chunks, ctx_chunks):
    reverse = pl.program_id(0) == 1
    a1 = a1_ref[...]
    a2 = a2_ref[...]
    width = a1.shape[1]

    def swap_halves(s):
        return jnp.concatenate(
            [pltpu.roll(s[:, j * LANES:(j + 1) * LANES], LANES // 2, 1) for j in range(width // LANES)], axis=1)

    def run(base, count, carry):
        def body(i, st):
            s, ssw = st
            row = base + jnp.where(reverse, count - 1 - i, i)
            z = z_ref[pl.ds(row, 1), :]
            p_ref[pl.ds(row, 1), :] = s
            return a1 * s + a2 * ssw + z, a1 * ssw - a2 * s + swap_halves(z)
        return lax.fori_loop(0, count, body, carry, unroll=4)

    zero = jnp.zeros(a1.shape, F32)
    for b in range(n_batch):
        st = run(n_batch * lat_chunks + b * ctx_chunks, ctx_chunks, (zero, zero))
        run(b * lat_chunks, lat_chunks, st)


def _s5_out_kernel(u_ref, p_ref, t_ref, ct_ref, y_ref, stage_ref):
    y = jnp.dot(u_ref[...], t_ref[...], preferred_element_type=F32)
    y = y + lax.dot_general(p_ref[...].astype(BF16), ct_ref[...], (((1,), (1,)), ((), ())),
                            preferred_element_type=F32)
    y = _gelu(y)
    per_step = y.shape[1] // S5_WIDTH
    first = pl.program_id(1) * per_step
    halves = S5_WIDTH // LANES
    for k in range(per_step):
        for half in range(halves):
            col = k * S5_WIDTH + half * LANES
            stage_ref[half, pl.ds(first + k, y.shape[0], stride=S5_CHUNK), :] = y[:, col:col + LANES]

    @pl.when(pl.program_id(1) == pl.num_programs(1) - 1)
    def _():
        for half in range(halves):
            y_ref[:, half * LANES:(half + 1) * LANES] = stage_ref[half]


def _s5_mixer(p, mats, *, n_batch, lat_chunks, ctx_chunks):
    t_all = p.shape[0]
    r = t_all // S5_CHUNK
    w = S5_CHUNK * S5_WIDTH
    tr = r // 4
    state_w = mats["b"].shape[1]
    tn = 1024
    u, z = pl.pallas_call(
        _s5_in_kernel,
        grid=(r // tr, state_w // tn),
        in_specs=[pl.BlockSpec((tr * S5_CHUNK, S5_WIDTH), lambda i, j: (i, COL_S5 // S5_WIDTH)),
                  pl.BlockSpec((w, tn), lambda i, j: (0, j))],
        out_specs=[pl.BlockSpec((tr, w), lambda i, j: (i, 0)),
                   pl.BlockSpec((tr, tn), lambda i, j: (i, j))],
        out_shape=[jax.ShapeDtypeStruct((r, w), BF16), jax.ShapeDtypeStruct((r, state_w), F32)],
        scratch_shapes=[pltpu.VMEM((S5_WIDTH // LANES, tr * S5_CHUNK, LANES), F32)],
        compiler_params=_cparams("arbitrary", "arbitrary"),
        name="s5_in",
    )(p, mats["b"])
    half = state_w // 2
    coef = pl.BlockSpec((None, 1, half), lambda d: (d, 0, 0))
    prev = pl.pallas_call(
        functools.partial(_s5_scan_kernel, n_batch=n_batch, lat_chunks=lat_chunks, ctx_chunks=ctx_chunks),
        grid=(2,),
        in_specs=[pl.BlockSpec((r, half), lambda d: (0, d)), coef, coef],
        out_specs=pl.BlockSpec((r, half), lambda d: (0, d)),
        out_shape=jax.ShapeDtypeStruct((r, state_w), F32),
        compiler_params=_cparams("arbitrary"),
        name="s5_scan",
    )(z, mats["a1"], mats["a2"])
    tn = 512
    return pl.pallas_call(
        _s5_out_kernel,
        grid=(r // tr, w // tn),
        in_specs=[pl.BlockSpec((tr, w), lambda i, j: (i, 0)),
                  pl.BlockSpec((tr, state_w), lambda i, j: (i, 0)),
                  pl.BlockSpec((w, tn), lambda i, j: (0, j)),
                  pl.BlockSpec((tn, state_w), lambda i, j: (j, 0))],
        out_specs=pl.BlockSpec((tr * S5_CHUNK, S5_WIDTH), lambda i, j: (i, 0)),
        out_shape=jax.ShapeDtypeStruct((t_all, S5_WIDTH), F32),
        scratch_shapes=[pltpu.VMEM((S5_WIDTH // LANES, tr * S5_CHUNK, LANES), F32)],
        compiler_params=_cparams("arbitrary", "arbitrary"),
        name="s5_out",
    )(u, prev, mats["t"], mats["ct"])


def _s5_matrices(lam_re, lam_im, log_step, b_re, b_im, c_re, c_im, d_skip):
    n = S5_CHUNK
    dt = jnp.exp(log_step.astype(F32))[..., None]
    lr, li = lam_re.astype(F32), lam_im.astype(F32)
    mag = jnp.exp(lr * dt)
    a_re, a_im = mag * jnp.cos(li * dt), mag * jnp.sin(li * dt)
    den = lr * lr + li * li
    n_re = a_re - 1.0
    z_re = (n_re * lr + a_im * li) / den
    z_im = (a_im * lr - n_re * li) / den
    br, bi = b_re.astype(F32), b_im.astype(F32)
    bb_re = z_re[..., None] * br - z_im[..., None] * bi
    bb_im = z_re[..., None] * bi + z_im[..., None] * br
    j = jnp.arange(n + 1, dtype=F32)[:, None, None, None]
    pmag = jnp.exp(lr * dt * j)
    pw_re, pw_im = pmag * jnp.cos(li * dt * j), pmag * jnp.sin(li * dt * j)
    cr, ci = c_re.astype(F32), c_im.astype(F32)
    g, p, c = S5_GROUPS, S5_STATE, S5_GROUP

    def cmul(xr, xi, yr, yi):
        return xr * yr - xi * yi, xr * yi + xi * yr

    def in_mat(direction, powers):
        er = pw_re[powers, direction][:, :, :, None]
        ei = pw_im[powers, direction][:, :, :, None]
        xr, xi = cmul(er, ei, bb_re[direction][None], bb_im[direction][None])
        m = jnp.concatenate([xr, xi], axis=2)
        return m.transpose(1, 0, 3, 2).reshape(g, n * c, 2 * p)

    def out_mat(direction, powers):
        er = pw_re[powers, direction][:, :, None, :]
        ei = pw_im[powers, direction][:, :, None, :]
        wr, wi = cmul(cr[direction][None], ci[direction][None], er, ei)
        m = jnp.concatenate([wr, -wi], axis=3)
        return m.transpose(1, 3, 0, 2).reshape(g, 2 * p, n * c)

    def toeplitz(direction):
        er = pw_re[:n, direction][:, :, None, :, None]
        ei = pw_im[:n, direction][:, :, None, :, None]
        wr, wi = cmul(cr[direction][None, :, :, :, None], ci[direction][None, :, :, :, None], er, ei)
        k = jnp.sum(wr * bb_re[direction][None, :, None] - wi * bb_im[direction][None, :, None], axis=3)
        return k

    up, down = slice(0, n), slice(n - 1, None, -1)
    up1, down1 = slice(1, n + 1), slice(n, 0, -1)
    kf, kb = toeplitz(0), toeplitz(1)
    skip = d_skip.astype(F32).reshape(g, c)
    k0 = kf[0] + kb[0] + jnp.eye(c, dtype=F32)[None] * skip[:, :, None]
    by_lag = jnp.concatenate([kb[:0:-1], k0[None], kf[1:]], axis=0)
    eye_g = jnp.eye(g, dtype=F32)
    lag_blocks = jnp.einsum("mgca,gh->mgahc", by_lag, eye_g).reshape(2 * n - 1, g * c, g * c)
    steps = jnp.arange(n)
    t_dense = lag_blocks[steps[None, :] - steps[:, None] + n - 1]
    t_dense = t_dense.transpose(0, 2, 1, 3).reshape(n * g * c, n * g * c)

    def coef(direction):
        ar, ai = pw_re[n, direction], pw_im[n, direction]
        return (jnp.concatenate([ar, ar], -1).reshape(-1), jnp.concatenate([-ai, ai], -1).reshape(-1))

    def dense_in(m):
        m = m.reshape(g, n, c, -1)
        return jnp.einsum("gscq,gh->sgchq", m, eye_g).reshape(n * g * c, -1)

    def dense_out_t(m):
        q = m.shape[1]
        return jnp.einsum("gqtc,gh->thcgq", m.reshape(g, q, n, c), eye_g).reshape(n * g * c, g * q)

    a1f, a2f = coef(0)
    a1b, a2b = coef(1)
    return {
        "b": jnp.concatenate([dense_in(in_mat(0, down)), dense_in(in_mat(1, up))], 1).astype(BF16),
        "ct": jnp.concatenate([dense_out_t(out_mat(0, up1)), dense_out_t(out_mat(1, down1))], 1).astype(BF16),
        "t": t_dense.astype(BF16),
        "a1": jnp.stack([a1f, a1b])[:, None, :], "a2": jnp.stack([a2f, a2b])[:, None, :],
    }


def _merge_kernel(ya_ref, zu_ref, zv_ref, yc_ref, ga_ref, gb_ref, gc_ref, h_ref,
                  g1_ref, sh2_ref, sc2_ref, lng_ref, lnb_ref,
                  wglu_ref, wgm_ref, wda_ref, wout_ref, ws_ref, bs_ref, wr_ref,
                  h1_ref, f_ref, lg_ref, *, alpha):
    f32 = lambda ref: ref[...].astype(F32)
    glu = jnp.dot(ya_ref[...].astype(BF16), wglu_ref[...], preferred_element_type=F32)
    branch_a = glu[:, :D_MODEL] * _sigmoid(glu[:, D_MODEL:])

    u = _gelu(f32(zu_ref))
    v = _ln(_gelu(f32(zv_ref))).astype(BF16)
    head = lax.broadcasted_iota(jnp.int32, (GM_CHUNK, GM_WIDTH), 1) // (GM_WIDTH // GM_HEADS)
    parts = []
    for ck in range(v.shape[0] // GM_CHUNK):
        vc = v[ck * GM_CHUNK:(ck + 1) * GM_CHUNK]
        s = bs_ref[...]
        for hd in range(GM_HEADS):
            s = s + jnp.dot(ws_ref[hd], jnp.where(head == hd, vc, jnp.zeros_like(vc)),
                            preferred_element_type=F32)
        parts.append(s)
    yb = (u * jnp.concatenate(parts, axis=0)).astype(BF16)

    m = _sigmoid(f32(ga_ref)) * branch_a
    m = m + _sigmoid(f32(gb_ref)) * jnp.dot(yb, wgm_ref[...], preferred_element_type=F32)
    m = m + _sigmoid(f32(gc_ref)) * jnp.dot(yc_ref[...], wda_ref[...], preferred_element_type=F32)
    mix = jnp.dot(m.astype(BF16), wout_ref[...], preferred_element_type=F32)

    h1 = _ln(alpha * h_ref[...] + g1_ref[...] * mix) * lng_ref[...] + lnb_ref[...]
    h1_ref[...] = h1
    f = _ln(h1) * (1.0 + sc2_ref[...]) + sh2_ref[...]
    f_ref[...] = f.astype(BF16)
    lg_ref[...] = jnp.dot(f, wr_ref[...], preferred_element_type=F32, precision=HIGHEST)


def _merge(ya, p, yc, h, mods, ln_g, ln_b, w, *, n_rows, tiles_per_batch, n_batch, alpha):
    d = D_MODEL
    tm = MERGE_TILE
    row = lambda i: (i, 0)
    mod_row = lambda i: (jnp.minimum(i // tiles_per_batch, n_batch), 0, 0)
    pcol = lambda c, width: pl.BlockSpec((tm, width), lambda i: (i, c // width))
    full = lambda a: pl.BlockSpec(a.shape, lambda i: (0,) * a.ndim)
    mod_spec = pl.BlockSpec((None, 1, d), mod_row)
    vec = lambda a: a.reshape(1, d)
    weights = (w["glu"], w["gm"], w["da"], w["out"], w["gm_ws"], w["gm_bs"], w["router"])
    return pl.pallas_call(
        functools.partial(_merge_kernel, alpha=alpha),
        grid=(n_rows // tm,),
        in_specs=[pl.BlockSpec((tm, S5_WIDTH), row),
                  pcol(COL_ZU, GM_WIDTH), pcol(COL_ZV, GM_WIDTH),
                  pl.BlockSpec((tm, DA_V), row),
                  pcol(COL_GATE, d), pcol(COL_GATE + d, d), pcol(COL_GATE + 2 * d, d),
                  pl.BlockSpec((tm, d), row),
                  mod_spec, mod_spec, mod_spec,
                  pl.BlockSpec((1, d), lambda i: (0, 0)), pl.BlockSpec((1, d), lambda i: (0, 0))]
                 + [full(a) for a in weights],
        out_specs=[pl.BlockSpec((tm, d), row), pl.BlockSpec((tm, d), row), pl.BlockSpec((tm, LANES), row)],
        out_shape=[jax.ShapeDtypeStruct((n_rows, d), F32), jax.ShapeDtypeStruct((n_rows, d), BF16),
                   jax.ShapeDtypeStruct((n_rows, LANES), F32)],
        compiler_params=_cparams("arbitrary"),
        name="merge",
    )(ya, p, p, yc, p, p, p, h, mods["g1"], mods["sh2"], mods["sc2"], vec(ln_g), vec(ln_b), *weights)


def _router_kernel(lg_ref, b_ref, before_ref, lrow_ref, w_ref, lrow_t_ref, units_ref):
    tm = lg_ref.shape[0]
    per_group = N_EXPERTS // N_GROUPS
    neg = -jnp.inf
    logits = lg_ref[...].T[:N_EXPERTS]
    scores = _sigmoid(logits).reshape(N_GROUPS, per_group, tm)
    sel = scores + b_ref[...].reshape(N_GROUPS, per_group, 1)

    in_group = lax.broadcasted_iota(jnp.int32, sel.shape, 1)
    top1 = jnp.max(sel, axis=1, keepdims=True)
    first = jnp.min(jnp.where(sel == top1, in_group, per_group), axis=1, keepdims=True)
    top2 = jnp.max(jnp.where(in_group == first, neg, sel), axis=1, keepdims=True)
    gscore = top1 + top2

    gidx = lax.broadcasted_iota(jnp.int32, gscore.shape, 0)
    gsel = jnp.zeros(gscore.shape, jnp.bool_)
    for _ in range(TOPK_GROUPS):
        best = jnp.max(gscore, axis=0, keepdims=True)
        hit = gidx == jnp.min(jnp.where(gscore == best, gidx, N_GROUPS), axis=0, keepdims=True)
        gsel = gsel | hit
        gscore = jnp.where(hit, neg, gscore)

    eidx = lax.broadcasted_iota(jnp.int32, sel.shape, 0) * per_group + in_group
    cand = jnp.where(gsel, sel, neg)
    chosen = jnp.zeros(sel.shape, jnp.bool_)
    hits = []
    for _ in range(TOP_K):
        best = jnp.max(jnp.max(cand, axis=1, keepdims=True), axis=0, keepdims=True)
        at = jnp.where(cand == best, eidx, N_EXPERTS)
        hit = eidx == jnp.min(jnp.min(at, axis=1, keepdims=True), axis=0, keepdims=True)
        hits.append(hit)
        chosen = chosen | hit
        cand = jnp.where(hit, neg, cand)

    w = jnp.where(chosen, scores, 0.0)
    total = jnp.sum(jnp.sum(w, axis=1, keepdims=True), axis=0, keepdims=True)
    w = w / total * ROUTED_SCALE

    onehot = jnp.where(chosen, 1.0, 0.0).reshape(N_EXPERTS, tm).astype(BF16)
    rank = jnp.dot(onehot, before_ref[...], preferred_element_type=F32)
    count = jnp.dot(onehot, jnp.ones((tm, LANES), BF16), preferred_element_type=F32)
    units = jnp.floor((count + (UNIT - 1)) * (1.0 / UNIT))
    ei = lax.broadcasted_iota(jnp.int32, (N_EXPERTS, N_EXPERTS), 0)
    ej = lax.broadcasted_iota(jnp.int32, (N_EXPERTS, N_EXPERTS), 1)
    first_unit = jnp.dot(jnp.where(ej < ei, 1.0, 0.0).astype(BF16), units.astype(BF16),
                         preferred_element_type=F32)
    base = jnp.concatenate([first_unit * UNIT] * (tm // LANES), axis=1)
    pos = (base + rank).reshape(sel.shape)

    pick = lambda hit, val: jnp.sum(jnp.sum(jnp.where(hit, val, 0.0), axis=1, keepdims=True), axis=0)
    pad = jnp.zeros((LANES - TOP_K, tm), F32)
    lrow_t = jnp.concatenate([pick(hit, pos) for hit in hits] + [pad], axis=0)
    w_t = jnp.concatenate([pick(hit, w) for hit in hits] + [pad], axis=0)
    lrow_t_ref[...] = lrow_t[:TOP_K].astype(jnp.int32)
    lrow_ref[...] = lrow_t.T.astype(jnp.int32)
    w_ref[...] = w_t.T
    units_ref[...] = units.astype(jnp.int32)


def _route(logits, b_router):
    t = logits.shape[0]
    tm = SUB_TILE
    n_sub = t // tm
    before = jnp.triu(jnp.ones((tm, tm), F32), 1).astype(BF16)
    tok = pl.BlockSpec((tm, LANES), lambda i: (i, 0))
    pick = pl.BlockSpec((TOP_K, tm), lambda i: (0, i))
    return pl.pallas_call(
        _router_kernel,
        grid=(n_sub,),
        in_specs=[tok, pl.BlockSpec((N_EXPERTS, 1), lambda i: (0, 0)), pl.BlockSpec((tm, tm), lambda i: (0, 0))],
        out_specs=[tok, tok, pick, pl.BlockSpec((N_EXPERTS, LANES), lambda i: (i, 0))],
        out_shape=[jax.ShapeDtypeStruct((t, LANES), jnp.int32), jax.ShapeDtypeStruct((t, LANES), F32),
                   jax.ShapeDtypeStruct((TOP_K, t), jnp.int32),
                   jax.ShapeDtypeStruct((n_sub * N_EXPERTS, LANES), jnp.int32)],
        compiler_params=_cparams("arbitrary"),
        name="router",
    )(logits, b_router.astype(F32).reshape(N_EXPERTS, 1), before)


def _moe_layout(units, n_blocks):
    per_expert = jnp.sum(units, axis=0)
    padded = (per_expert + FFN_UNITS - 1) // FFN_UNITS * FFN_UNITS
    ends = jnp.cumsum(padded)
    starts = ends - padded
    goff = starts[None, :] + jnp.cumsum(units, axis=0) - units
    local_end = jnp.cumsum(units, axis=1)
    lu = jnp.arange(LOCAL_UNITS, dtype=jnp.int32)
    owner = jnp.minimum(jnp.sum(local_end[:, None, :] <= lu[None, :, None], axis=2), N_EXPERTS - 1)
    is_owner = owner[:, :, None] == jnp.arange(N_EXPERTS, dtype=jnp.int32)[None, None, :]
    shift = goff - (local_end - units)
    gmap = lu[None, :] + jnp.sum(jnp.where(is_owner, shift[:, None, :], 0), axis=2)
    n_used = ends[-1] // FFN_UNITS
    blk = jnp.arange(n_blocks, dtype=jnp.int32)
    blk = jnp.minimum(blk, n_used - 1)
    first = blk * FFN_UNITS
    expert = jnp.minimum(jnp.sum(ends[None, :] <= first[:, None], axis=1), N_EXPERTS - 1)
    of_block = expert[:, None] == jnp.arange(N_EXPERTS, dtype=jnp.int32)[None, :]
    range_end = jnp.sum(jnp.where(of_block, (starts + per_expert)[None, :], 0), axis=1)
    valid = jnp.clip(range_end - first, 0, FFN_UNITS)
    i32 = lambda a: a.astype(jnp.int32)
    return i32(gmap.reshape(-1)), i32(local_end[:, -1]), i32(expert), i32(valid), i32(n_used.reshape(1))


def _unit_rows(ref, unit):
    return ref.at[pl.ds(pl.multiple_of(unit * UNIT, UNIT), UNIT), :]


def _for_each_unit(gmap_ref, count_ref, tile, fn):
    def body(lu, carry):
        fn(lu, gmap_ref[tile * LOCAL_UNITS + lu])
        return carry

    lax.fori_loop(0, count_ref[tile], body, 0)


def _dispatch_kernel(gmap_ref, count_ref, f_ref, lrow_t_ref, xs_hbm, buf_ref, sem_ref, *onehot_refs):
    s = pl.program_id(0)
    last = pl.num_programs(0) - 1
    slot = s % 2

    def copy(slot, lu, gu):
        return pltpu.make_async_copy(_unit_rows(buf_ref.at[slot], lu), _unit_rows(xs_hbm, gu), sem_ref.at[slot])

    def drain(step):
        def body(i, carry):
            copy(step % 2, 0, 0).wait()
            return carry
        lax.fori_loop(0, count_ref[step], body, 0)

    @pl.when(s >= 2)
    def _():
        drain(s - 2)

    lt = lrow_t_ref[...]
    f = f_ref[...]
    chunk = MOE_CHUNK

    def sort_chunk(r0, onehot_ref):
        for p0 in range(0, chunk, ONEHOT_BLOCK):
            row = lax.broadcasted_iota(jnp.int32, (ONEHOT_BLOCK, SUB_TILE), 0) + (r0 + p0)
            hit = row == lt[0:1, :]
            for k in range(1, TOP_K):
                hit = hit | (row == lt[k:k + 1, :])
            onehot_ref[p0:p0 + ONEHOT_BLOCK, :] = jnp.where(hit, 1.0, 0.0).astype(BF16)
        buf_ref[slot, r0:r0 + chunk, :] = jnp.dot(onehot_ref[...], f, preferred_element_type=F32).astype(BF16)

    for i, r0 in enumerate(range(0, LOCAL_ROWS, chunk)):
        if r0 + chunk <= SUB_TILE * TOP_K:
            sort_chunk(r0, onehot_refs[i])
        else:
            pl.when(r0 < count_ref[s] * UNIT)(functools.partial(sort_chunk, r0, onehot_refs[i]))

    _for_each_unit(gmap_ref, count_ref, s, lambda lu, gu: copy(slot, lu, gu).start())

    @pl.when(s == last)
    def _():
        drain(s)

        @pl.when(s >= 1)
        def _():
            drain(s - 1)


def _dispatch(f, lrow_t, gmap, count, n_rows_out):
    t, d = f.shape
    grid_spec = pltpu.PrefetchScalarGridSpec(
        num_scalar_prefetch=2,
        grid=(t // SUB_TILE,),
        in_specs=[pl.BlockSpec((SUB_TILE, d), lambda s, u, g: (s, 0)),
                  pl.BlockSpec((TOP_K, SUB_TILE), lambda s, u, g: (0, s))],
        out_specs=pl.BlockSpec(memory_space=pl.ANY),
        scratch_shapes=([pltpu.VMEM((2, LOCAL_ROWS, d), BF16), pltpu.SemaphoreType.DMA((2,))]
                        + [pltpu.VMEM((MOE_CHUNK, SUB_TILE), BF16)] * (LOCAL_ROWS // MOE_CHUNK)))
    return pl.pallas_call(
        _dispatch_kernel,
        grid_spec=grid_spec,
        out_shape=jax.ShapeDtypeStruct((n_rows_out, d), BF16),
        compiler_params=_cparams("arbitrary"),
        name="moe_dispatch",
    )(gmap, count, f, lrow_t)


def _ffn_kernel(expert_ref, valid_ref, nused_ref, x_ref, wg_ref, wu_ref, wd_ref, y_ref, wgu_s, wd_s):
    b = pl.program_id(0)

    @pl.when(b < nused_ref[0])
    def _():
        @pl.when((b == 0) | (expert_ref[b] != expert_ref[jnp.maximum(b - 1, 0)]))
        def _():
            wgu_s[:, :EXPERT_HIDDEN] = wg_ref[...].astype(BF16)
            wgu_s[:, EXPERT_HIDDEN:] = wu_ref[...].astype(BF16)
            wd_s[...] = wd_ref[...].astype(BF16)

        rows = lax.broadcasted_iota(jnp.int32, (FFN_ROWS, 1), 0)
        x = x_ref[...]
        x = jnp.where(rows < valid_ref[b] * UNIT, x, jnp.zeros_like(x))
        hgu = jnp.dot(x, wgu_s[...], preferred_element_type=F32)
        hid = _silu(hgu[:, :EXPERT_HIDDEN]) * hgu[:, EXPERT_HIDDEN:]
        y_ref[...] = jnp.dot(hid.astype(BF16), wd_s[...], preferred_element_type=F32).astype(BF16)


def _expert_ffn(xs, w_gate, w_up, w_down, layer, expert, valid, n_used):
    n_rows, d = xs.shape
    row_blk = lambda b, e, v, n: (jnp.minimum(b, n[0] - 1), 0)
    of_expert = lambda b, e, v, n: (layer, e[b], 0, 0)
    grid_spec = pltpu.PrefetchScalarGridSpec(
        num_scalar_prefetch=3,
        grid=(n_rows // FFN_ROWS,),
        in_specs=[pl.BlockSpec((FFN_ROWS, d), row_blk),
                  pl.BlockSpec((None, None, d, EXPERT_HIDDEN), of_expert),
                  pl.BlockSpec((None, None, d, EXPERT_HIDDEN), of_expert),
                  pl.BlockSpec((None, None, EXPERT_HIDDEN, d), of_expert)],
        out_specs=pl.BlockSpec((FFN_ROWS, d), row_blk),
        scratch_shapes=[pltpu.VMEM((d, 2 * EXPERT_HIDDEN), BF16), pltpu.VMEM((EXPERT_HIDDEN, d), BF16)])
    return pl.pallas_call(
        _ffn_kernel,
        grid_spec=grid_spec,
        out_shape=jax.ShapeDtypeStruct((n_rows, d), BF16),
        compiler_params=_cparams("arbitrary"),
        name="moe_ffn",
    )(expert, valid, n_used, xs, w_gate, w_up, w_down)


def _combine_kernel(gmap_ref, count_ref, ys_hbm, lrow_ref, w_ref, f_ref, sgu_ref, sdn_ref, h1_ref, g2_ref,
                    lng_ref, lnb_ref, o_ref, buf_ref, sem_ref, acc_ref, lrow_rep_ref, w_rep_ref, *, alpha):
    s = pl.program_id(0)
    n_tiles = pl.num_programs(0)
    slot = s % 2

    def copy(slot, lu, gu):
        return pltpu.make_async_copy(_unit_rows(ys_hbm, gu), _unit_rows(buf_ref.at[slot], lu), sem_ref.at[slot])

    def fetch(tile, slot):
        _for_each_unit(gmap_ref, count_ref, tile, lambda lu, gu: copy(slot, lu, gu).start())

    @pl.when(s == 0)
    def _():
        buf_ref[...] = jnp.zeros(buf_ref.shape, BF16)
        fetch(0, 0)

    @pl.when(s + 1 < n_tiles)
    def _():
        fetch(s + 1, 1 - slot)

    n_units = count_ref[s]

    def wait_one(i, carry):
        copy(slot, 0, 0).wait()
        return carry

    lax.fori_loop(0, n_units, wait_one, 0)

    hgu = jnp.dot(f_ref[...], sgu_ref[...], preferred_element_type=F32)
    hid = _silu(hgu[:, :EXPERT_HIDDEN]) * hgu[:, EXPERT_HIDDEN:]
    acc_ref[...] = jnp.dot(hid.astype(BF16), sdn_ref[...], preferred_element_type=F32)

    lrow = lrow_ref[...]
    w = w_ref[...]
    for k in range(TOP_K):
        lrow_rep_ref[k] = jnp.broadcast_to(lrow[:, k:k + 1], (SUB_TILE, LANES))
        w_rep_ref[k] = jnp.broadcast_to(w[:, k:k + 1], (SUB_TILE, LANES))
    chunk = MOE_CHUNK

    def unsort_chunk(r0):
        pieces = []
        for c0 in range(r0, r0 + chunk, LANES):
            col = lax.broadcasted_iota(jnp.int32, (SUB_TILE, LANES), 1) + c0
            pw = jnp.zeros((SUB_TILE, LANES), F32)
            for k in range(TOP_K):
                pw = jnp.where(lrow_rep_ref[k] == col, w_rep_ref[k], pw)
            pieces.append(pw.astype(BF16))
        return jnp.dot(jnp.concatenate(pieces, axis=1), buf_ref[slot, r0:r0 + chunk, :],
                       preferred_element_type=F32)

    def add_chunk(r0):
        acc_ref[...] += unsort_chunk(r0)

    always = [r0 for r0 in range(0, LOCAL_ROWS, chunk) if r0 + chunk <= SUB_TILE * TOP_K]
    acc_ref[...] += functools.reduce(lambda a, b: a + b, [unsort_chunk(r0) for r0 in always])
    for r0 in range(len(always) * chunk, LOCAL_ROWS, chunk):
        pl.when(r0 < n_units * UNIT)(functools.partial(add_chunk, r0))

    o_ref[...] = _ln(alpha * h1_ref[...] + g2_ref[...] * acc_ref[...]) * lng_ref[...] + lnb_ref[...]


def _combine(ys, lrow, w, f, sgu, sdn, h1, g2, ln_g, ln_b, gmap, count, *, tiles_per_batch, n_batch, alpha):
    t, d = f.shape
    tm = SUB_TILE
    row = lambda s, u, g: (s, 0)
    fixed = lambda s, u, g: (0, 0)
    mod_row = lambda s, u, g: (jnp.minimum(s // tiles_per_batch, n_batch), 0, 0)
    grid_spec = pltpu.PrefetchScalarGridSpec(
        num_scalar_prefetch=2,
        grid=(t // tm,),
        in_specs=[pl.BlockSpec(memory_space=pl.ANY),
                  pl.BlockSpec((tm, LANES), row), pl.BlockSpec((tm, LANES), row), pl.BlockSpec((tm, d), row),
                  pl.BlockSpec(sgu.shape, fixed), pl.BlockSpec(sdn.shape, fixed),
                  pl.BlockSpec((tm, d), row), pl.BlockSpec((None, 1, d), mod_row),
                  pl.BlockSpec((1, d), fixed), pl.BlockSpec((1, d), fixed)],
        out_specs=pl.BlockSpec((tm, d), row),
        scratch_shapes=[pltpu.VMEM((2, LOCAL_ROWS, d), BF16), pltpu.SemaphoreType.DMA((2,)),
                        pltpu.VMEM((tm, d), F32), pltpu.VMEM((TOP_K, tm, LANES), jnp.int32),
                        pltpu.VMEM((TOP_K, tm, LANES), F32)])
    return pl.pallas_call(
        functools.partial(_combine_kernel, alpha=alpha),
        grid_spec=grid_spec,
        out_shape=jax.ShapeDtypeStruct((t, d), F32),
        compiler_params=_cparams("arbitrary"),
        name="moe_combine",
    )(gmap, count, ys, lrow, w, f, sgu, sdn, h1, g2, ln_g.reshape(1, d), ln_b.reshape(1, d))


def _forward(x, c, ctx, c_ctx, w_mod, b_mod, w_in, s5_lam_re, s5_lam_im, s5_log_step, s5_b_re, s5_b_im,
             s5_c_re, s5_c_im, s5_d, gm_w_s, gm_b_s, da_lam, da_subln_g, w_glu_val, w_glu_gate,
             w_proj_gm, w_proj_da, w_out, ln1_g, ln1_b, ln2_g, ln2_b, w_router, b_router,
             w_exp_gate, w_exp_up, w_exp_down, w_sh_gate, w_sh_up, w_sh_down):
    n_batch, seq, d = x.shape
    ctx_len = ctx.shape[1]
    depth = w_mod.shape[0]
    n_lat = n_batch * seq
    alpha = (2 * depth) ** 0.25
    tiles_per_batch = seq // ROW_TILE
    sub_per_batch = seq // SUB_TILE
    assert d == D_MODEL and seq % ROW_TILE == 0 and (n_batch * ctx_len) % ROW_TILE == 0
    assert ctx_len % SUB_TILE == 0 and n_batch == 4

    cond = jnp.concatenate([c, c_ctx[None], jnp.zeros((8 - n_batch - 1, d), F32)], axis=0)
    mod = _modulation(cond, w_mod, b_mod)
    cos_t, sin_t = _rope_tables(seq)
    h = jnp.concatenate([x.reshape(n_lat, d), ctx.reshape(n_batch * ctx_len, d)], axis=0)

    for l in range(depth):
        last = l == depth - 1
        lam_init = 0.8 - 0.6 * math.exp(-0.3 * l)
        names = ("sh1", "sc1", "g1", "sh2", "sc2", "g2")
        mods = {k: mod[l, :, i * d:(i + 1) * d].reshape(8, 1, d) for i, k in enumerate(names)}

        wi = w_in[l]
        a_w, z_w, q_w, k_w, v_w, g_w = jnp.split(wi, (256, 768, 1280, 1792, 2304), axis=1)
        w_in_l = jnp.concatenate([g_w, q_w, k_w, v_w, a_w, z_w], axis=1).astype(BF16)
        p = _inproj(h, mods["sh1"], mods["sc1"], w_in_l, tiles_per_batch, n_batch)

        lf = da_lam[l].astype(F32)
        lam = (jnp.exp(jnp.sum(lf[0] * lf[1])) - jnp.exp(jnp.sum(lf[2] * lf[3])) + lam_init).reshape(1)
        q, k0, k1, v = _rope_pack(p, cos_t, sin_t, n_batch, seq, ctx_len)
        subln = da_subln_g[l].astype(F32).reshape(1, LANES)
        attn = functools.partial(_attention, lam, q, k0, k1, v, subln, out_scale=1.0 - lam_init)
        yc = attn(n_rows=n_lat, q_row0=0, tq=512, kv_len=ctx_len + seq, tk=ctx_len)
        if not last:
            yc_ctx = attn(n_rows=n_batch * ctx_len, q_row0=n_lat, tq=ctx_len, kv_len=ctx_len, tk=ctx_len)
            yc = jnp.concatenate([yc, yc_ctx], axis=0)

        mats = _s5_matrices(s5_lam_re[l], s5_lam_im[l], s5_log_step[l], s5_b_re[l], s5_b_im[l],
                            s5_c_re[l], s5_c_im[l], s5_d[l])
        ya = _s5_mixer(p, mats, n_batch=n_batch, lat_chunks=seq // S5_CHUNK, ctx_chunks=ctx_len // S5_CHUNK)

        n_rows = n_lat if last else h.shape[0]
        router_w = jnp.concatenate([w_router[l].astype(F32), jnp.zeros((d, LANES - N_EXPERTS), F32)], axis=1)
        merge_w = {
            "glu": jnp.concatenate([w_glu_val[l], w_glu_gate[l]], axis=1).astype(BF16),
            "gm": w_proj_gm[l].astype(BF16), "da": w_proj_da[l].astype(BF16), "out": w_out[l].astype(BF16),
            "gm_ws": gm_w_s[l].astype(BF16),
            "gm_bs": jnp.repeat(gm_b_s[l].astype(F32).T, GM_WIDTH // GM_HEADS, axis=1),
            "router": router_w,
        }
        h1, f, logits = _merge(ya, p, yc, h, mods, ln1_g[l], ln1_b[l], merge_w, n_rows=n_rows,
                               tiles_per_batch=seq // MERGE_TILE, n_batch=n_batch, alpha=alpha)

        lrow, w_tok, lrow_t, units_lanes = _route(logits, b_router[l])
        n_tiles = n_rows // SUB_TILE
        units = units_lanes[:, 0].reshape(n_tiles, N_EXPERTS)
        max_units = n_rows * TOP_K // UNIT + n_tiles * N_EXPERTS + N_EXPERTS * (FFN_UNITS - 1)
        n_blocks = max_units // FFN_UNITS + 1
        gmap, tile_units, blk_expert, blk_valid, n_used = _moe_layout(units, n_blocks)
        xs = _dispatch(f, lrow_t, gmap, tile_units, n_blocks * FFN_ROWS)
        ys = _expert_ffn(xs, w_exp_gate, w_exp_up, w_exp_down, l, blk_expert, blk_valid, n_used)
        sgu = jnp.concatenate([w_sh_gate[l], w_sh_up[l]], axis=1).astype(BF16)
        h = _combine(ys, lrow, w_tok, f, sgu, w_sh_down[l].astype(BF16), h1, mods["g2"], ln2_g[l], ln2_b[l],
                     gmap, tile_units, tiles_per_batch=sub_per_batch, n_batch=n_batch, alpha=alpha)

    return h[:n_lat].reshape(n_batch, seq, d)


def kernel(x, c, ctx, c_ctx, w_mod, b_mod, w_in, s5_lam_re, s5_lam_im, s5_log_step, s5_b_re, s5_b_im, s5_c_re, s5_c_im, s5_d, gm_w_s, gm_b_s, da_lam, da_subln_g, w_glu_val, w_glu_gate, w_proj_gm, w_proj_da, w_out, ln1_g, ln1_b, ln2_g, ln2_b, w_router, b_router, w_exp_gate, w_exp_up, w_exp_down, w_sh_gate, w_sh_up, w_sh_down):
    return _forward(x, c, ctx, c_ctx, w_mod, b_mod, w_in, s5_lam_re, s5_lam_im, s5_log_step, s5_b_re, s5_b_im,
                    s5_c_re, s5_c_im, s5_d, gm_w_s, gm_b_s, da_lam, da_subln_g, w_glu_val, w_glu_gate,
                    w_proj_gm, w_proj_da, w_out, ln1_g, ln1_b, ln2_g, ln2_b, w_router, b_router,
                    w_exp_gate, w_exp_up, w_exp_down, w_sh_gate, w_sh_up, w_sh_down)
```

```python
import functools
import math

import jax
import jax.numpy as jnp
from jax import lax
from jax.experimental import pallas as pl
from jax.experimental.pallas import tpu as pltpu

F32 = jnp.float32
BF16 = jnp.bfloat16
HIGHEST = lax.Precision.HIGHEST

D_MODEL = 1024
GRID_W = 64
S5_WIDTH = 256
S5_GROUP = 16
S5_GROUPS = S5_WIDTH // S5_GROUP
S5_STATE = 64
S5_CHUNK = 16
GM_WIDTH = 256
GM_HEADS = 4
GM_CHUNK = 128
DA_HEADS = 4
DA_HEAD_DIM = 64
DA_QK = DA_HEADS * 2 * DA_HEAD_DIM
DA_V = DA_HEADS * 2 * DA_HEAD_DIM
ROPE_BASE = 10000.0
N_BRANCH = 3
N_EXPERTS = 64
TOP_K = 8
N_GROUPS = 8
TOPK_GROUPS = 4
EXPERT_HIDDEN = 256
ROUTED_SCALE = 2.5
LN_EPS = 1e-5

COL_GATE = 0
COL_Q = N_BRANCH * D_MODEL
COL_K = COL_Q + DA_QK
COL_V = COL_K + DA_QK
COL_S5 = COL_V + DA_V
COL_ZU = COL_S5 + S5_WIDTH
COL_ZV = COL_ZU + GM_WIDTH
IN_WIDTH = COL_ZV + GM_WIDTH

LANES = 128
ROW_TILE = 1024
SUB_TILE = 256
MERGE_TILE = 512
VMEM_LIMIT = 48 * 1024 * 1024
UNIT = 16
LOCAL_ROWS = SUB_TILE * TOP_K + N_EXPERTS * UNIT
LOCAL_UNITS = LOCAL_ROWS // UNIT
FFN_ROWS = 512
FFN_UNITS = FFN_ROWS // UNIT
ATTN_ROW_BLOCK = 64
ONEHOT_BLOCK = 128
MOE_CHUNK = 512


def _cparams(*sem):
    return pltpu.CompilerParams(dimension_semantics=sem, vmem_limit_bytes=VMEM_LIMIT)


def _ln(x):
    mu = jnp.mean(x, -1, keepdims=True)
    xc = x - mu
    var = jnp.mean(xc * xc, -1, keepdims=True)
    return xc * lax.rsqrt(var + LN_EPS)


def _gelu(x):
    return 0.5 * x * (1.0 + jnp.tanh(math.sqrt(2.0 / math.pi) * (x + 0.044715 * (x * x * x))))


def _sigmoid(x):
    return 1.0 / (1.0 + jnp.exp(-x))


def _silu(x):
    return x * _sigmoid(x)


def _mod_kernel(c_ref, w_ref, b_ref, o_ref):
    s = _silu(c_ref[...])
    o_ref[...] = jnp.dot(s, w_ref[...], preferred_element_type=F32, precision=HIGHEST) + b_ref[...]


def _modulation(cond, w_mod, b_mod):
    depth, d, n = w_mod.shape
    tn = 1536
    return pl.pallas_call(
        _mod_kernel,
        grid=(depth, n // tn),
        in_specs=[pl.BlockSpec((8, d), lambda l, j: (0, 0)),
                  pl.BlockSpec((None, d, tn), lambda l, j: (l, 0, j)),
                  pl.BlockSpec((None, 1, tn), lambda l, j: (l, 0, j))],
        out_specs=pl.BlockSpec((None, 8, tn), lambda l, j: (l, 0, j)),
        out_shape=jax.ShapeDtypeStruct((depth, 8, n), F32),
        compiler_params=_cparams("arbitrary", "arbitrary"),
        name="modulation",
    )(cond, w_mod, b_mod.reshape(depth, 1, n))


def _inproj_kernel(h_ref, sh_ref, sc_ref, w_ref, o_ref, xn_ref):
    @pl.when(pl.program_id(1) == 0)
    def _():
        x = _ln(h_ref[...])
        xn_ref[...] = (x * (1.0 + sc_ref[...]) + sh_ref[...]).astype(BF16)

    o_ref[...] = jnp.dot(xn_ref[...], w_ref[...], preferred_element_type=F32).astype(BF16)


def _inproj(h, shift, scale, w, tiles_per_batch, n_batch):
    t, d = h.shape
    n = w.shape[1]
    tn = n // 2
    mod_row = lambda i, j: (jnp.minimum(i // tiles_per_batch, n_batch), 0, 0)
    return pl.pallas_call(
        _inproj_kernel,
        grid=(t // ROW_TILE, n // tn),
        in_specs=[pl.BlockSpec((ROW_TILE, d), lambda i, j: (i, 0)),
                  pl.BlockSpec((None, 1, d), mod_row),
                  pl.BlockSpec((None, 1, d), mod_row),
                  pl.BlockSpec((d, tn), lambda i, j: (0, j))],
        out_specs=pl.BlockSpec((ROW_TILE, tn), lambda i, j: (i, j)),
        out_shape=jax.ShapeDtypeStruct((t, n), BF16),
        scratch_shapes=[pltpu.VMEM((ROW_TILE, d), BF16)],
        compiler_params=_cparams("arbitrary", "arbitrary"),
        name="inproj",
    )(h, shift, scale, w)


def _rope_kernel(q_ref, k_ref, v_ref, cos_ref, sin_ref, qo_ref, k0_ref, k1_ref, vo_ref, *, n_lat_tiles):
    is_lat = pl.program_id(0) < n_lat_tiles
    shape = q_ref.shape
    lane = lax.broadcasted_iota(jnp.int32, shape, 1)
    upper16 = (lane % 32) >= 16
    map1 = (lane % LANES) >= DA_HEAD_DIM
    cos = jnp.where(is_lat, cos_ref[...], 1.0)
    sin = jnp.where(is_lat, sin_ref[...], 0.0)

    def rope(x):
        partner = jnp.where(upper16, pltpu.roll(x, 16, 1), pltpu.roll(x, shape[1] - 16, 1))
        return x * cos + partner * sin

    q = rope(q_ref[...].astype(F32)) * (DA_HEAD_DIM ** -0.5)
    k = rope(k_ref[...].astype(F32))
    qo_ref[...] = q.astype(BF16)
    k0_ref[...] = jnp.where(map1, 0.0, k).astype(BF16)
    k1_ref[...] = jnp.where(map1, k, 0.0).astype(BF16)
    ones = jnp.ones((shape[0], LANES), BF16)
    v = v_ref[...]
    vo_ref[...] = jnp.concatenate(
        [piece for hd in range(DA_HEADS) for piece in (v[:, hd * LANES:(hd + 1) * LANES], ones)], axis=1)


def _rope_pack(p, cos_t, sin_t, n_batch, seq, ctx):
    t = p.shape[0]
    nl = seq // SUB_TILE
    nc = ctx // SUB_TILE
    n_lat_tiles = n_batch * nl

    def kv_map(i):
        j = i - n_lat_tiles
        b = jnp.where(i < n_lat_tiles, i // nl, j // nc)
        blk = jnp.where(i < n_lat_tiles, nc + i % nl, j % nc)
        return (b, blk, 0)

    tab_map = lambda i: (jnp.where(i < n_lat_tiles, i % nl, 0), 0)
    col = lambda c: (lambda i: (i, c // DA_QK))
    kv_shape = jax.ShapeDtypeStruct((n_batch, ctx + seq, DA_QK), BF16)
    return pl.pallas_call(
        functools.partial(_rope_kernel, n_lat_tiles=n_lat_tiles),
        grid=(t // SUB_TILE,),
        in_specs=[pl.BlockSpec((SUB_TILE, DA_QK), col(COL_Q)),
                  pl.BlockSpec((SUB_TILE, DA_QK), col(COL_K)),
                  pl.BlockSpec((SUB_TILE, DA_V), col(COL_V)),
                  pl.BlockSpec((SUB_TILE, DA_QK), tab_map),
                  pl.BlockSpec((SUB_TILE, DA_QK), tab_map)],
        out_specs=[pl.BlockSpec((SUB_TILE, DA_QK), lambda i: (i, 0)),
                   pl.BlockSpec((None, SUB_TILE, DA_QK), kv_map),
                   pl.BlockSpec((None, SUB_TILE, DA_QK), kv_map),
                   pl.BlockSpec((None, SUB_TILE, 2 * DA_V), kv_map)],
        out_shape=[jax.ShapeDtypeStruct((t, DA_QK), BF16), kv_shape, kv_shape,
                   jax.ShapeDtypeStruct((n_batch, ctx + seq, 2 * DA_V), BF16)],
        compiler_params=_cparams("arbitrary"),
        name="rope_pack",
    )(p, p, p, cos_t, sin_t)


def _rope_tables(seq):
    pos = jnp.arange(seq)
    row = (pos // GRID_W).astype(F32)[:, None]
    colp = (pos % GRID_W).astype(F32)[:, None]
    axis_dim = DA_HEAD_DIM // 2
    inv_freq = ROPE_BASE ** (-jnp.arange(0, axis_dim, 2, dtype=F32) / axis_dim)
    ang_r = row * inv_freq
    ang_c = colp * inv_freq
    ang = jnp.concatenate([ang_r, ang_r, ang_c, ang_c], -1)
    sign = jnp.concatenate([-jnp.ones((16,), F32), jnp.ones((16,), F32)] * 2)
    reps = DA_QK // DA_HEAD_DIM
    return jnp.tile(jnp.cos(ang), (1, reps)), jnp.tile(jnp.sin(ang) * sign, (1, reps))


def _attn_kernel(lam_ref, q_ref, k0_ref, k1_ref, v_ref, g_ref, o_ref, *scratch, n_chunks, tk, out_scale):
    q = q_ref[...]
    nt = (((1,), (1,)), ((), ()))
    k_refs = (k0_ref, k1_ref)
    s_refs = (scratch[0:2], scratch[2:4])
    m_refs, acc_refs, p_refs = scratch[4:6], scratch[6:8], scratch[8:10]

    opaque_zero = jnp.minimum(pl.program_id(2), 0)

    def chunk(j):
        return pl.ds(pl.multiple_of(j * tk, tk), tk)

    def scores(j, buf):
        for mp in range(2):
            s_refs[buf][mp][...] = lax.dot_general(q, k_refs[mp][chunk(j), :], nt, preferred_element_type=F32)

    def absorb(j, buf):
        vc = v_ref[chunk(j), :]
        for mp in range(2):
            for r0 in range(0, q.shape[0], ATTN_ROW_BLOCK):
                rows = slice(r0, r0 + ATTN_ROW_BLOCK)
                s = s_refs[buf][mp][pl.ds(pl.multiple_of(r0 + opaque_zero, ATTN_ROW_BLOCK), ATTN_ROW_BLOCK), :]
                m = m_refs[mp][rows, :]
                m_new = jnp.maximum(m, jnp.broadcast_to(jnp.max(s, -1, keepdims=True), m.shape))
                alpha = jnp.exp(m - m_new)
                p = jnp.exp(s - jnp.concatenate([m_new] * (tk // LANES), axis=1))
                acc_refs[mp][rows, :] = jnp.concatenate([alpha, alpha], axis=1) * acc_refs[mp][rows, :]
                m_refs[mp][rows, :] = m_new
                p_refs[mp][rows, :] = p.astype(BF16)
            acc_refs[mp][...] += jnp.dot(p_refs[mp][...], vc, preferred_element_type=F32)

    for mp in range(2):
        m_refs[mp][...] = jnp.full(m_refs[mp].shape, -jnp.inf, F32)
        acc_refs[mp][...] = jnp.zeros(acc_refs[mp].shape, F32)
    scores(0, 0)

    def body(i, carry):
        scores(2 * i + 1, 1)
        absorb(2 * i, 0)
        scores(2 * i + 2, 0)
        absorb(2 * i + 1, 1)
        return carry

    lax.fori_loop(0, (n_chunks - 1) // 2, body, 0)
    absorb(n_chunks - 1, 0)
    a0, a1 = acc_refs[0][...], acc_refs[1][...]
    o = a0[:, :LANES] / a0[:, LANES:] - lam_ref[0] * (a1[:, :LANES] / a1[:, LANES:])
    o = o * lax.rsqrt(jnp.mean(o * o, -1, keepdims=True) + LN_EPS)
    o_ref[...] = (o * g_ref[...] * out_scale).astype(BF16)


def _attention(lam, q, k0, k1, v, subln_g, *, n_rows, q_row0, tq, kv_len, tk, out_scale):
    n_batch = k0.shape[0]
    per_batch = n_rows // n_batch // tq
    q0 = q_row0 // tq
    kv_spec = pl.BlockSpec((None, kv_len, LANES), lambda b, h, i: (b, 0, h))
    return pl.pallas_call(
        functools.partial(_attn_kernel, n_chunks=kv_len // tk, tk=tk, out_scale=out_scale),
        grid=(n_batch, DA_HEADS, per_batch),
        scratch_shapes=([pltpu.VMEM((tq, tk), F32)] * 4 + [pltpu.VMEM((tq, LANES), F32)] * 2
                        + [pltpu.VMEM((tq, 2 * LANES), F32)] * 2 + [pltpu.VMEM((tq, tk), BF16)] * 2),
        in_specs=[pl.BlockSpec(memory_space=pltpu.SMEM),
                  pl.BlockSpec((tq, LANES), lambda b, h, i: (q0 + b * per_batch + i, h)),
                  kv_spec, kv_spec,
                  pl.BlockSpec((None, kv_len, 2 * LANES), lambda b, h, i: (b, 0, h)),
                  pl.BlockSpec((1, LANES), lambda b, h, i: (0, 0))],
        out_specs=pl.BlockSpec((tq, LANES), lambda b, h, i: (b * per_batch + i, h)),
        out_shape=jax.ShapeDtypeStruct((n_rows, DA_V), BF16),
        compiler_params=_cparams("arbitrary", "arbitrary", "arbitrary"),
        name="diff_attention",
    )(lam, q, k0, k1, v, subln_g)


def _s5_in_kernel(x_ref, b_ref, u_ref, z_ref, stage_ref):
    @pl.when(pl.program_id(1) == 0)
    def _():
        x = x_ref[...].astype(F32)
        for half in range(S5_WIDTH // LANES):
            stage_ref[half] = x[:, half * LANES:(half + 1) * LANES]
        for s in range(S5_CHUNK):
            for half in range(S5_WIDTH // LANES):
                col = s * S5_WIDTH + half * LANES
                u_ref[:, col:col + LANES] = (
                    stage_ref[half, pl.ds(s, u_ref.shape[0], stride=S5_CHUNK), :].astype(BF16))

    z_ref[...] = jnp.dot(u_ref[...], b_ref[...], preferred_element_type=F32)


def _s5_scan_kernel(z_ref, a1_ref, a2_ref, p_ref, *, n_batch, lat_chunks, ctx_chunks):
    reverse = pl.program_id(0) == 1
    a1 = a1_ref[...]
    a2 = a2_ref[...]
    width = a1.shape[1]

    def swap_halves(s):
        return jnp.concatenate(
            [pltpu.roll(s[:, j * LANES:(j + 1) * LANES], LANES // 2, 1) for j in range(width // LANES)], axis=1)

    def run(base, count, carry):
        def body(i, st):
            s, ssw = st
            row = base + jnp.where(reverse, count - 1 - i, i)
            z = z_ref[pl.ds(row, 1), :]
            p_ref[pl.ds(row, 1), :] = s
            return a1 * s + a2 * ssw + z, a1 * ssw - a2 * s + swap_halves(z)
        return lax.fori_loop(0, count, body, carry, unroll=4)

    zero = jnp.zeros(a1.shape, F32)
    for b in range(n_batch):
        st = run(n_batch * lat_chunks + b * ctx_chunks, ctx_chunks, (zero, zero))
        run(b * lat_chunks, lat_chunks, st)


def _s5_out_kernel(u_ref, p_ref, t_ref, ct_ref, y_ref, stage_ref):
    y = jnp.dot(u_ref[...], t_ref[...], preferred_element_type=F32)
    y = y + lax.dot_general(p_ref[...].astype(BF16), ct_ref[...], (((1,), (1,)), ((), ())),
                            preferred_element_type=F32)
    y = _gelu(y)
    per_step = y.shape[1] // S5_WIDTH
    first = pl.program_id(1) * per_step
    halves = S5_WIDTH // LANES
    for k in range(per_step):
        for half in range(halves):
            col = k * S5_WIDTH + half * LANES
            stage_ref[half, pl.ds(first + k, y.shape[0], stride=S5_CHUNK), :] = y[:, col:col + LANES]

    @pl.when(pl.program_id(1) == pl.num_programs(1) - 1)
    def _():
        for half in range(halves):
            y_ref[:, half * LANES:(half + 1) * LANES] = stage_ref[half]


def _s5_mixer(p, mats, *, n_batch, lat_chunks, ctx_chunks):
    t_all = p.shape[0]
    r = t_all // S5_CHUNK
    w = S5_CHUNK * S5_WIDTH
    tr = r // 4
    state_w = mats["b"].shape[1]
    tn = 1024
    u, z = pl.pallas_call(
        _s5_in_kernel,
        grid=(r // tr, state_w // tn),
        in_specs=[pl.BlockSpec((tr * S5_CHUNK, S5_WIDTH), lambda i, j: (i, COL_S5 // S5_WIDTH)),
                  pl.BlockSpec((w, tn), lambda i, j: (0, j))],
        out_specs=[pl.BlockSpec((tr, w), lambda i, j: (i, 0)),
                   pl.BlockSpec((tr, tn), lambda i, j: (i, j))],
        out_shape=[jax.ShapeDtypeStruct((r, w), BF16), jax.ShapeDtypeStruct((r, state_w), F32)],
        scratch_shapes=[pltpu.VMEM((S5_WIDTH // LANES, tr * S5_CHUNK, LANES), F32)],
        compiler_params=_cparams("arbitrary", "arbitrary"),
        name="s5_in",
    )(p, mats["b"])
    half = state_w // 2
    coef = pl.BlockSpec((None, 1, half), lambda d: (d, 0, 0))
    prev = pl.pallas_call(
        functools.partial(_s5_scan_kernel, n_batch=n_batch, lat_chunks=lat_chunks, ctx_chunks=ctx_chunks),
        grid=(2,),
        in_specs=[pl.BlockSpec((r, half), lambda d: (0, d)), coef, coef],
        out_specs=pl.BlockSpec((r, half), lambda d: (0, d)),
        out_shape=jax.ShapeDtypeStruct((r, state_w), F32),
        compiler_params=_cparams("arbitrary"),
        name="s5_scan",
    )(z, mats["a1"], mats["a2"])
    tn = 512
    return pl.pallas_call(
        _s5_out_kernel,
        grid=(r // tr, w // tn),
        in_specs=[pl.BlockSpec((tr, w), lambda i, j: (i, 0)),
                  pl.BlockSpec((tr, state_w), lambda i, j: (i, 0)),
                  pl.BlockSpec((w, tn), lambda i, j: (0, j)),
                  pl.BlockSpec((tn, state_w), lambda i, j: (j, 0))],
        out_specs=pl.BlockSpec((tr * S5_CHUNK, S5_WIDTH), lambda i, j: (i, 0)),
        out_shape=jax.ShapeDtypeStruct((t_all, S5_WIDTH), F32),
        scratch_shapes=[pltpu.VMEM((S5_WIDTH // LANES, tr * S5_CHUNK, LANES), F32)],
        compiler_params=_cparams("arbitrary", "arbitrary"),
        name="s5_out",
    )(u, prev, mats["t"], mats["ct"])


def _s5_matrices(lam_re, lam_im, log_step, b_re, b_im, c_re, c_im, d_skip):
    n = S5_CHUNK
    dt = jnp.exp(log_step.astype(F32))[..., None]
    lr, li = lam_re.astype(F32), lam_im.astype(F32)
    mag = jnp.exp(lr * dt)
    a_re, a_im = mag * jnp.cos(li * dt), mag * jnp.sin(li * dt)
    den = lr * lr + li * li
    n_re = a_re - 1.0
    z_re = (n_re * lr + a_im * li) / den
    z_im = (a_im * lr - n_re * li) / den
    br, bi = b_re.astype(F32), b_im.astype(F32)
    bb_re = z_re[..., None] * br - z_im[..., None] * bi
    bb_im = z_re[..., None] * bi + z_im[..., None] * br
    j = jnp.arange(n + 1, dtype=F32)[:, None, None, None]
    pmag = jnp.exp(lr * dt * j)
    pw_re, pw_im = pmag * jnp.cos(li * dt * j), pmag * jnp.sin(li * dt * j)
    cr, ci = c_re.astype(F32), c_im.astype(F32)
    g, p, c = S5_GROUPS, S5_STATE, S5_GROUP

    def cmul(xr, xi, yr, yi):
        return xr * yr - xi * yi, xr * yi + xi * yr

    def in_mat(direction, powers):
        er = pw_re[powers, direction][:, :, :, None]
        ei = pw_im[powers, direction][:, :, :, None]
        xr, xi = cmul(er, ei, bb_re[direction][None], bb_im[direction][None])
        m = jnp.concatenate([xr, xi], axis=2)
        return m.transpose(1, 0, 3, 2).reshape(g, n * c, 2 * p)

    def out_mat(direction, powers):
        er = pw_re[powers, direction][:, :, None, :]
        ei = pw_im[powers, direction][:, :, None, :]
        wr, wi = cmul(cr[direction][None], ci[direction][None], er, ei)
        m = jnp.concatenate([wr, -wi], axis=3)
        return m.transpose(1, 3, 0, 2).reshape(g, 2 * p, n * c)

    def toeplitz(direction):
        er = pw_re[:n, direction][:, :, None, :, None]
        ei = pw_im[:n, direction][:, :, None, :, None]
        wr, wi = cmul(cr[direction][None, :, :, :, None], ci[direction][None, :, :, :, None], er, ei)
        k = jnp.sum(wr * bb_re[direction][None, :, None] - wi * bb_im[direction][None, :, None], axis=3)
        return k

    up, down = slice(0, n), slice(n - 1, None, -1)
    up1, down1 = slice(1, n + 1), slice(n, 0, -1)
    kf, kb = toeplitz(0), toeplitz(1)
    skip = d_skip.astype(F32).reshape(g, c)
    k0 = kf[0] + kb[0] + jnp.eye(c, dtype=F32)[None] * skip[:, :, None]
    by_lag = jnp.concatenate([kb[:0:-1], k0[None], kf[1:]], axis=0)
    eye_g = jnp.eye(g, dtype=F32)
    lag_blocks = jnp.einsum("mgca,gh->mgahc", by_lag, eye_g).reshape(2 * n - 1, g * c, g * c)
    steps = jnp.arange(n)
    t_dense = lag_blocks[steps[None, :] - steps[:, None] + n - 1]
    t_dense = t_dense.transpose(0, 2, 1, 3).reshape(n * g * c, n * g * c)

    def coef(direction):
        ar, ai = pw_re[n, direction], pw_im[n, direction]
        return (jnp.concatenate([ar, ar], -1).reshape(-1), jnp.concatenate([-ai, ai], -1).reshape(-1))

    def dense_in(m):
        m = m.reshape(g, n, c, -1)
        return jnp.einsum("gscq,gh->sgchq", m, eye_g).reshape(n * g * c, -1)

    def dense_out_t(m):
        q = m.shape[1]
        return jnp.einsum("gqtc,gh->thcgq", m.reshape(g, q, n, c), eye_g).reshape(n * g * c, g * q)

    a1f, a2f = coef(0)
    a1b, a2b = coef(1)
    return {
        "b": jnp.concatenate([dense_in(in_mat(0, down)), dense_in(in_mat(1, up))], 1).astype(BF16),
        "ct": jnp.concatenate([dense_out_t(out_mat(0, up1)), dense_out_t(out_mat(1, down1))], 1).astype(BF16),
        "t": t_dense.astype(BF16),
        "a1": jnp.stack([a1f, a1b])[:, None, :], "a2": jnp.stack([a2f, a2b])[:, None, :],
    }


def _merge_kernel(ya_ref, zu_ref, zv_ref, yc_ref, ga_ref, gb_ref, gc_ref, h_ref,
                  g1_ref, sh2_ref, sc2_ref, lng_ref, lnb_ref,
                  wglu_ref, wgm_ref, wda_ref, wout_ref, ws_ref, bs_ref, wr_ref,
                  h1_ref, f_ref, lg_ref, *, alpha):
    f32 = lambda ref: ref[...].astype(F32)
    glu = jnp.dot(ya_ref[...].astype(BF16), wglu_ref[...], preferred_element_type=F32)
    branch_a = glu[:, :D_MODEL] * _sigmoid(glu[:, D_MODEL:])

    u = _gelu(f32(zu_ref))
    v = _ln(_gelu(f32(zv_ref))).astype(BF16)
    head = lax.broadcasted_iota(jnp.int32, (GM_CHUNK, GM_WIDTH), 1) // (GM_WIDTH // GM_HEADS)
    parts = []
    for ck in range(v.shape[0] // GM_CHUNK):
        vc = v[ck * GM_CHUNK:(ck + 1) * GM_CHUNK]
        s = bs_ref[...]
        for hd in range(GM_HEADS):
            s = s + jnp.dot(ws_ref[hd], jnp.where(head == hd, vc, jnp.zeros_like(vc)),
                            preferred_element_type=F32)
        parts.append(s)
    yb = (u * jnp.concatenate(parts, axis=0)).astype(BF16)

    m = _sigmoid(f32(ga_ref)) * branch_a
    m = m + _sigmoid(f32(gb_ref)) * jnp.dot(yb, wgm_ref[...], preferred_element_type=F32)
    m = m + _sigmoid(f32(gc_ref)) * jnp.dot(yc_ref[...], wda_ref[...], preferred_element_type=F32)
    mix = jnp.dot(m.astype(BF16), wout_ref[...], preferred_element_type=F32)

    h1 = _ln(alpha * h_ref[...] + g1_ref[...] * mix) * lng_ref[...] + lnb_ref[...]
    h1_ref[...] = h1
    f = _ln(h1) * (1.0 + sc2_ref[...]) + sh2_ref[...]
    f_ref[...] = f.astype(BF16)
    lg_ref[...] = jnp.dot(f, wr_ref[...], preferred_element_type=F32, precision=HIGHEST)


def _merge(ya, p, yc, h, mods, ln_g, ln_b, w, *, n_rows, tiles_per_batch, n_batch, alpha):
    d = D_MODEL
    tm = MERGE_TILE
    row = lambda i: (i, 0)
    mod_row = lambda i: (jnp.minimum(i // tiles_per_batch, n_batch), 0, 0)
    pcol = lambda c, width: pl.BlockSpec((tm, width), lambda i: (i, c // width))
    full = lambda a: pl.BlockSpec(a.shape, lambda i: (0,) * a.ndim)
    mod_spec = pl.BlockSpec((None, 1, d), mod_row)
    vec = lambda a: a.reshape(1, d)
    weights = (w["glu"], w["gm"], w["da"], w["out"], w["gm_ws"], w["gm_bs"], w["router"])
    return pl.pallas_call(
        functools.partial(_merge_kernel, alpha=alpha),
        grid=(n_rows // tm,),
        in_specs=[pl.BlockSpec((tm, S5_WIDTH), row),
                  pcol(COL_ZU, GM_WIDTH), pcol(COL_ZV, GM_WIDTH),
                  pl.BlockSpec((tm, DA_V), row),
                  pcol(COL_GATE, d), pcol(COL_GATE + d, d), pcol(COL_GATE + 2 * d, d),
                  pl.BlockSpec((tm, d), row),
                  mod_spec, mod_spec, mod_spec,
                  pl.BlockSpec((1, d), lambda i: (0, 0)), pl.BlockSpec((1, d), lambda i: (0, 0))]
                 + [full(a) for a in weights],
        out_specs=[pl.BlockSpec((tm, d), row), pl.BlockSpec((tm, d), row), pl.BlockSpec((tm, LANES), row)],
        out_shape=[jax.ShapeDtypeStruct((n_rows, d), F32), jax.ShapeDtypeStruct((n_rows, d), BF16),
                   jax.ShapeDtypeStruct((n_rows, LANES), F32)],
        compiler_params=_cparams("arbitrary"),
        name="merge",
    )(ya, p, p, yc, p, p, p, h, mods["g1"], mods["sh2"], mods["sc2"], vec(ln_g), vec(ln_b), *weights)


def _router_kernel(lg_ref, b_ref, before_ref, lrow_ref, w_ref, lrow_t_ref, units_ref):
    tm = lg_ref.shape[0]
    per_group = N_EXPERTS // N_GROUPS
    neg = -jnp.inf
    logits = lg_ref[...].T[:N_EXPERTS]
    scores = _sigmoid(logits).reshape(N_GROUPS, per_group, tm)
    sel = scores + b_ref[...].reshape(N_GROUPS, per_group, 1)

    in_group = lax.broadcasted_iota(jnp.int32, sel.shape, 1)
    top1 = jnp.max(sel, axis=1, keepdims=True)
    first = jnp.min(jnp.where(sel == top1, in_group, per_group), axis=1, keepdims=True)
    top2 = jnp.max(jnp.where(in_group == first, neg, sel), axis=1, keepdims=True)
    gscore = top1 + top2

    gidx = lax.broadcasted_iota(jnp.int32, gscore.shape, 0)
    gsel = jnp.zeros(gscore.shape, jnp.bool_)
    for _ in range(TOPK_GROUPS):
        best = jnp.max(gscore, axis=0, keepdims=True)
        hit = gidx == jnp.min(jnp.where(gscore == best, gidx, N_GROUPS), axis=0, keepdims=True)
        gsel = gsel | hit
        gscore = jnp.where(hit, neg, gscore)

    eidx = lax.broadcasted_iota(jnp.int32, sel.shape, 0) * per_group + in_group
    cand = jnp.where(gsel, sel, neg)
    chosen = jnp.zeros(sel.shape, jnp.bool_)
    hits = []
    for _ in range(TOP_K):
        best = jnp.max(jnp.max(cand, axis=1, keepdims=True), axis=0, keepdims=True)
        at = jnp.where(cand == best, eidx, N_EXPERTS)
        hit = eidx == jnp.min(jnp.min(at, axis=1, keepdims=True), axis=0, keepdims=True)
        hits.append(hit)
        chosen = chosen | hit
        cand = jnp.where(hit, neg, cand)

    w = jnp.where(chosen, scores, 0.0)
    total = jnp.sum(jnp.sum(w, axis=1, keepdims=True), axis=0, keepdims=True)
    w = w / total * ROUTED_SCALE

    onehot = jnp.where(chosen, 1.0, 0.0).reshape(N_EXPERTS, tm).astype(BF16)
    rank = jnp.dot(onehot, before_ref[...], preferred_element_type=F32)
    count = jnp.dot(onehot, jnp.ones((tm, LANES), BF16), preferred_element_type=F32)
    units = jnp.floor((count + (UNIT - 1)) * (1.0 / UNIT))
    ei = lax.broadcasted_iota(jnp.int32, (N_EXPERTS, N_EXPERTS), 0)
    ej = lax.broadcasted_iota(jnp.int32, (N_EXPERTS, N_EXPERTS), 1)
    first_unit = jnp.dot(jnp.where(ej < ei, 1.0, 0.0).astype(BF16), units.astype(BF16),
                         preferred_element_type=F32)
    base = jnp.concatenate([first_unit * UNIT] * (tm // LANES), axis=1)
    pos = (base + rank).reshape(sel.shape)

    pick = lambda hit, val: jnp.sum(jnp.sum(jnp.where(hit, val, 0.0), axis=1, keepdims=True), axis=0)
    pad = jnp.zeros((LANES - TOP_K, tm), F32)
    lrow_t = jnp.concatenate([pick(hit, pos) for hit in hits] + [pad], axis=0)
    w_t = jnp.concatenate([pick(hit, w) for hit in hits] + [pad], axis=0)
    lrow_t_ref[...] = lrow_t[:TOP_K].astype(jnp.int32)
    lrow_ref[...] = lrow_t.T.astype(jnp.int32)
    w_ref[...] = w_t.T
    units_ref[...] = units.astype(jnp.int32)


def _route(logits, b_router):
    t = logits.shape[0]
    tm = SUB_TILE
    n_sub = t // tm
    before = jnp.triu(jnp.ones((tm, tm), F32), 1).astype(BF16)
    tok = pl.BlockSpec((tm, LANES), lambda i: (i, 0))
    pick = pl.BlockSpec((TOP_K, tm), lambda i: (0, i))
    return pl.pallas_call(
        _router_kernel,
        grid=(n_sub,),
        in_specs=[tok, pl.BlockSpec((N_EXPERTS, 1), lambda i: (0, 0)), pl.BlockSpec((tm, tm), lambda i: (0, 0))],
        out_specs=[tok, tok, pick, pl.BlockSpec((N_EXPERTS, LANES), lambda i: (i, 0))],
        out_shape=[jax.ShapeDtypeStruct((t, LANES), jnp.int32), jax.ShapeDtypeStruct((t, LANES), F32),
                   jax.ShapeDtypeStruct((TOP_K, t), jnp.int32),
                   jax.ShapeDtypeStruct((n_sub * N_EXPERTS, LANES), jnp.int32)],
        compiler_params=_cparams("arbitrary"),
        name="router",
    )(logits, b_router.astype(F32).reshape(N_EXPERTS, 1), before)


def _moe_layout(units, n_blocks):
    per_expert = jnp.sum(units, axis=0)
    padded = (per_expert + FFN_UNITS - 1) // FFN_UNITS * FFN_UNITS
    ends = jnp.cumsum(padded)
    starts = ends - padded
    goff = starts[None, :] + jnp.cumsum(units, axis=0) - units
    local_end = jnp.cumsum(units, axis=1)
    lu = jnp.arange(LOCAL_UNITS, dtype=jnp.int32)
    owner = jnp.minimum(jnp.sum(local_end[:, None, :] <= lu[None, :, None], axis=2), N_EXPERTS - 1)
    is_owner = owner[:, :, None] == jnp.arange(N_EXPERTS, dtype=jnp.int32)[None, None, :]
    shift = goff - (local_end - units)
    gmap = lu[None, :] + jnp.sum(jnp.where(is_owner, shift[:, None, :], 0), axis=2)
    n_used = ends[-1] // FFN_UNITS
    blk = jnp.arange(n_blocks, dtype=jnp.int32)
    blk = jnp.minimum(blk, n_used - 1)
    first = blk * FFN_UNITS
    expert = jnp.minimum(jnp.sum(ends[None, :] <= first[:, None], axis=1), N_EXPERTS - 1)
    of_block = expert[:, None] == jnp.arange(N_EXPERTS, dtype=jnp.int32)[None, :]
    range_end = jnp.sum(jnp.where(of_block, (starts + per_expert)[None, :], 0), axis=1)
    valid = jnp.clip(range_end - first, 0, FFN_UNITS)
    i32 = lambda a: a.astype(jnp.int32)
    return i32(gmap.reshape(-1)), i32(local_end[:, -1]), i32(expert), i32(valid), i32(n_used.reshape(1))


def _unit_rows(ref, unit):
    return ref.at[pl.ds(pl.multiple_of(unit * UNIT, UNIT), UNIT), :]


def _for_each_unit(gmap_ref, count_ref, tile, fn):
    def body(lu, carry):
        fn(lu, gmap_ref[tile * LOCAL_UNITS + lu])
        return carry

    lax.fori_loop(0, count_ref[tile], body, 0)


def _dispatch_kernel(gmap_ref, count_ref, f_ref, lrow_t_ref, xs_hbm, buf_ref, sem_ref, *onehot_refs):
    s = pl.program_id(0)
    last = pl.num_programs(0) - 1
    slot = s % 2

    def copy(slot, lu, gu):
        return pltpu.make_async_copy(_unit_rows(buf_ref.at[slot], lu), _unit_rows(xs_hbm, gu), sem_ref.at[slot])

    def drain(step):
        def body(i, carry):
            copy(step % 2, 0, 0).wait()
            return carry
        lax.fori_loop(0, count_ref[step], body, 0)

    @pl.when(s >= 2)
    def _():
        drain(s - 2)

    lt = lrow_t_ref[...]
    f = f_ref[...]
    chunk = MOE_CHUNK

    def sort_chunk(r0, onehot_ref):
        for p0 in range(0, chunk, ONEHOT_BLOCK):
            row = lax.broadcasted_iota(jnp.int32, (ONEHOT_BLOCK, SUB_TILE), 0) + (r0 + p0)
            hit = row == lt[0:1, :]
            for k in range(1, TOP_K):
                hit = hit | (row == lt[k:k + 1, :])
            onehot_ref[p0:p0 + ONEHOT_BLOCK, :] = jnp.where(hit, 1.0, 0.0).astype(BF16)
        buf_ref[slot, r0:r0 + chunk, :] = jnp.dot(onehot_ref[...], f, preferred_element_type=F32).astype(BF16)

    for i, r0 in enumerate(range(0, LOCAL_ROWS, chunk)):
        if r0 + chunk <= SUB_TILE * TOP_K:
            sort_chunk(r0, onehot_refs[i])
        else:
            pl.when(r0 < count_ref[s] * UNIT)(functools.partial(sort_chunk, r0, onehot_refs[i]))

    _for_each_unit(gmap_ref, count_ref, s, lambda lu, gu: copy(slot, lu, gu).start())

    @pl.when(s == last)
    def _():
        drain(s)

        @pl.when(s >= 1)
        def _():
            drain(s - 1)


def _dispatch(f, lrow_t, gmap, count, n_rows_out):
    t, d = f.shape
    grid_spec = pltpu.PrefetchScalarGridSpec(
        num_scalar_prefetch=2,
        grid=(t // SUB_TILE,),
        in_specs=[pl.BlockSpec((SUB_TILE, d), lambda s, u, g: (s, 0)),
                  pl.BlockSpec((TOP_K, SUB_TILE), lambda s, u, g: (0, s))],
        out_specs=pl.BlockSpec(memory_space=pl.ANY),
        scratch_shapes=([pltpu.VMEM((2, LOCAL_ROWS, d), BF16), pltpu.SemaphoreType.DMA((2,))]
                        + [pltpu.VMEM((MOE_CHUNK, SUB_TILE), BF16)] * (LOCAL_ROWS // MOE_CHUNK)))
    return pl.pallas_call(
        _dispatch_kernel,
        grid_spec=grid_spec,
        out_shape=jax.ShapeDtypeStruct((n_rows_out, d), BF16),
        compiler_params=_cparams("arbitrary"),
        name="moe_dispatch",
    )(gmap, count, f, lrow_t)


def _ffn_kernel(expert_ref, valid_ref, nused_ref, x_ref, wg_ref, wu_ref, wd_ref, y_ref, wgu_s, wd_s):
    b = pl.program_id(0)

    @pl.when(b < nused_ref[0])
    def _():
        @pl.when((b == 0) | (expert_ref[b] != expert_ref[jnp.maximum(b - 1, 0)]))
        def _():
            wgu_s[:, :EXPERT_HIDDEN] = wg_ref[...].astype(BF16)
            wgu_s[:, EXPERT_HIDDEN:] = wu_ref[...].astype(BF16)
            wd_s[...] = wd_ref[...].astype(BF16)

        rows = lax.broadcasted_iota(jnp.int32, (FFN_ROWS, 1), 0)
        x = x_ref[...]
        x = jnp.where(rows < valid_ref[b] * UNIT, x, jnp.zeros_like(x))
        hgu = jnp.dot(x, wgu_s[...], preferred_element_type=F32)
        hid = _silu(hgu[:, :EXPERT_HIDDEN]) * hgu[:, EXPERT_HIDDEN:]
        y_ref[...] = jnp.dot(hid.astype(BF16), wd_s[...], preferred_element_type=F32).astype(BF16)


def _expert_ffn(xs, w_gate, w_up, w_down, layer, expert, valid, n_used):
    n_rows, d = xs.shape
    row_blk = lambda b, e, v, n: (jnp.minimum(b, n[0] - 1), 0)
    of_expert = lambda b, e, v, n: (layer, e[b], 0, 0)
    grid_spec = pltpu.PrefetchScalarGridSpec(
        num_scalar_prefetch=3,
        grid=(n_rows // FFN_ROWS,),
        in_specs=[pl.BlockSpec((FFN_ROWS, d), row_blk),
                  pl.BlockSpec((None, None, d, EXPERT_HIDDEN), of_expert),
                  pl.BlockSpec((None, None, d, EXPERT_HIDDEN), of_expert),
                  pl.BlockSpec((None, None, EXPERT_HIDDEN, d), of_expert)],
        out_specs=pl.BlockSpec((FFN_ROWS, d), row_blk),
        scratch_shapes=[pltpu.VMEM((d, 2 * EXPERT_HIDDEN), BF16), pltpu.VMEM((EXPERT_HIDDEN, d), BF16)])
    return pl.pallas_call(
        _ffn_kernel,
        grid_spec=grid_spec,
        out_shape=jax.ShapeDtypeStruct((n_rows, d), BF16),
        compiler_params=_cparams("arbitrary"),
        name="moe_ffn",
    )(expert, valid, n_used, xs, w_gate, w_up, w_down)


def _combine_kernel(gmap_ref, count_ref, ys_hbm, lrow_ref, w_ref, f_ref, sgu_ref, sdn_ref, h1_ref, g2_ref,
                    lng_ref, lnb_ref, o_ref, buf_ref, sem_ref, acc_ref, lrow_rep_ref, w_rep_ref, *, alpha):
    s = pl.program_id(0)
    n_tiles = pl.num_programs(0)
    slot = s % 2

    def copy(slot, lu, gu):
        return pltpu.make_async_copy(_unit_rows(ys_hbm, gu), _unit_rows(buf_ref.at[slot], lu), sem_ref.at[slot])

    def fetch(tile, slot):
        _for_each_unit(gmap_ref, count_ref, tile, lambda lu, gu: copy(slot, lu, gu).start())

    @pl.when(s == 0)
    def _():
        buf_ref[...] = jnp.zeros(buf_ref.shape, BF16)
        fetch(0, 0)

    @pl.when(s + 1 < n_tiles)
    def _():
        fetch(s + 1, 1 - slot)

    n_units = count_ref[s]

    def wait_one(i, carry):
        copy(slot, 0, 0).wait()
        return carry

    lax.fori_loop(0, n_units, wait_one, 0)

    hgu = jnp.dot(f_ref[...], sgu_ref[...], preferred_element_type=F32)
    hid = _silu(hgu[:, :EXPERT_HIDDEN]) * hgu[:, EXPERT_HIDDEN:]
    acc_ref[...] = jnp.dot(hid.astype(BF16), sdn_ref[...], preferred_element_type=F32)

    lrow = lrow_ref[...]
    w = w_ref[...]
    for k in range(TOP_K):
        lrow_rep_ref[k] = jnp.broadcast_to(lrow[:, k:k + 1], (SUB_TILE, LANES))
        w_rep_ref[k] = jnp.broadcast_to(w[:, k:k + 1], (SUB_TILE, LANES))
    chunk = MOE_CHUNK

    def unsort_chunk(r0):
        pieces = []
        for c0 in range(r0, r0 + chunk, LANES):
            col = lax.broadcasted_iota(jnp.int32, (SUB_TILE, LANES), 1) + c0
            pw = jnp.zeros((SUB_TILE, LANES), F32)
            for k in range(TOP_K):
                pw = jnp.where(lrow_rep_ref[k] == col, w_rep_ref[k], pw)
            pieces.append(pw.astype(BF16))
        return jnp.dot(jnp.concatenate(pieces, axis=1), buf_ref[slot, r0:r0 + chunk, :],
                       preferred_element_type=F32)

    def add_chunk(r0):
        acc_ref[...] += unsort_chunk(r0)

    always = [r0 for r0 in range(0, LOCAL_ROWS, chunk) if r0 + chunk <= SUB_TILE * TOP_K]
    acc_ref[...] += functools.reduce(lambda a, b: a + b, [unsort_chunk(r0) for r0 in always])
    for r0 in range(len(always) * chunk, LOCAL_ROWS, chunk):
        pl.when(r0 < n_units * UNIT)(functools.partial(add_chunk, r0))

    o_ref[...] = _ln(alpha * h1_ref[...] + g2_ref[...] * acc_ref[...]) * lng_ref[...] + lnb_ref[...]


def _combine(ys, lrow, w, f, sgu, sdn, h1, g2, ln_g, ln_b, gmap, count, *, tiles_per_batch, n_batch, alpha):
    t, d = f.shape
    tm = SUB_TILE
    row = lambda s, u, g: (s, 0)
    fixed = lambda s, u, g: (0, 0)
    mod_row = lambda s, u, g: (jnp.minimum(s // tiles_per_batch, n_batch), 0, 0)
    grid_spec = pltpu.PrefetchScalarGridSpec(
        num_scalar_prefetch=2,
        grid=(t // tm,),
        in_specs=[pl.BlockSpec(memory_space=pl.ANY),
                  pl.BlockSpec((tm, LANES), row), pl.BlockSpec((tm, LANES), row), pl.BlockSpec((tm, d), row),
                  pl.BlockSpec(sgu.shape, fixed), pl.BlockSpec(sdn.shape, fixed),
                  pl.BlockSpec((tm, d), row), pl.BlockSpec((None, 1, d), mod_row),
                  pl.BlockSpec((1, d), fixed), pl.BlockSpec((1, d), fixed)],
        out_specs=pl.BlockSpec((tm, d), row),
        scratch_shapes=[pltpu.VMEM((2, LOCAL_ROWS, d), BF16), pltpu.SemaphoreType.DMA((2,)),
                        pltpu.VMEM((tm, d), F32), pltpu.VMEM((TOP_K, tm, LANES), jnp.int32),
                        pltpu.VMEM((TOP_K, tm, LANES), F32)])
    return pl.pallas_call(
        functools.partial(_combine_kernel, alpha=alpha),
        grid_spec=grid_spec,
        out_shape=jax.ShapeDtypeStruct((t, d), F32),
        compiler_params=_cparams("arbitrary"),
        name="moe_combine",
    )(gmap, count, ys, lrow, w, f, sgu, sdn, h1, g2, ln_g.reshape(1, d), ln_b.reshape(1, d))


def _forward(x, c, ctx, c_ctx, w_mod, b_mod, w_in, s5_lam_re, s5_lam_im, s5_log_step, s5_b_re, s5_b_im,
             s5_c_re, s5_c_im, s5_d, gm_w_s, gm_b_s, da_lam, da_subln_g, w_glu_val, w_glu_gate,
             w_proj_gm, w_proj_da, w_out, ln1_g, ln1_b, ln2_g, ln2_b, w_router, b_router,
             w_exp_gate, w_exp_up, w_exp_down, w_sh_gate, w_sh_up, w_sh_down):
    n_batch, seq, d = x.shape
    ctx_len = ctx.shape[1]
    depth = w_mod.shape[0]
    n_lat = n_batch * seq
    alpha = (2 * depth) ** 0.25
    tiles_per_batch = seq // ROW_TILE
    sub_per_batch = seq // SUB_TILE
    assert d == D_MODEL and seq % ROW_TILE == 0 and (n_batch * ctx_len) % ROW_TILE == 0
    assert ctx_len % SUB_TILE == 0 and n_batch == 4

    cond = jnp.concatenate([c, c_ctx[None], jnp.zeros((8 - n_batch - 1, d), F32)], axis=0)
    mod = _modulation(cond, w_mod, b_mod)
    cos_t, sin_t = _rope_tables(seq)
    h = jnp.concatenate([x.reshape(n_lat, d), ctx.reshape(n_batch * ctx_len, d)], axis=0)

    for l in range(depth):
        last = l == depth - 1
        lam_init = 0.8 - 0.6 * math.exp(-0.3 * l)
        names = ("sh1", "sc1", "g1", "sh2", "sc2", "g2")
        mods = {k: mod[l, :, i * d:(i + 1) * d].reshape(8, 1, d) for i, k in enumerate(names)}

        wi = w_in[l]
        a_w, z_w, q_w, k_w, v_w, g_w = jnp.split(wi, (256, 768, 1280, 1792, 2304), axis=1)
        w_in_l = jnp.concatenate([g_w, q_w, k_w, v_w, a_w, z_w], axis=1).astype(BF16)
        p = _inproj(h, mods["sh1"], mods["sc1"], w_in_l, tiles_per_batch, n_batch)

        lf = da_lam[l].astype(F32)
        lam = (jnp.exp(jnp.sum(lf[0] * lf[1])) - jnp.exp(jnp.sum(lf[2] * lf[3])) + lam_init).reshape(1)
        q, k0, k1, v = _rope_pack(p, cos_t, sin_t, n_batch, seq, ctx_len)
        subln = da_subln_g[l].astype(F32).reshape(1, LANES)
        attn = functools.partial(_attention, lam, q, k0, k1, v, subln, out_scale=1.0 - lam_init)
        yc = attn(n_rows=n_lat, q_row0=0, tq=512, kv_len=ctx_len + seq, tk=ctx_len)
        if not last:
            yc_ctx = attn(n_rows=n_batch * ctx_len, q_row0=n_lat, tq=ctx_len, kv_len=ctx_len, tk=ctx_len)
            yc = jnp.concatenate([yc, yc_ctx], axis=0)

        mats = _s5_matrices(s5_lam_re[l], s5_lam_im[l], s5_log_step[l], s5_b_re[l], s5_b_im[l],
                            s5_c_re[l], s5_c_im[l], s5_d[l])
        ya = _s5_mixer(p, mats, n_batch=n_batch, lat_chunks=seq // S5_CHUNK, ctx_chunks=ctx_len // S5_CHUNK)

        n_rows = n_lat if last else h.shape[0]
        router_w = jnp.concatenate([w_router[l].astype(F32), jnp.zeros((d, LANES - N_EXPERTS), F32)], axis=1)
        merge_w = {
            "glu": jnp.concatenate([w_glu_val[l], w_glu_gate[l]], axis=1).astype(BF16),
            "gm": w_proj_gm[l].astype(BF16), "da": w_proj_da[l].astype(BF16), "out": w_out[l].astype(BF16),
            "gm_ws": gm_w_s[l].astype(BF16),
            "gm_bs": jnp.repeat(gm_b_s[l].astype(F32).T, GM_WIDTH // GM_HEADS, axis=1),
            "router": router_w,
        }
        h1, f, logits = _merge(ya, p, yc, h, mods, ln1_g[l], ln1_b[l], merge_w, n_rows=n_rows,
                               tiles_per_batch=seq // MERGE_TILE, n_batch=n_batch, alpha=alpha)

        lrow, w_tok, lrow_t, units_lanes = _route(logits, b_router[l])
        n_tiles = n_rows // SUB_TILE
        units = units_lanes[:, 0].reshape(n_tiles, N_EXPERTS)
        max_units = n_rows * TOP_K // UNIT + n_tiles * N_EXPERTS + N_EXPERTS * (FFN_UNITS - 1)
        n_blocks = max_units // FFN_UNITS + 1
        gmap, tile_units, blk_expert, blk_valid, n_used = _moe_layout(units, n_blocks)
        xs = _dispatch(f, lrow_t, gmap, tile_units, n_blocks * FFN_ROWS)
        ys = _expert_ffn(xs, w_exp_gate, w_exp_up, w_exp_down, l, blk_expert, blk_valid, n_used)
        sgu = jnp.concatenate([w_sh_gate[l], w_sh_up[l]], axis=1).astype(BF16)
        h = _combine(ys, lrow, w_tok, f, sgu, w_sh_down[l].astype(BF16), h1, mods["g2"], ln2_g[l], ln2_b[l],
                     gmap, tile_units, tiles_per_batch=sub_per_batch, n_batch=n_batch, alpha=alpha)

    return h[:n_lat].reshape(n_batch, seq, d)


def kernel(x, c, ctx, c_ctx, w_mod, b_mod, w_in, s5_lam_re, s5_lam_im, s5_log_step, s5_b_re, s5_b_im, s5_c_re, s5_c_im, s5_d, gm_w_s, gm_b_s, da_lam, da_subln_g, w_glu_val, w_glu_gate, w_proj_gm, w_proj_da, w_out, ln1_g, ln1_b, ln2_g, ln2_b, w_router, b_router, w_exp_gate, w_exp_up, w_exp_down, w_sh_gate, w_sh_up, w_sh_down):
    return _forward(x, c, ctx, c_ctx, w_mod, b_mod, w_in, s5_lam_re, s5_lam_im, s5_log_step, s5_b_re, s5_b_im,
                    s5_c_re, s5_c_im, s5_d, gm_w_s, gm_b_s, da_lam, da_subln_g, w_glu_val, w_glu_gate,
                    w_proj_gm, w_proj_da, w_out, ln1_g, ln1_b, ln2_g, ln2_b, w_router, b_router,
                    w_exp_gate, w_exp_up, w_exp_down, w_sh_gate, w_sh_up, w_sh_down)
```

```python
import functools
import math

import jax
import jax.numpy as jnp
from jax import lax
from jax.experimental import pallas as pl
from jax.experimental.pallas import tpu as pltpu

F32 = jnp.float32
BF16 = jnp.bfloat16
HIGHEST = lax.Precision.HIGHEST

D_MODEL = 1024
GRID_W = 64
S5_WIDTH = 256
S5_GROUP = 16
S5_GROUPS = S5_WIDTH // S5_GROUP
S5_STATE = 64
S5_CHUNK = 16
GM_WIDTH = 256
GM_HEADS = 4
GM_CHUNK = 128
DA_HEADS = 4
DA_HEAD_DIM = 64
DA_QK = DA_HEADS * 2 * DA_HEAD_DIM
DA_V = DA_HEADS * 2 * DA_HEAD_DIM
ROPE_BASE = 10000.0
N_BRANCH = 3
N_EXPERTS = 64
TOP_K = 8
N_GROUPS = 8
TOPK_GROUPS = 4
EXPERT_HIDDEN = 256
ROUTED_SCALE = 2.5
LN_EPS = 1e-5

COL_GATE = 0
COL_Q = N_BRANCH * D_MODEL
COL_K = COL_Q + DA_QK
COL_V = COL_K + DA_QK
COL_S5 = COL_V + DA_V
COL_ZU = COL_S5 + S5_WIDTH
COL_ZV = COL_ZU + GM_WIDTH
IN_WIDTH = COL_ZV + GM_WIDTH

LANES = 128
ROW_TILE = 1024
SUB_TILE = 256
MERGE_TILE = 512
VMEM_LIMIT = 48 * 1024 * 1024
UNIT = 16
LOCAL_ROWS = SUB_TILE * TOP_K + N_EXPERTS * UNIT
LOCAL_UNITS = LOCAL_ROWS // UNIT
FFN_ROWS = 512
FFN_UNITS = FFN_ROWS // UNIT
ATTN_ROW_BLOCK = 64
ONEHOT_BLOCK = 128
MOE_CHUNK = 512


def _cparams(*sem):
    return pltpu.CompilerParams(dimension_semantics=sem, vmem_limit_bytes=VMEM_LIMIT)


def _ln(x):
    mu = jnp.mean(x, -1, keepdims=True)
    xc = x - mu
    var = jnp.mean(xc * xc, -1, keepdims=True)
    return xc * lax.rsqrt(var + LN_EPS)


def _gelu(x):
    return 0.5 * x * (1.0 + jnp.tanh(math.sqrt(2.0 / math.pi) * (x + 0.044715 * (x * x * x))))


def _sigmoid(x):
    return 1.0 / (1.0 + jnp.exp(-x))


def _silu(x):
    return x * _sigmoid(x)


def _sigmoid_t(x):
    return 0.5 * jnp.tanh(0.5 * x) + 0.5


def _mod_kernel(c_ref, w_ref, b_ref, o_ref):
    s = _silu(c_ref[...])
    o_ref[...] = jnp.dot(s, w_ref[...], preferred_element_type=F32, precision=HIGHEST) + b_ref[...]


def _modulation(cond, w_mod, b_mod):
    depth, d, n = w_mod.shape
    tn = 1536
    return pl.pallas_call(
        _mod_kernel,
        grid=(depth, n // tn),
        in_specs=[pl.BlockSpec((8, d), lambda l, j: (0, 0)),
                  pl.BlockSpec((None, d, tn), lambda l, j: (l, 0, j)),
                  pl.BlockSpec((None, 1, tn), lambda l, j: (l, 0, j))],
        out_specs=pl.BlockSpec((None, 8, tn), lambda l, j: (l, 0, j)),
        out_shape=jax.ShapeDtypeStruct((depth, 8, n), F32),
        compiler_params=_cparams("arbitrary", "arbitrary"),
        name="modulation",
    )(cond, w_mod, b_mod.reshape(depth, 1, n))


def _inproj_kernel(h_ref, sh_ref, sc_ref, w_ref, o_ref, xn_ref):
    @pl.when(pl.program_id(1) == 0)
    def _():
        x = _ln(h_ref[...])
        xn_ref[...] = (x * (1.0 + sc_ref[...]) + sh_ref[...]).astype(BF16)

    o_ref[...] = jnp.dot(xn_ref[...], w_ref[...], preferred_element_type=F32).astype(BF16)


def _inproj(h, shift, scale, w, tiles_per_batch, n_batch):
    t, d = h.shape
    n = w.shape[1]
    tn = n // 2
    mod_row = lambda i, j: (jnp.minimum(i // tiles_per_batch, n_batch), 0, 0)
    return pl.pallas_call(
        _inproj_kernel,
        grid=(t // ROW_TILE, n // tn),
        in_specs=[pl.BlockSpec((ROW_TILE, d), lambda i, j: (i, 0)),
                  pl.BlockSpec((None, 1, d), mod_row),
                  pl.BlockSpec((None, 1, d), mod_row),
                  pl.BlockSpec((d, tn), lambda i, j: (0, j))],
        out_specs=pl.BlockSpec((ROW_TILE, tn), lambda i, j: (i, j)),
        out_shape=jax.ShapeDtypeStruct((t, n), BF16),
        scratch_shapes=[pltpu.VMEM((ROW_TILE, d), BF16)],
        compiler_params=_cparams("arbitrary", "arbitrary"),
        name="inproj",
    )(h, shift, scale, w)


def _rope_kernel(q_ref, k_ref, v_ref, cos_ref, sin_ref, qo_ref, k0_ref, k1_ref, vo_ref, *, n_lat_tiles):
    is_lat = pl.program_id(0) < n_lat_tiles
    shape = q_ref.shape
    lane = lax.broadcasted_iota(jnp.int32, shape, 1)
    upper16 = (lane % 32) >= 16
    map1 = (lane % LANES) >= DA_HEAD_DIM
    cos = jnp.where(is_lat, cos_ref[...], 1.0)
    sin = jnp.where(is_lat, sin_ref[...], 0.0)

    def rope(x):
        partner = jnp.where(upper16, pltpu.roll(x, 16, 1), pltpu.roll(x, shape[1] - 16, 1))
        return x * cos + partner * sin

    q = rope(q_ref[...].astype(F32)) * (DA_HEAD_DIM ** -0.5)
    k = rope(k_ref[...].astype(F32))
    qo_ref[...] = q.astype(BF16)
    k0_ref[...] = jnp.where(map1, 0.0, k).astype(BF16)
    k1_ref[...] = jnp.where(map1, k, 0.0).astype(BF16)
    ones = jnp.ones((shape[0], LANES), BF16)
    v = v_ref[...]
    vo_ref[...] = jnp.concatenate(
        [piece for hd in range(DA_HEADS) for piece in (v[:, hd * LANES:(hd + 1) * LANES], ones)], axis=1)


def _rope_pack(p, cos_t, sin_t, n_batch, seq, ctx):
    t = p.shape[0]
    nl = seq // SUB_TILE
    nc = ctx // SUB_TILE
    n_lat_tiles = n_batch * nl

    def kv_map(i):
        j = i - n_lat_tiles
        b = jnp.where(i < n_lat_tiles, i // nl, j // nc)
        blk = jnp.where(i < n_lat_tiles, nc + i % nl, j % nc)
        return (b, blk, 0)

    tab_map = lambda i: (jnp.where(i < n_lat_tiles, i % nl, 0), 0)
    col = lambda c: (lambda i: (i, c // DA_QK))
    kv_shape = jax.ShapeDtypeStruct((n_batch, ctx + seq, DA_QK), BF16)
    return pl.pallas_call(
        functools.partial(_rope_kernel, n_lat_tiles=n_lat_tiles),
        grid=(t // SUB_TILE,),
        in_specs=[pl.BlockSpec((SUB_TILE, DA_QK), col(COL_Q)),
                  pl.BlockSpec((SUB_TILE, DA_QK), col(COL_K)),
                  pl.BlockSpec((SUB_TILE, DA_V), col(COL_V)),
                  pl.BlockSpec((SUB_TILE, DA_QK), tab_map),
                  pl.BlockSpec((SUB_TILE, DA_QK), tab_map)],
        out_specs=[pl.BlockSpec((SUB_TILE, DA_QK), lambda i: (i, 0)),
                   pl.BlockSpec((None, SUB_TILE, DA_QK), kv_map),
                   pl.BlockSpec((None, SUB_TILE, DA_QK), kv_map),
                   pl.BlockSpec((None, SUB_TILE, 2 * DA_V), kv_map)],
        out_shape=[jax.ShapeDtypeStruct((t, DA_QK), BF16), kv_shape, kv_shape,
                   jax.ShapeDtypeStruct((n_batch, ctx + seq, 2 * DA_V), BF16)],
        compiler_params=_cparams("arbitrary"),
        name="rope_pack",
    )(p, p, p, cos_t, sin_t)


def _rope_tables(seq):
    pos = jnp.arange(seq)
    row = (pos // GRID_W).astype(F32)[:, None]
    colp = (pos % GRID_W).astype(F32)[:, None]
    axis_dim = DA_HEAD_DIM // 2
    inv_freq = ROPE_BASE ** (-jnp.arange(0, axis_dim, 2, dtype=F32) / axis_dim)
    ang_r = row * inv_freq
    ang_c = colp * inv_freq
    ang = jnp.concatenate([ang_r, ang_r, ang_c, ang_c], -1)
    sign = jnp.concatenate([-jnp.ones((16,), F32), jnp.ones((16,), F32)] * 2)
    reps = DA_QK // DA_HEAD_DIM
    return jnp.tile(jnp.cos(ang), (1, reps)), jnp.tile(jnp.sin(ang) * sign, (1, reps))


def _attn_kernel(lam_ref, q_ref, k0_ref, k1_ref, v_ref, g_ref, o_ref, *scratch, head, n_main, tk, out_scale):
    q = q_ref[...]
    nt = (((1,), (1,)), ((), ()))
    k_refs = (k0_ref, k1_ref)
    s_refs = (scratch[0:2], scratch[2:4])
    m_refs, acc_refs, p_refs = scratch[4:6], scratch[6:8], scratch[8:10]

    opaque_zero = jnp.minimum(pl.program_id(2), 0)

    def keys(j, size):
        if size == head:
            return pl.ds(0, head)
        return pl.ds(pl.multiple_of(head + (j - 1) * tk, math.gcd(head, tk)), tk)

    def scores(j, buf, size=tk):
        for mp in range(2):
            s_refs[buf][mp][:, :size] = lax.dot_general(q, k_refs[mp][keys(j, size), :], nt,
                                                        preferred_element_type=F32)

    def absorb(j, buf, size=tk):
        vc = v_ref[keys(j, size), :]
        for mp in range(2):
            for r0 in range(0, q.shape[0], ATTN_ROW_BLOCK):
                rows = slice(r0, r0 + ATTN_ROW_BLOCK)
                s = s_refs[buf][mp][pl.ds(pl.multiple_of(r0 + opaque_zero, ATTN_ROW_BLOCK), ATTN_ROW_BLOCK),
                                    :size]
                m = m_refs[mp][rows, :]
                m_new = jnp.maximum(m, jnp.broadcast_to(jnp.max(s, -1, keepdims=True), m.shape))
                alpha = jnp.exp(m - m_new)
                p = jnp.exp(s - jnp.concatenate([m_new] * (size // LANES), axis=1))
                acc_refs[mp][rows, :] = jnp.concatenate([alpha, alpha], axis=1) * acc_refs[mp][rows, :]
                m_refs[mp][rows, :] = m_new
                p_refs[mp][rows, :size] = p.astype(BF16)
            acc_refs[mp][...] += jnp.dot(p_refs[mp][:, :size], vc, preferred_element_type=F32)

    for mp in range(2):
        m_refs[mp][...] = jnp.full(m_refs[mp].shape, -jnp.inf, F32)
        acc_refs[mp][...] = jnp.zeros(acc_refs[mp].shape, F32)
    scores(0, 0, head)
    if n_main:
        scores(1, 1)
    absorb(0, 0, head)
    if n_main:
        def body(i, carry):
            scores(2 * i + 2, 0)
            absorb(2 * i + 1, 1)
            scores(2 * i + 3, 1)
            absorb(2 * i + 2, 0)
            return carry

        lax.fori_loop(0, n_main // 2 - 1, body, 0)
        scores(n_main, 0)
        absorb(n_main - 1, 1)
        absorb(n_main, 0)
    a0, a1 = acc_refs[0][...], acc_refs[1][...]
    o = a0[:, :LANES] / a0[:, LANES:] - lam_ref[0] * (a1[:, :LANES] / a1[:, LANES:])
    o = o * lax.rsqrt(jnp.mean(o * o, -1, keepdims=True) + LN_EPS)
    o_ref[...] = (o * g_ref[...] * out_scale).astype(BF16)


def _attention(lam, q, k0, k1, v, subln_g, *, n_rows, q_row0, tq, kv_len, head, tk, out_scale):
    n_batch = k0.shape[0]
    per_batch = n_rows // n_batch // tq
    q0 = q_row0 // tq
    kv_spec = pl.BlockSpec((None, kv_len, LANES), lambda b, h, i: (b, 0, h))
    return pl.pallas_call(
        functools.partial(_attn_kernel, head=head, n_main=(kv_len - head) // tk, tk=tk, out_scale=out_scale),
        grid=(n_batch, DA_HEADS, per_batch),
        scratch_shapes=([pltpu.VMEM((tq, tk), F32)] * 4 + [pltpu.VMEM((tq, LANES), F32)] * 2
                        + [pltpu.VMEM((tq, 2 * LANES), F32)] * 2 + [pltpu.VMEM((tq, tk), BF16)] * 2),
        in_specs=[pl.BlockSpec(memory_space=pltpu.SMEM),
                  pl.BlockSpec((tq, LANES), lambda b, h, i: (q0 + b * per_batch + i, h)),
                  kv_spec, kv_spec,
                  pl.BlockSpec((None, kv_len, 2 * LANES), lambda b, h, i: (b, 0, h)),
                  pl.BlockSpec((1, LANES), lambda b, h, i: (0, 0))],
        out_specs=pl.BlockSpec((tq, LANES), lambda b, h, i: (b * per_batch + i, h)),
        out_shape=jax.ShapeDtypeStruct((n_rows, DA_V), BF16),
        compiler_params=_cparams("arbitrary", "arbitrary", "arbitrary"),
        name="diff_attention",
    )(lam, q, k0, k1, v, subln_g)


def _s5_in_kernel(x_ref, b_ref, u_ref, z_ref, stage_ref):
    @pl.when(pl.program_id(1) == 0)
    def _():
        x = x_ref[...].astype(F32)
        for half in range(S5_WIDTH // LANES):
            stage_ref[half] = x[:, half * LANES:(half + 1) * LANES]
        for s in range(S5_CHUNK):
            for half in range(S5_WIDTH // LANES):
                col = s * S5_WIDTH + half * LANES
                u_ref[:, col:col + LANES] = (
                    stage_ref[half, pl.ds(s, u_ref.shape[0], stride=S5_CHUNK), :].astype(BF16))

    z_ref[...] = jnp.dot(u_ref[...], b_ref[...], preferred_element_type=F32)


def _s5_scan_kernel(z_ref, a1_ref, a2_ref, p_ref, *, n_batch, lat_chunks, ctx_chunks):
    reverse = pl.program_id(0) == 1
    a1 = a1_ref[...]
    a2 = a2_ref[...]
    width = a1.shape[1]

    def swap_halves(s):
        return jnp.concatenate(
            [pltpu.roll(s[:, j * LANES:(j + 1) * LANES], LANES // 2, 1) for j in range(width // LANES)], axis=1)

    def run(base, count, carry):
        def body(i, st):
            s, ssw = st
            row = base + jnp.where(reverse, count - 1 - i, i)
            z = z_ref[pl.ds(row, 1), :]
            p_ref[pl.ds(row, 1), :] = s
            return a1 * s + a2 * ssw + z, a1 * ssw - a2 * s + swap_halves(z)
        return lax.fori_loop(0, count, body, carry, unroll=4)

    zero = jnp.zeros(a1.shape, F32)
    for b in range(n_batch):
        st = run(n_batch * lat_chunks + b * ctx_chunks, ctx_chunks, (zero, zero))
        run(b * lat_chunks, lat_chunks, st)


def _s5_out_kernel(u_ref, p_ref, t_ref, ct_ref, y_ref, stage_ref):
    y = jnp.dot(u_ref[...], t_ref[...], preferred_element_type=F32)
    y = y + lax.dot_general(p_ref[...].astype(BF16), ct_ref[...], (((1,), (1,)), ((), ())),
                            preferred_element_type=F32)
    y = _gelu(y)
    per_step = y.shape[1] // S5_WIDTH
    first = pl.program_id(1) * per_step
    halves = S5_WIDTH // LANES
    for k in range(per_step):
        for half in range(halves):
            col = k * S5_WIDTH + half * LANES
            stage_ref[half, pl.ds(first + k, y.shape[0], stride=S5_CHUNK), :] = y[:, col:col + LANES]

    @pl.when(pl.program_id(1) == pl.num_programs(1) - 1)
    def _():
        for half in range(halves):
            y_ref[:, half * LANES:(half + 1) * LANES] = stage_ref[half]


def _s5_mixer(p, mats, *, n_batch, lat_chunks, ctx_chunks):
    t_all = p.shape[0]
    r = t_all // S5_CHUNK
    w = S5_CHUNK * S5_WIDTH
    tr = r // 4
    state_w = mats["b"].shape[1]
    tn = 1024
    u, z = pl.pallas_call(
        _s5_in_kernel,
        grid=(r // tr, state_w // tn),
        in_specs=[pl.BlockSpec((tr * S5_CHUNK, S5_WIDTH), lambda i, j: (i, COL_S5 // S5_WIDTH)),
                  pl.BlockSpec((w, tn), lambda i, j: (0, j))],
        out_specs=[pl.BlockSpec((tr, w), lambda i, j: (i, 0)),
                   pl.BlockSpec((tr, tn), lambda i, j: (i, j))],
        out_shape=[jax.ShapeDtypeStruct((r, w), BF16), jax.ShapeDtypeStruct((r, state_w), F32)],
        scratch_shapes=[pltpu.VMEM((S5_WIDTH // LANES, tr * S5_CHUNK, LANES), F32)],
        compiler_params=_cparams("arbitrary", "arbitrary"),
        name="s5_in",
    )(p, mats["b"])
    half = state_w // 2
    coef = pl.BlockSpec((None, 1, half), lambda d: (d, 0, 0))
    prev = pl.pallas_call(
        functools.partial(_s5_scan_kernel, n_batch=n_batch, lat_chunks=lat_chunks, ctx_chunks=ctx_chunks),
        grid=(2,),
        in_specs=[pl.BlockSpec((r, half), lambda d: (0, d)), coef, coef],
        out_specs=pl.BlockSpec((r, half), lambda d: (0, d)),
        out_shape=jax.ShapeDtypeStruct((r, state_w), F32),
        compiler_params=_cparams("arbitrary"),
        name="s5_scan",
    )(z, mats["a1"], mats["a2"])
    tn = 512
    return pl.pallas_call(
        _s5_out_kernel,
        grid=(r // tr, w // tn),
        in_specs=[pl.BlockSpec((tr, w), lambda i, j: (i, 0)),
                  pl.BlockSpec((tr, state_w), lambda i, j: (i, 0)),
                  pl.BlockSpec((w, tn), lambda i, j: (0, j)),
                  pl.BlockSpec((tn, state_w), lambda i, j: (j, 0))],
        out_specs=pl.BlockSpec((tr * S5_CHUNK, S5_WIDTH), lambda i, j: (i, 0)),
        out_shape=jax.ShapeDtypeStruct((t_all, S5_WIDTH), F32),
        scratch_shapes=[pltpu.VMEM((S5_WIDTH // LANES, tr * S5_CHUNK, LANES), F32)],
        compiler_params=_cparams("arbitrary", "arbitrary"),
        name="s5_out",
    )(u, prev, mats["t"], mats["ct"])


def _s5_matrices(lam_re, lam_im, log_step, b_re, b_im, c_re, c_im, d_skip):
    n = S5_CHUNK
    dt = jnp.exp(log_step.astype(F32))[..., None]
    lr, li = lam_re.astype(F32), lam_im.astype(F32)
    mag = jnp.exp(lr * dt)
    a_re, a_im = mag * jnp.cos(li * dt), mag * jnp.sin(li * dt)
    den = lr * lr + li * li
    n_re = a_re - 1.0
    z_re = (n_re * lr + a_im * li) / den
    z_im = (a_im * lr - n_re * li) / den
    br, bi = b_re.astype(F32), b_im.astype(F32)
    bb_re = z_re[..., None] * br - z_im[..., None] * bi
    bb_im = z_re[..., None] * bi + z_im[..., None] * br
    j = jnp.arange(n + 1, dtype=F32)[:, None, None, None]
    pmag = jnp.exp(lr * dt * j)
    pw_re, pw_im = pmag * jnp.cos(li * dt * j), pmag * jnp.sin(li * dt * j)
    cr, ci = c_re.astype(F32), c_im.astype(F32)
    g, p, c = S5_GROUPS, S5_STATE, S5_GROUP

    def cmul(xr, xi, yr, yi):
        return xr * yr - xi * yi, xr * yi + xi * yr

    def in_mat(direction, powers):
        er = pw_re[powers, direction][:, :, :, None]
        ei = pw_im[powers, direction][:, :, :, None]
        xr, xi = cmul(er, ei, bb_re[direction][None], bb_im[direction][None])
        m = jnp.concatenate([xr, xi], axis=2)
        return m.transpose(1, 0, 3, 2).reshape(g, n * c, 2 * p)

    def out_mat(direction, powers):
        er = pw_re[powers, direction][:, :, None, :]
        ei = pw_im[powers, direction][:, :, None, :]
        wr, wi = cmul(cr[direction][None], ci[direction][None], er, ei)
        m = jnp.concatenate([wr, -wi], axis=3)
        return m.transpose(1, 3, 0, 2).reshape(g, 2 * p, n * c)

    def toeplitz(direction):
        er = pw_re[:n, direction][:, :, None, :, None]
        ei = pw_im[:n, direction][:, :, None, :, None]
        wr, wi = cmul(cr[direction][None, :, :, :, None], ci[direction][None, :, :, :, None], er, ei)
        k = jnp.sum(wr * bb_re[direction][None, :, None] - wi * bb_im[direction][None, :, None], axis=3)
        return k

    up, down = slice(0, n), slice(n - 1, None, -1)
    up1, down1 = slice(1, n + 1), slice(n, 0, -1)
    kf, kb = toeplitz(0), toeplitz(1)
    skip = d_skip.astype(F32).reshape(g, c)
    k0 = kf[0] + kb[0] + jnp.eye(c, dtype=F32)[None] * skip[:, :, None]
    by_lag = jnp.concatenate([kb[:0:-1], k0[None], kf[1:]], axis=0)
    eye_g = jnp.eye(g, dtype=F32)
    lag_blocks = jnp.einsum("mgca,gh->mgahc", by_lag, eye_g).reshape(2 * n - 1, g * c, g * c)
    steps = jnp.arange(n)
    t_dense = lag_blocks[steps[None, :] - steps[:, None] + n - 1]
    t_dense = t_dense.transpose(0, 2, 1, 3).reshape(n * g * c, n * g * c)

    def coef(direction):
        ar, ai = pw_re[n, direction], pw_im[n, direction]
        return (jnp.concatenate([ar, ar], -1).reshape(-1), jnp.concatenate([-ai, ai], -1).reshape(-1))

    def dense_in(m):
        m = m.reshape(g, n, c, -1)
        return jnp.einsum("gscq,gh->sgchq", m, eye_g).reshape(n * g * c, -1)

    def dense_out_t(m):
        q = m.shape[1]
        return jnp.einsum("gqtc,gh->thcgq", m.reshape(g, q, n, c), eye_g).reshape(n * g * c, g * q)

    a1f, a2f = coef(0)
    a1b, a2b = coef(1)
    return {
        "b": jnp.concatenate([dense_in(in_mat(0, down)), dense_in(in_mat(1, up))], 1).astype(BF16),
        "ct": jnp.concatenate([dense_out_t(out_mat(0, up1)), dense_out_t(out_mat(1, down1))], 1).astype(BF16),
        "t": t_dense.astype(BF16),
        "a1": jnp.stack([a1f, a1b])[:, None, :], "a2": jnp.stack([a2f, a2b])[:, None, :],
    }


def _merge_kernel(ya_ref, zu_ref, zv_ref, yc_ref, ga_ref, gb_ref, gc_ref, h_ref,
                  g1_ref, sh2_ref, sc2_ref, lng_ref, lnb_ref,
                  wglu_ref, wgm_ref, wda_ref, wout_ref, ws_ref, bs_ref, wr_ref,
                  h1_ref, f_ref, lg_ref, *, alpha):
    f32 = lambda ref: ref[...].astype(F32)
    glu = jnp.dot(ya_ref[...].astype(BF16), wglu_ref[...], preferred_element_type=F32)
    branch_a = glu[:, :D_MODEL] * _sigmoid_t(glu[:, D_MODEL:])

    u = _gelu(f32(zu_ref))
    v = _ln(_gelu(f32(zv_ref))).astype(BF16)
    head = lax.broadcasted_iota(jnp.int32, (GM_CHUNK, GM_WIDTH), 1) // (GM_WIDTH // GM_HEADS)
    parts = []
    for ck in range(v.shape[0] // GM_CHUNK):
        vc = v[ck * GM_CHUNK:(ck + 1) * GM_CHUNK]
        s = bs_ref[...]
        for hd in range(GM_HEADS):
            s = s + jnp.dot(ws_ref[hd], jnp.where(head == hd, vc, jnp.zeros_like(vc)),
                            preferred_element_type=F32)
        parts.append(s)
    yb = (u * jnp.concatenate(parts, axis=0)).astype(BF16)

    m = _sigmoid_t(f32(ga_ref)) * branch_a
    m = m + _sigmoid_t(f32(gb_ref)) * jnp.dot(yb, wgm_ref[...], preferred_element_type=F32)
    m = m + _sigmoid_t(f32(gc_ref)) * jnp.dot(yc_ref[...], wda_ref[...], preferred_element_type=F32)
    mix = jnp.dot(m.astype(BF16), wout_ref[...], preferred_element_type=F32)

    h1 = _ln(alpha * h_ref[...] + g1_ref[...] * mix) * lng_ref[...] + lnb_ref[...]
    h1_ref[...] = h1
    f = _ln(h1) * (1.0 + sc2_ref[...]) + sh2_ref[...]
    f_ref[...] = f.astype(BF16)
    lg_ref[...] = jnp.dot(f, wr_ref[...], preferred_element_type=F32, precision=HIGHEST)


def _merge(ya, p, yc, h, mods, ln_g, ln_b, w, *, n_rows, tiles_per_batch, n_batch, alpha):
    d = D_MODEL
    tm = MERGE_TILE
    row = lambda i: (i, 0)
    mod_row = lambda i: (jnp.minimum(i // tiles_per_batch, n_batch), 0, 0)
    pcol = lambda c, width: pl.BlockSpec((tm, width), lambda i: (i, c // width))
    full = lambda a: pl.BlockSpec(a.shape, lambda i: (0,) * a.ndim)
    mod_spec = pl.BlockSpec((None, 1, d), mod_row)
    vec = lambda a: a.reshape(1, d)
    weights = (w["glu"], w["gm"], w["da"], w["out"], w["gm_ws"], w["gm_bs"], w["router"])
    return pl.pallas_call(
        functools.partial(_merge_kernel, alpha=alpha),
        grid=(n_rows // tm,),
        in_specs=[pl.BlockSpec((tm, S5_WIDTH), row),
                  pcol(COL_ZU, GM_WIDTH), pcol(COL_ZV, GM_WIDTH),
                  pl.BlockSpec((tm, DA_V), row),
                  pcol(COL_GATE, d), pcol(COL_GATE + d, d), pcol(COL_GATE + 2 * d, d),
                  pl.BlockSpec((tm, d), row),
                  mod_spec, mod_spec, mod_spec,
                  pl.BlockSpec((1, d), lambda i: (0, 0)), pl.BlockSpec((1, d), lambda i: (0, 0))]
                 + [full(a) for a in weights],
        out_specs=[pl.BlockSpec((tm, d), row), pl.BlockSpec((tm, d), row), pl.BlockSpec((tm, LANES), row)],
        out_shape=[jax.ShapeDtypeStruct((n_rows, d), F32), jax.ShapeDtypeStruct((n_rows, d), BF16),
                   jax.ShapeDtypeStruct((n_rows, LANES), F32)],
        compiler_params=_cparams("arbitrary"),
        name="merge",
    )(ya, p, p, yc, p, p, p, h, mods["g1"], mods["sh2"], mods["sc2"], vec(ln_g), vec(ln_b), *weights)


def _router_kernel(lg_ref, b_ref, before_ref, lrow_ref, w_ref, lrow_t_ref, units_ref):
    tm = lg_ref.shape[0]
    per_group = N_EXPERTS // N_GROUPS
    neg = -jnp.inf
    logits = lg_ref[...].T[:N_EXPERTS]
    scores = _sigmoid(logits).reshape(N_GROUPS, per_group, tm)
    sel = scores + b_ref[...].reshape(N_GROUPS, per_group, 1)

    in_group = lax.broadcasted_iota(jnp.int32, sel.shape, 1)
    top1 = jnp.max(sel, axis=1, keepdims=True)
    first = jnp.min(jnp.where(sel == top1, in_group, per_group), axis=1, keepdims=True)
    top2 = jnp.max(jnp.where(in_group == first, neg, sel), axis=1, keepdims=True)
    gscore = top1 + top2

    gidx = lax.broadcasted_iota(jnp.int32, gscore.shape, 0)
    gsel = jnp.zeros(gscore.shape, jnp.bool_)
    for _ in range(TOPK_GROUPS):
        best = jnp.max(gscore, axis=0, keepdims=True)
        hit = gidx == jnp.min(jnp.where(gscore == best, gidx, N_GROUPS), axis=0, keepdims=True)
        gsel = gsel | hit
        gscore = jnp.where(hit, neg, gscore)

    eidx = lax.broadcasted_iota(jnp.int32, sel.shape, 0) * per_group + in_group
    cand = jnp.where(gsel, sel, neg)
    chosen = jnp.zeros(sel.shape, jnp.bool_)
    hits = []
    for _ in range(TOP_K):
        best = jnp.max(jnp.max(cand, axis=1, keepdims=True), axis=0, keepdims=True)
        at = jnp.where(cand == best, eidx, N_EXPERTS)
        hit = eidx == jnp.min(jnp.min(at, axis=1, keepdims=True), axis=0, keepdims=True)
        hits.append(hit)
        chosen = chosen | hit
        cand = jnp.where(hit, neg, cand)

    w = jnp.where(chosen, scores, 0.0)
    total = jnp.sum(jnp.sum(w, axis=1, keepdims=True), axis=0, keepdims=True)
    w = w / total * ROUTED_SCALE

    onehot = jnp.where(chosen, 1.0, 0.0).reshape(N_EXPERTS, tm).astype(BF16)
    rank = jnp.dot(onehot, before_ref[...], preferred_element_type=F32)
    count = jnp.dot(onehot, jnp.ones((tm, LANES), BF16), preferred_element_type=F32)
    units = jnp.floor((count + (UNIT - 1)) * (1.0 / UNIT))
    ei = lax.broadcasted_iota(jnp.int32, (N_EXPERTS, N_EXPERTS), 0)
    ej = lax.broadcasted_iota(jnp.int32, (N_EXPERTS, N_EXPERTS), 1)
    first_unit = jnp.dot(jnp.where(ej < ei, 1.0, 0.0).astype(BF16), units.astype(BF16),
                         preferred_element_type=F32)
    base = jnp.concatenate([first_unit * UNIT] * (tm // LANES), axis=1)
    pos = (base + rank).reshape(sel.shape)

    pick = lambda hit, val: jnp.sum(jnp.sum(jnp.where(hit, val, 0.0), axis=1, keepdims=True), axis=0)
    pad = jnp.zeros((LANES - TOP_K, tm), F32)
    lrow_t = jnp.concatenate([pick(hit, pos) for hit in hits] + [pad], axis=0)
    w_t = jnp.concatenate([pick(hit, w) for hit in hits] + [pad], axis=0)
    lrow_t_ref[...] = lrow_t[:TOP_K].astype(jnp.int32)
    lrow_ref[...] = lrow_t.T.astype(jnp.int32)
    w_ref[...] = w_t.T
    units_ref[...] = units.astype(jnp.int32)


def _route(logits, b_router):
    t = logits.shape[0]
    tm = SUB_TILE
    n_sub = t // tm
    before = jnp.triu(jnp.ones((tm, tm), F32), 1).astype(BF16)
    tok = pl.BlockSpec((tm, LANES), lambda i: (i, 0))
    pick = pl.BlockSpec((TOP_K, tm), lambda i: (0, i))
    return pl.pallas_call(
        _router_kernel,
        grid=(n_sub,),
        in_specs=[tok, pl.BlockSpec((N_EXPERTS, 1), lambda i: (0, 0)), pl.BlockSpec((tm, tm), lambda i: (0, 0))],
        out_specs=[tok, tok, pick, pl.BlockSpec((N_EXPERTS, LANES), lambda i: (i, 0))],
        out_shape=[jax.ShapeDtypeStruct((t, LANES), jnp.int32), jax.ShapeDtypeStruct((t, LANES), F32),
                   jax.ShapeDtypeStruct((TOP_K, t), jnp.int32),
                   jax.ShapeDtypeStruct((n_sub * N_EXPERTS, LANES), jnp.int32)],
        compiler_params=_cparams("arbitrary"),
        name="router",
    )(logits, b_router.astype(F32).reshape(N_EXPERTS, 1), before)


def _moe_layout(units, n_blocks):
    per_expert = jnp.sum(units, axis=0)
    padded = (per_expert + FFN_UNITS - 1) // FFN_UNITS * FFN_UNITS
    ends = jnp.cumsum(padded)
    starts = ends - padded
    goff = starts[None, :] + jnp.cumsum(units, axis=0) - units
    local_end = jnp.cumsum(units, axis=1)
    lu = jnp.arange(LOCAL_UNITS, dtype=jnp.int32)
    owner = jnp.minimum(jnp.sum(local_end[:, None, :] <= lu[None, :, None], axis=2), N_EXPERTS - 1)
    is_owner = owner[:, :, None] == jnp.arange(N_EXPERTS, dtype=jnp.int32)[None, None, :]
    shift = goff - (local_end - units)
    gmap = lu[None, :] + jnp.sum(jnp.where(is_owner, shift[:, None, :], 0), axis=2)
    n_used = ends[-1] // FFN_UNITS
    blk = jnp.arange(n_blocks, dtype=jnp.int32)
    blk = jnp.minimum(blk, n_used - 1)
    first = blk * FFN_UNITS
    expert = jnp.minimum(jnp.sum(ends[None, :] <= first[:, None], axis=1), N_EXPERTS - 1)
    of_block = expert[:, None] == jnp.arange(N_EXPERTS, dtype=jnp.int32)[None, :]
    range_end = jnp.sum(jnp.where(of_block, (starts + per_expert)[None, :], 0), axis=1)
    valid = jnp.clip(range_end - first, 0, FFN_UNITS)
    i32 = lambda a: a.astype(jnp.int32)
    return i32(gmap.reshape(-1)), i32(local_end[:, -1]), i32(expert), i32(valid), i32(n_used.reshape(1))


def _unit_rows(ref, unit):
    return ref.at[pl.ds(pl.multiple_of(unit * UNIT, UNIT), UNIT), :]


def _for_each_unit(gmap_ref, count_ref, tile, fn):
    def body(lu, carry):
        fn(lu, gmap_ref[tile * LOCAL_UNITS + lu])
        return carry

    lax.fori_loop(0, count_ref[tile], body, 0)


def _dispatch_kernel(gmap_ref, count_ref, f_ref, lrow_t_ref, xs_hbm, buf_ref, sem_ref, *onehot_refs):
    s = pl.program_id(0)
    last = pl.num_programs(0) - 1
    slot = s % 2

    def copy(slot, lu, gu):
        return pltpu.make_async_copy(_unit_rows(buf_ref.at[slot], lu), _unit_rows(xs_hbm, gu), sem_ref.at[slot])

    def drain(step):
        def body(i, carry):
            copy(step % 2, 0, 0).wait()
            return carry
        lax.fori_loop(0, count_ref[step], body, 0)

    @pl.when(s >= 2)
    def _():
        drain(s - 2)

    lt = lrow_t_ref[...]
    f = f_ref[...]
    chunk = MOE_CHUNK

    def sort_chunk(r0, onehot_ref):
        for p0 in range(0, chunk, ONEHOT_BLOCK):
            row = lax.broadcasted_iota(jnp.int32, (ONEHOT_BLOCK, SUB_TILE), 0) + (r0 + p0)
            hit = row == lt[0:1, :]
            for k in range(1, TOP_K):
                hit = hit | (row == lt[k:k + 1, :])
            onehot_ref[p0:p0 + ONEHOT_BLOCK, :] = jnp.where(hit, 1.0, 0.0).astype(BF16)
        buf_ref[slot, r0:r0 + chunk, :] = jnp.dot(onehot_ref[...], f, preferred_element_type=F32).astype(BF16)

    for i, r0 in enumerate(range(0, LOCAL_ROWS, chunk)):
        if r0 + chunk <= SUB_TILE * TOP_K:
            sort_chunk(r0, onehot_refs[i])
        else:
            pl.when(r0 < count_ref[s] * UNIT)(functools.partial(sort_chunk, r0, onehot_refs[i]))

    _for_each_unit(gmap_ref, count_ref, s, lambda lu, gu: copy(slot, lu, gu).start())

    @pl.when(s == last)
    def _():
        drain(s)

        @pl.when(s >= 1)
        def _():
            drain(s - 1)


def _dispatch(f, lrow_t, gmap, count, n_rows_out):
    t, d = f.shape
    grid_spec = pltpu.PrefetchScalarGridSpec(
        num_scalar_prefetch=2,
        grid=(t // SUB_TILE,),
        in_specs=[pl.BlockSpec((SUB_TILE, d), lambda s, u, g: (s, 0)),
                  pl.BlockSpec((TOP_K, SUB_TILE), lambda s, u, g: (0, s))],
        out_specs=pl.BlockSpec(memory_space=pl.ANY),
        scratch_shapes=([pltpu.VMEM((2, LOCAL_ROWS, d), BF16), pltpu.SemaphoreType.DMA((2,))]
                        + [pltpu.VMEM((MOE_CHUNK, SUB_TILE), BF16)] * (LOCAL_ROWS // MOE_CHUNK)))
    return pl.pallas_call(
        _dispatch_kernel,
        grid_spec=grid_spec,
        out_shape=jax.ShapeDtypeStruct((n_rows_out, d), BF16),
        compiler_params=_cparams("arbitrary"),
        name="moe_dispatch",
    )(gmap, count, f, lrow_t)


def _ffn_kernel(expert_ref, valid_ref, nused_ref, x_ref, wg_ref, wu_ref, wd_ref, y_ref, wgu_s, wd_s):
    b = pl.program_id(0)

    @pl.when(b < nused_ref[0])
    def _():
        @pl.when((b == 0) | (expert_ref[b] != expert_ref[jnp.maximum(b - 1, 0)]))
        def _():
            wgu_s[:, :EXPERT_HIDDEN] = wg_ref[...].astype(BF16)
            wgu_s[:, EXPERT_HIDDEN:] = wu_ref[...].astype(BF16)
            wd_s[...] = wd_ref[...].astype(BF16)

        rows = lax.broadcasted_iota(jnp.int32, (FFN_ROWS, 1), 0)
        x = x_ref[...]
        x = jnp.where(rows < valid_ref[b] * UNIT, x, jnp.zeros_like(x))
        hgu = jnp.dot(x, wgu_s[...], preferred_element_type=F32)
        hid = hgu[:, :EXPERT_HIDDEN] * _sigmoid_t(hgu[:, :EXPERT_HIDDEN]) * hgu[:, EXPERT_HIDDEN:]
        y_ref[...] = jnp.dot(hid.astype(BF16), wd_s[...], preferred_element_type=F32).astype(BF16)


def _expert_ffn(xs, w_gate, w_up, w_down, layer, expert, valid, n_used):
    n_rows, d = xs.shape
    row_blk = lambda b, e, v, n: (jnp.minimum(b, n[0] - 1), 0)
    of_expert = lambda b, e, v, n: (layer, e[b], 0, 0)
    grid_spec = pltpu.PrefetchScalarGridSpec(
        num_scalar_prefetch=3,
        grid=(n_rows // FFN_ROWS,),
        in_specs=[pl.BlockSpec((FFN_ROWS, d), row_blk),
                  pl.BlockSpec((None, None, d, EXPERT_HIDDEN), of_expert),
                  pl.BlockSpec((None, None, d, EXPERT_HIDDEN), of_expert),
                  pl.BlockSpec((None, None, EXPERT_HIDDEN, d), of_expert)],
        out_specs=pl.BlockSpec((FFN_ROWS, d), row_blk),
        scratch_shapes=[pltpu.VMEM((d, 2 * EXPERT_HIDDEN), BF16), pltpu.VMEM((EXPERT_HIDDEN, d), BF16)])
    return pl.pallas_call(
        _ffn_kernel,
        grid_spec=grid_spec,
        out_shape=jax.ShapeDtypeStruct((n_rows, d), BF16),
        compiler_params=_cparams("arbitrary"),
        name="moe_ffn",
    )(expert, valid, n_used, xs, w_gate, w_up, w_down)


def _combine_kernel(gmap_ref, count_ref, ys_hbm, lrow_ref, w_ref, f_ref, sgu_ref, sdn_ref, h1_ref, g2_ref,
                    lng_ref, lnb_ref, o_ref, buf_ref, sem_ref, acc_ref, lrow_rep_ref, w_rep_ref, *, alpha):
    s = pl.program_id(0)
    n_tiles = pl.num_programs(0)
    slot = s % 2

    def copy(slot, lu, gu):
        return pltpu.make_async_copy(_unit_rows(ys_hbm, gu), _unit_rows(buf_ref.at[slot], lu), sem_ref.at[slot])

    def fetch(tile, slot):
        _for_each_unit(gmap_ref, count_ref, tile, lambda lu, gu: copy(slot, lu, gu).start())

    @pl.when(s == 0)
    def _():
        buf_ref[...] = jnp.zeros(buf_ref.shape, BF16)
        fetch(0, 0)

    @pl.when(s + 1 < n_tiles)
    def _():
        fetch(s + 1, 1 - slot)

    n_units = count_ref[s]

    def wait_one(i, carry):
        copy(slot, 0, 0).wait()
        return carry

    lax.fori_loop(0, n_units, wait_one, 0)

    hgu = jnp.dot(f_ref[...], sgu_ref[...], preferred_element_type=F32)
    hid = hgu[:, :EXPERT_HIDDEN] * _sigmoid_t(hgu[:, :EXPERT_HIDDEN]) * hgu[:, EXPERT_HIDDEN:]
    acc_ref[...] = jnp.dot(hid.astype(BF16), sdn_ref[...], preferred_element_type=F32)

    lrow = lrow_ref[...]
    w = w_ref[...]
    for k in range(TOP_K):
        lrow_rep_ref[k] = jnp.broadcast_to(lrow[:, k:k + 1], (SUB_TILE, LANES))
        w_rep_ref[k] = jnp.broadcast_to(w[:, k:k + 1], (SUB_TILE, LANES))
    chunk = MOE_CHUNK

    def unsort_chunk(r0):
        pieces = []
        for c0 in range(r0, r0 + chunk, LANES):
            col = lax.broadcasted_iota(jnp.int32, (SUB_TILE, LANES), 1) + c0
            pw = jnp.zeros((SUB_TILE, LANES), F32)
            for k in range(TOP_K):
                pw = jnp.where(lrow_rep_ref[k] == col, w_rep_ref[k], pw)
            pieces.append(pw.astype(BF16))
        return jnp.dot(jnp.concatenate(pieces, axis=1), buf_ref[slot, r0:r0 + chunk, :],
                       preferred_element_type=F32)

    def add_chunk(r0):
        acc_ref[...] += unsort_chunk(r0)

    always = [r0 for r0 in range(0, LOCAL_ROWS, chunk) if r0 + chunk <= SUB_TILE * TOP_K]
    acc_ref[...] += functools.reduce(lambda a, b: a + b, [unsort_chunk(r0) for r0 in always])
    for r0 in range(len(always) * chunk, LOCAL_ROWS, chunk):
        pl.when(r0 < n_units * UNIT)(functools.partial(add_chunk, r0))

    o_ref[...] = _ln(alpha * h1_ref[...] + g2_ref[...] * acc_ref[...]) * lng_ref[...] + lnb_ref[...]


def _combine(ys, lrow, w, f, sgu, sdn, h1, g2, ln_g, ln_b, gmap, count, *, tiles_per_batch, n_batch, alpha):
    t, d = f.shape
    tm = SUB_TILE
    row = lambda s, u, g: (s, 0)
    fixed = lambda s, u, g: (0, 0)
    mod_row = lambda s, u, g: (jnp.minimum(s // tiles_per_batch, n_batch), 0, 0)
    grid_spec = pltpu.PrefetchScalarGridSpec(
        num_scalar_prefetch=2,
        grid=(t // tm,),
        in_specs=[pl.BlockSpec(memory_space=pl.ANY),
                  pl.BlockSpec((tm, LANES), row), pl.BlockSpec((tm, LANES), row), pl.BlockSpec((tm, d), row),
                  pl.BlockSpec(sgu.shape, fixed), pl.BlockSpec(sdn.shape, fixed),
                  pl.BlockSpec((tm, d), row), pl.BlockSpec((None, 1, d), mod_row),
                  pl.BlockSpec((1, d), fixed), pl.BlockSpec((1, d), fixed)],
        out_specs=pl.BlockSpec((tm, d), row),
        scratch_shapes=[pltpu.VMEM((2, LOCAL_ROWS, d), BF16), pltpu.SemaphoreType.DMA((2,)),
                        pltpu.VMEM((tm, d), F32), pltpu.VMEM((TOP_K, tm, LANES), jnp.int32),
                        pltpu.VMEM((TOP_K, tm, LANES), F32)])
    return pl.pallas_call(
        functools.partial(_combine_kernel, alpha=alpha),
        grid_spec=grid_spec,
        out_shape=jax.ShapeDtypeStruct((t, d), F32),
        compiler_params=_cparams("arbitrary"),
        name="moe_combine",
    )(gmap, count, ys, lrow, w, f, sgu, sdn, h1, g2, ln_g.reshape(1, d), ln_b.reshape(1, d))


def _forward(x, c, ctx, c_ctx, w_mod, b_mod, w_in, s5_lam_re, s5_lam_im, s5_log_step, s5_b_re, s5_b_im,
             s5_c_re, s5_c_im, s5_d, gm_w_s, gm_b_s, da_lam, da_subln_g, w_glu_val, w_glu_gate,
             w_proj_gm, w_proj_da, w_out, ln1_g, ln1_b, ln2_g, ln2_b, w_router, b_router,
             w_exp_gate, w_exp_up, w_exp_down, w_sh_gate, w_sh_up, w_sh_down):
    n_batch, seq, d = x.shape
    ctx_len = ctx.shape[1]
    depth = w_mod.shape[0]
    n_lat = n_batch * seq
    alpha = (2 * depth) ** 0.25
    tiles_per_batch = seq // ROW_TILE
    sub_per_batch = seq // SUB_TILE
    assert d == D_MODEL and seq % ROW_TILE == 0 and (n_batch * ctx_len) % ROW_TILE == 0
    assert ctx_len % SUB_TILE == 0 and n_batch == 4

    cond = jnp.concatenate([c, c_ctx[None], jnp.zeros((8 - n_batch - 1, d), F32)], axis=0)
    mod = _modulation(cond, w_mod, b_mod)
    cos_t, sin_t = _rope_tables(seq)
    h = jnp.concatenate([x.reshape(n_lat, d), ctx.reshape(n_batch * ctx_len, d)], axis=0)

    for l in range(depth):
        last = l == depth - 1
        lam_init = 0.8 - 0.6 * math.exp(-0.3 * l)
        names = ("sh1", "sc1", "g1", "sh2", "sc2", "g2")
        mods = {k: mod[l, :, i * d:(i + 1) * d].reshape(8, 1, d) for i, k in enumerate(names)}

        wi = w_in[l]
        a_w, z_w, q_w, k_w, v_w, g_w = jnp.split(wi, (256, 768, 1280, 1792, 2304), axis=1)
        w_in_l = jnp.concatenate([g_w, q_w, k_w, v_w, a_w, z_w], axis=1).astype(BF16)
        p = _inproj(h, mods["sh1"], mods["sc1"], w_in_l, tiles_per_batch, n_batch)

        lf = da_lam[l].astype(F32)
        lam = (jnp.exp(jnp.sum(lf[0] * lf[1])) - jnp.exp(jnp.sum(lf[2] * lf[3])) + lam_init).reshape(1)
        q, k0, k1, v = _rope_pack(p, cos_t, sin_t, n_batch, seq, ctx_len)
        subln = da_subln_g[l].astype(F32).reshape(1, LANES)
        attn = functools.partial(_attention, lam, q, k0, k1, v, subln, out_scale=1.0 - lam_init)
        yc = attn(n_rows=n_lat, q_row0=0, tq=512, kv_len=ctx_len + seq, head=ctx_len, tk=512)
        if not last:
            yc_ctx = attn(n_rows=n_batch * ctx_len, q_row0=n_lat, tq=ctx_len, kv_len=ctx_len,
                          head=ctx_len, tk=ctx_len)
            yc = jnp.concatenate([yc, yc_ctx], axis=0)

        mats = _s5_matrices(s5_lam_re[l], s5_lam_im[l], s5_log_step[l], s5_b_re[l], s5_b_im[l],
                            s5_c_re[l], s5_c_im[l], s5_d[l])
        ya = _s5_mixer(p, mats, n_batch=n_batch, lat_chunks=seq // S5_CHUNK, ctx_chunks=ctx_len // S5_CHUNK)

        n_rows = n_lat if last else h.shape[0]
        router_w = jnp.concatenate([w_router[l].astype(F32), jnp.zeros((d, LANES - N_EXPERTS), F32)], axis=1)
        merge_w = {
            "glu": jnp.concatenate([w_glu_val[l], w_glu_gate[l]], axis=1).astype(BF16),
            "gm": w_proj_gm[l].astype(BF16), "da": w_proj_da[l].astype(BF16), "out": w_out[l].astype(BF16),
            "gm_ws": gm_w_s[l].astype(BF16),
            "gm_bs": jnp.repeat(gm_b_s[l].astype(F32).T, GM_WIDTH // GM_HEADS, axis=1),
            "router": router_w,
        }
        h1, f, logits = _merge(ya, p, yc, h, mods, ln1_g[l], ln1_b[l], merge_w, n_rows=n_rows,
                               tiles_per_batch=seq // MERGE_TILE, n_batch=n_batch, alpha=alpha)

        lrow, w_tok, lrow_t, units_lanes = _route(logits, b_router[l])
        n_tiles = n_rows // SUB_TILE
        units = units_lanes[:, 0].reshape(n_tiles, N_EXPERTS)
        max_units = n_rows * TOP_K // UNIT + n_tiles * N_EXPERTS + N_EXPERTS * (FFN_UNITS - 1)
        n_blocks = max_units // FFN_UNITS + 1
        gmap, tile_units, blk_expert, blk_valid, n_used = _moe_layout(units, n_blocks)
        xs = _dispatch(f, lrow_t, gmap, tile_units, n_blocks * FFN_ROWS)
        ys = _expert_ffn(xs, w_exp_gate, w_exp_up, w_exp_down, l, blk_expert, blk_valid, n_used)
        sgu = jnp.concatenate([w_sh_gate[l], w_sh_up[l]], axis=1).astype(BF16)
        h = _combine(ys, lrow, w_tok, f, sgu, w_sh_down[l].astype(BF16), h1, mods["g2"], ln2_g[l], ln2_b[l],
                     gmap, tile_units, tiles_per_batch=sub_per_batch, n_batch=n_batch, alpha=alpha)

    return h[:n_lat].reshape(n_batch, seq, d)


def kernel(x, c, ctx, c_ctx, w_mod, b_mod, w_in, s5_lam_re, s5_lam_im, s5_log_step, s5_b_re, s5_b_im, s5_c_re, s5_c_im, s5_d, gm_w_s, gm_b_s, da_lam, da_subln_g, w_glu_val, w_glu_gate, w_proj_gm, w_proj_da, w_out, ln1_g, ln1_b, ln2_g, ln2_b, w_router, b_router, w_exp_gate, w_exp_up, w_exp_down, w_sh_gate, w_sh_up, w_sh_down):
    return _forward(x, c, ctx, c_ctx, w_mod, b_mod, w_in, s5_lam_re, s5_lam_im, s5_log_step, s5_b_re, s5_b_im,
                    s5_c_re, s5_c_im, s5_d, gm_w_s, gm_b_s, da_lam, da_subln_g, w_glu_val, w_glu_gate,
                    w_proj_gm, w_proj_da, w_out, ln1_g, ln1_b, ln2_g, ln2_b, w_router, b_router,
                    w_exp_gate, w_exp_up, w_exp_down, w_sh_gate, w_sh_up, w_sh_down)
```

```python
import functools
import math

import jax
import jax.numpy as jnp
from jax import lax
from jax.experimental import pallas as pl
from jax.experimental.pallas import tpu as pltpu

F32 = jnp.float32
BF16 = jnp.bfloat16
HIGHEST = lax.Precision.HIGHEST

D_MODEL = 1024
GRID_W = 64
S5_WIDTH = 256
S5_GROUP = 16
S5_GROUPS = S5_WIDTH // S5_GROUP
S5_STATE = 64
S5_CHUNK = 16
GM_WIDTH = 256
GM_HEADS = 4
GM_CHUNK = 128
DA_HEADS = 4
DA_HEAD_DIM = 64
DA_QK = DA_HEADS * 2 * DA_HEAD_DIM
DA_V = DA_HEADS * 2 * DA_HEAD_DIM
ROPE_BASE = 10000.0
N_BRANCH = 3
N_EXPERTS = 64
TOP_K = 8
N_GROUPS = 8
TOPK_GROUPS = 4
EXPERT_HIDDEN = 256
ROUTED_SCALE = 2.5
LN_EPS = 1e-5

COL_GATE = 0
COL_Q = N_BRANCH * D_MODEL
COL_K = COL_Q + DA_QK
COL_V = COL_K + DA_QK
COL_S5 = COL_V + DA_V
COL_ZU = COL_S5 + S5_WIDTH
COL_ZV = COL_ZU + GM_WIDTH
IN_WIDTH = COL_ZV + GM_WIDTH

LANES = 128
ROW_TILE = 1024
SUB_TILE = 256
MERGE_TILE = 512
VMEM_LIMIT = 48 * 1024 * 1024
UNIT = 16
LOCAL_ROWS = SUB_TILE * TOP_K + N_EXPERTS * UNIT
LOCAL_UNITS = LOCAL_ROWS // UNIT
FFN_ROWS = 512
FFN_UNITS = FFN_ROWS // UNIT
ATTN_ROW_BLOCK = 64
ONEHOT_BLOCK = 128
MOE_CHUNK = 512


def _cparams(*sem):
    return pltpu.CompilerParams(dimension_semantics=sem, vmem_limit_bytes=VMEM_LIMIT)


def _ln(x):
    mu = jnp.mean(x, -1, keepdims=True)
    xc = x - mu
    var = jnp.mean(xc * xc, -1, keepdims=True)
    return xc * lax.rsqrt(var + LN_EPS)


def _gelu(x):
    return 0.5 * x * (1.0 + jnp.tanh(math.sqrt(2.0 / math.pi) * (x + 0.044715 * (x * x * x))))


def _sigmoid(x):
    return 1.0 / (1.0 + jnp.exp(-x))


def _silu(x):
    return x * _sigmoid(x)


def _sigmoid_t(x):
    return 0.5 * jnp.tanh(0.5 * x) + 0.5


def _mod_kernel(c_ref, w_ref, b_ref, o_ref):
    s = _silu(c_ref[...])
    o_ref[...] = jnp.dot(s, w_ref[...], preferred_element_type=F32, precision=HIGHEST) + b_ref[...]


def _modulation(cond, w_mod, b_mod):
    depth, d, n = w_mod.shape
    tn = 1536
    return pl.pallas_call(
        _mod_kernel,
        grid=(depth, n // tn),
        in_specs=[pl.BlockSpec((8, d), lambda l, j: (0, 0)),
                  pl.BlockSpec((None, d, tn), lambda l, j: (l, 0, j)),
                  pl.BlockSpec((None, 1, tn), lambda l, j: (l, 0, j))],
        out_specs=pl.BlockSpec((None, 8, tn), lambda l, j: (l, 0, j)),
        out_shape=jax.ShapeDtypeStruct((depth, 8, n), F32),
        compiler_params=_cparams("arbitrary", "arbitrary"),
        name="modulation",
    )(cond, w_mod, b_mod.reshape(depth, 1, n))


def _inproj_kernel(h_ref, sh_ref, sc_ref, w_ref, o_ref, xn_ref):
    @pl.when(pl.program_id(1) == 0)
    def _():
        x = _ln(h_ref[...])
        xn_ref[...] = (x * (1.0 + sc_ref[...]) + sh_ref[...]).astype(BF16)

    o_ref[...] = jnp.dot(xn_ref[...], w_ref[...], preferred_element_type=F32).astype(BF16)


def _inproj(h, shift, scale, w, tiles_per_batch, n_batch):
    t, d = h.shape
    n = w.shape[1]
    tn = n // 2
    mod_row = lambda i, j: (jnp.minimum(i // tiles_per_batch, n_batch), 0, 0)
    return pl.pallas_call(
        _inproj_kernel,
        grid=(t // ROW_TILE, n // tn),
        in_specs=[pl.BlockSpec((ROW_TILE, d), lambda i, j: (i, 0)),
                  pl.BlockSpec((None, 1, d), mod_row),
                  pl.BlockSpec((None, 1, d), mod_row),
                  pl.BlockSpec((d, tn), lambda i, j: (0, j))],
        out_specs=pl.BlockSpec((ROW_TILE, tn), lambda i, j: (i, j)),
        out_shape=jax.ShapeDtypeStruct((t, n), BF16),
        scratch_shapes=[pltpu.VMEM((ROW_TILE, d), BF16)],
        compiler_params=_cparams("arbitrary", "arbitrary"),
        name="inproj",
    )(h, shift, scale, w)


def _rope_kernel(q_ref, k_ref, v_ref, cos_ref, sin_ref, qo_ref, k0_ref, k1_ref, vo_ref, *, n_lat_tiles):
    is_lat = pl.program_id(0) < n_lat_tiles
    shape = q_ref.shape
    lane = lax.broadcasted_iota(jnp.int32, shape, 1)
    upper16 = (lane % 32) >= 16
    map1 = (lane % LANES) >= DA_HEAD_DIM
    cos = jnp.where(is_lat, cos_ref[...], 1.0)
    sin = jnp.where(is_lat, sin_ref[...], 0.0)

    def rope(x):
        partner = jnp.where(upper16, pltpu.roll(x, 16, 1), pltpu.roll(x, shape[1] - 16, 1))
        return x * cos + partner * sin

    q = rope(q_ref[...].astype(F32)) * (DA_HEAD_DIM ** -0.5)
    k = rope(k_ref[...].astype(F32))
    qo_ref[...] = q.astype(BF16)
    k0_ref[...] = jnp.where(map1, 0.0, k).astype(BF16)
    k1_ref[...] = jnp.where(map1, k, 0.0).astype(BF16)
    ones = jnp.ones((shape[0], LANES), BF16)
    v = v_ref[...]
    vo_ref[...] = jnp.concatenate(
        [piece for hd in range(DA_HEADS) for piece in (v[:, hd * LANES:(hd + 1) * LANES], ones)], axis=1)


def _rope_pack(p, cos_t, sin_t, n_batch, seq, ctx):
    t = p.shape[0]
    nl = seq // SUB_TILE
    nc = ctx // SUB_TILE
    n_lat_tiles = n_batch * nl

    def kv_map(i):
        j = i - n_lat_tiles
        b = jnp.where(i < n_lat_tiles, i // nl, j // nc)
        blk = jnp.where(i < n_lat_tiles, nc + i % nl, j % nc)
        return (b, blk, 0)

    tab_map = lambda i: (jnp.where(i < n_lat_tiles, i % nl, 0), 0)
    col = lambda c: (lambda i: (i, c // DA_QK))
    kv_shape = jax.ShapeDtypeStruct((n_batch, ctx + seq, DA_QK), BF16)
    return pl.pallas_call(
        functools.partial(_rope_kernel, n_lat_tiles=n_lat_tiles),
        grid=(t // SUB_TILE,),
        in_specs=[pl.BlockSpec((SUB_TILE, DA_QK), col(COL_Q)),
                  pl.BlockSpec((SUB_TILE, DA_QK), col(COL_K)),
                  pl.BlockSpec((SUB_TILE, DA_V), col(COL_V)),
                  pl.BlockSpec((SUB_TILE, DA_QK), tab_map),
                  pl.BlockSpec((SUB_TILE, DA_QK), tab_map)],
        out_specs=[pl.BlockSpec((SUB_TILE, DA_QK), lambda i: (i, 0)),
                   pl.BlockSpec((None, SUB_TILE, DA_QK), kv_map),
                   pl.BlockSpec((None, SUB_TILE, DA_QK), kv_map),
                   pl.BlockSpec((None, SUB_TILE, 2 * DA_V), kv_map)],
        out_shape=[jax.ShapeDtypeStruct((t, DA_QK), BF16), kv_shape, kv_shape,
                   jax.ShapeDtypeStruct((n_batch, ctx + seq, 2 * DA_V), BF16)],
        compiler_params=_cparams("arbitrary"),
        name="rope_pack",
    )(p, p, p, cos_t, sin_t)


def _rope_tables(seq):
    pos = jnp.arange(seq)
    row = (pos // GRID_W).astype(F32)[:, None]
    colp = (pos % GRID_W).astype(F32)[:, None]
    axis_dim = DA_HEAD_DIM // 2
    inv_freq = ROPE_BASE ** (-jnp.arange(0, axis_dim, 2, dtype=F32) / axis_dim)
    ang_r = row * inv_freq
    ang_c = colp * inv_freq
    ang = jnp.concatenate([ang_r, ang_r, ang_c, ang_c], -1)
    sign = jnp.concatenate([-jnp.ones((16,), F32), jnp.ones((16,), F32)] * 2)
    reps = DA_QK // DA_HEAD_DIM
    return jnp.tile(jnp.cos(ang), (1, reps)), jnp.tile(jnp.sin(ang) * sign, (1, reps))


def _attn_kernel(lam_ref, q_ref, k0_ref, k1_ref, v_ref, g_ref, o_ref, *scratch, head, n_main, tk, out_scale):
    q = q_ref[...]
    nt = (((1,), (1,)), ((), ()))
    k_refs = (k0_ref, k1_ref)
    s_refs = (scratch[0:2], scratch[2:4])
    m_refs, acc_refs, p_refs = scratch[4:6], scratch[6:8], scratch[8:10]

    opaque_zero = jnp.minimum(pl.program_id(2), 0)

    def keys(j, size):
        if size == head:
            return pl.ds(0, head)
        return pl.ds(pl.multiple_of(head + (j - 1) * tk, math.gcd(head, tk)), tk)

    def scores(j, buf, size=tk):
        for mp in range(2):
            s_refs[buf][mp][:, :size] = lax.dot_general(q, k_refs[mp][keys(j, size), :], nt,
                                                        preferred_element_type=F32)

    def absorb(j, buf, size=tk):
        vc = v_ref[keys(j, size), :]
        for mp in range(2):
            for r0 in range(0, q.shape[0], ATTN_ROW_BLOCK):
                rows = slice(r0, r0 + ATTN_ROW_BLOCK)
                s = s_refs[buf][mp][pl.ds(pl.multiple_of(r0 + opaque_zero, ATTN_ROW_BLOCK), ATTN_ROW_BLOCK),
                                    :size]
                m = m_refs[mp][rows, :]
                m_new = jnp.maximum(m, jnp.broadcast_to(jnp.max(s, -1, keepdims=True), m.shape))
                alpha = jnp.exp(m - m_new)
                p = jnp.exp(s - jnp.concatenate([m_new] * (size // LANES), axis=1))
                acc_refs[mp][rows, :] = jnp.concatenate([alpha, alpha], axis=1) * acc_refs[mp][rows, :]
                m_refs[mp][rows, :] = m_new
                p_refs[mp][rows, :size] = p.astype(BF16)
            acc_refs[mp][...] += jnp.dot(p_refs[mp][:, :size], vc, preferred_element_type=F32)

    for mp in range(2):
        m_refs[mp][...] = jnp.full(m_refs[mp].shape, -jnp.inf, F32)
        acc_refs[mp][...] = jnp.zeros(acc_refs[mp].shape, F32)
    scores(0, 0, head)
    if n_main:
        scores(1, 1)
    absorb(0, 0, head)
    if n_main:
        def body(i, carry):
            scores(2 * i + 2, 0)
            absorb(2 * i + 1, 1)
            scores(2 * i + 3, 1)
            absorb(2 * i + 2, 0)
            return carry

        lax.fori_loop(0, n_main // 2 - 1, body, 0)
        scores(n_main, 0)
        absorb(n_main - 1, 1)
        absorb(n_main, 0)
    a0, a1 = acc_refs[0][...], acc_refs[1][...]
    o = a0[:, :LANES] / a0[:, LANES:] - lam_ref[0] * (a1[:, :LANES] / a1[:, LANES:])
    o = o * lax.rsqrt(jnp.mean(o * o, -1, keepdims=True) + LN_EPS)
    o_ref[...] = (o * g_ref[...] * out_scale).astype(BF16)


def _attention(lam, q, k0, k1, v, subln_g, *, n_rows, q_row0, tq, kv_len, head, tk, out_scale):
    n_batch = k0.shape[0]
    per_batch = n_rows // n_batch // tq
    q0 = q_row0 // tq
    kv_spec = pl.BlockSpec((None, kv_len, LANES), lambda b, h, i: (b, 0, h))
    return pl.pallas_call(
        functools.partial(_attn_kernel, head=head, n_main=(kv_len - head) // tk, tk=tk, out_scale=out_scale),
        grid=(n_batch, DA_HEADS, per_batch),
        scratch_shapes=([pltpu.VMEM((tq, tk), F32)] * 4 + [pltpu.VMEM((tq, LANES), F32)] * 2
                        + [pltpu.VMEM((tq, 2 * LANES), F32)] * 2 + [pltpu.VMEM((tq, tk), BF16)] * 2),
        in_specs=[pl.BlockSpec(memory_space=pltpu.SMEM),
                  pl.BlockSpec((tq, LANES), lambda b, h, i: (q0 + b * per_batch + i, h)),
                  kv_spec, kv_spec,
                  pl.BlockSpec((None, kv_len, 2 * LANES), lambda b, h, i: (b, 0, h)),
                  pl.BlockSpec((1, LANES), lambda b, h, i: (0, 0))],
        out_specs=pl.BlockSpec((tq, LANES), lambda b, h, i: (b * per_batch + i, h)),
        out_shape=jax.ShapeDtypeStruct((n_rows, DA_V), BF16),
        compiler_params=_cparams("arbitrary", "arbitrary", "arbitrary"),
        name="diff_attention",
    )(lam, q, k0, k1, v, subln_g)


def _s5_in_kernel(x_ref, b_ref, u_ref, z_ref, stage_ref):
    @pl.when(pl.program_id(1) == 0)
    def _():
        x = x_ref[...].astype(F32)
        for half in range(S5_WIDTH // LANES):
            stage_ref[half] = x[:, half * LANES:(half + 1) * LANES]
        for s in range(S5_CHUNK):
            for half in range(S5_WIDTH // LANES):
                col = s * S5_WIDTH + half * LANES
                u_ref[:, col:col + LANES] = (
                    stage_ref[half, pl.ds(s, u_ref.shape[0], stride=S5_CHUNK), :].astype(BF16))

    z_ref[...] = jnp.dot(u_ref[...], b_ref[...], preferred_element_type=F32)


def _s5_scan_kernel(z_ref, a1_ref, a2_ref, p_ref, *, n_batch, lat_chunks, ctx_chunks):
    reverse = pl.program_id(0) == 1
    a1 = a1_ref[...]
    a2 = a2_ref[...]
    width = a1.shape[1]

    def swap_halves(s):
        return jnp.concatenate(
            [pltpu.roll(s[:, j * LANES:(j + 1) * LANES], LANES // 2, 1) for j in range(width // LANES)], axis=1)

    def run(base, count, carry):
        def body(i, st):
            s, ssw = st
            row = base + jnp.where(reverse, count - 1 - i, i)
            z = z_ref[pl.ds(row, 1), :]
            p_ref[pl.ds(row, 1), :] = s
            return a1 * s + a2 * ssw + z, a1 * ssw - a2 * s + swap_halves(z)
        return lax.fori_loop(0, count, body, carry, unroll=4)

    zero = jnp.zeros(a1.shape, F32)
    for b in range(n_batch):
        st = run(n_batch * lat_chunks + b * ctx_chunks, ctx_chunks, (zero, zero))
        run(b * lat_chunks, lat_chunks, st)


def _s5_out_kernel(u_ref, p_ref, t_ref, ct_ref, y_ref, stage_ref):
    y = jnp.dot(u_ref[...], t_ref[...], preferred_element_type=F32)
    y = y + lax.dot_general(p_ref[...].astype(BF16), ct_ref[...], (((1,), (1,)), ((), ())),
                            preferred_element_type=F32)
    y = _gelu(y)
    per_step = y.shape[1] // S5_WIDTH
    first = pl.program_id(1) * per_step
    halves = S5_WIDTH // LANES
    for k in range(per_step):
        for half in range(halves):
            col = k * S5_WIDTH + half * LANES
            stage_ref[half, pl.ds(first + k, y.shape[0], stride=S5_CHUNK), :] = y[:, col:col + LANES]

    @pl.when(pl.program_id(1) == pl.num_programs(1) - 1)
    def _():
        for half in range(halves):
            y_ref[:, half * LANES:(half + 1) * LANES] = stage_ref[half]


def _s5_mixer(p, mats, *, n_batch, lat_chunks, ctx_chunks):
    t_all = p.shape[0]
    r = t_all // S5_CHUNK
    w = S5_CHUNK * S5_WIDTH
    tr = r // 4
    state_w = mats["b"].shape[1]
    tn = 1024
    u, z = pl.pallas_call(
        _s5_in_kernel,
        grid=(r // tr, state_w // tn),
        in_specs=[pl.BlockSpec((tr * S5_CHUNK, S5_WIDTH), lambda i, j: (i, COL_S5 // S5_WIDTH)),
                  pl.BlockSpec((w, tn), lambda i, j: (0, j))],
        out_specs=[pl.BlockSpec((tr, w), lambda i, j: (i, 0)),
                   pl.BlockSpec((tr, tn), lambda i, j: (i, j))],
        out_shape=[jax.ShapeDtypeStruct((r, w), BF16), jax.ShapeDtypeStruct((r, state_w), F32)],
        scratch_shapes=[pltpu.VMEM((S5_WIDTH // LANES, tr * S5_CHUNK, LANES), F32)],
        compiler_params=_cparams("arbitrary", "arbitrary"),
        name="s5_in",
    )(p, mats["b"])
    half = state_w // 2
    coef = pl.BlockSpec((None, 1, half), lambda d: (d, 0, 0))
    prev = pl.pallas_call(
        functools.partial(_s5_scan_kernel, n_batch=n_batch, lat_chunks=lat_chunks, ctx_chunks=ctx_chunks),
        grid=(2,),
        in_specs=[pl.BlockSpec((r, half), lambda d: (0, d)), coef, coef],
        out_specs=pl.BlockSpec((r, half), lambda d: (0, d)),
        out_shape=jax.ShapeDtypeStruct((r, state_w), F32),
        compiler_params=_cparams("arbitrary"),
        name="s5_scan",
    )(z, mats["a1"], mats["a2"])
    tn = 512
    return pl.pallas_call(
        _s5_out_kernel,
        grid=(r // tr, w // tn),
        in_specs=[pl.BlockSpec((tr, w), lambda i, j: (i, 0)),
                  pl.BlockSpec((tr, state_w), lambda i, j: (i, 0)),
                  pl.BlockSpec((w, tn), lambda i, j: (0, j)),
                  pl.BlockSpec((tn, state_w), lambda i, j: (j, 0))],
        out_specs=pl.BlockSpec((tr * S5_CHUNK, S5_WIDTH), lambda i, j: (i, 0)),
        out_shape=jax.ShapeDtypeStruct((t_all, S5_WIDTH), F32),
        scratch_shapes=[pltpu.VMEM((S5_WIDTH // LANES, tr * S5_CHUNK, LANES), F32)],
        compiler_params=_cparams("arbitrary", "arbitrary"),
        name="s5_out",
    )(u, prev, mats["t"], mats["ct"])


def _s5_matrices(lam_re, lam_im, log_step, b_re, b_im, c_re, c_im, d_skip):
    n = S5_CHUNK
    dt = jnp.exp(log_step.astype(F32))[..., None]
    lr, li = lam_re.astype(F32), lam_im.astype(F32)
    mag = jnp.exp(lr * dt)
    a_re, a_im = mag * jnp.cos(li * dt), mag * jnp.sin(li * dt)
    den = lr * lr + li * li
    n_re = a_re - 1.0
    z_re = (n_re * lr + a_im * li) / den
    z_im = (a_im * lr - n_re * li) / den
    br, bi = b_re.astype(F32), b_im.astype(F32)
    bb_re = z_re[..., None] * br - z_im[..., None] * bi
    bb_im = z_re[..., None] * bi + z_im[..., None] * br
    j = jnp.arange(n + 1, dtype=F32)[:, None, None, None]
    pmag = jnp.exp(lr * dt * j)
    pw_re, pw_im = pmag * jnp.cos(li * dt * j), pmag * jnp.sin(li * dt * j)
    cr, ci = c_re.astype(F32), c_im.astype(F32)
    g, p, c = S5_GROUPS, S5_STATE, S5_GROUP

    def cmul(xr, xi, yr, yi):
        return xr * yr - xi * yi, xr * yi + xi * yr

    def in_mat(direction, powers):
        er = pw_re[powers, direction][:, :, :, None]
        ei = pw_im[powers, direction][:, :, :, None]
        xr, xi = cmul(er, ei, bb_re[direction][None], bb_im[direction][None])
        m = jnp.concatenate([xr, xi], axis=2)
        return m.transpose(1, 0, 3, 2).reshape(g, n * c, 2 * p)

    def out_mat(direction, powers):
        er = pw_re[powers, direction][:, :, None, :]
        ei = pw_im[powers, direction][:, :, None, :]
        wr, wi = cmul(cr[direction][None], ci[direction][None], er, ei)
        m = jnp.concatenate([wr, -wi], axis=3)
        return m.transpose(1, 3, 0, 2).reshape(g, 2 * p, n * c)

    def toeplitz(direction):
        er = pw_re[:n, direction][:, :, None, :, None]
        ei = pw_im[:n, direction][:, :, None, :, None]
        wr, wi = cmul(cr[direction][None, :, :, :, None], ci[direction][None, :, :, :, None], er, ei)
        k = jnp.sum(wr * bb_re[direction][None, :, None] - wi * bb_im[direction][None, :, None], axis=3)
        return k

    up, down = slice(0, n), slice(n - 1, None, -1)
    up1, down1 = slice(1, n + 1), slice(n, 0, -1)
    kf, kb = toeplitz(0), toeplitz(1)
    skip = d_skip.astype(F32).reshape(g, c)
    k0 = kf[0] + kb[0] + jnp.eye(c, dtype=F32)[None] * skip[:, :, None]
    by_lag = jnp.concatenate([kb[:0:-1], k0[None], kf[1:]], axis=0)
    eye_g = jnp.eye(g, dtype=F32)
    lag_blocks = jnp.einsum("mgca,gh->mgahc", by_lag, eye_g).reshape(2 * n - 1, g * c, g * c)
    steps = jnp.arange(n)
    t_dense = lag_blocks[steps[None, :] - steps[:, None] + n - 1]
    t_dense = t_dense.transpose(0, 2, 1, 3).reshape(n * g * c, n * g * c)

    def coef(direction):
        ar, ai = pw_re[n, direction], pw_im[n, direction]
        return (jnp.concatenate([ar, ar], -1).reshape(-1), jnp.concatenate([-ai, ai], -1).reshape(-1))

    def dense_in(m):
        m = m.reshape(g, n, c, -1)
        return jnp.einsum("gscq,gh->sgchq", m, eye_g).reshape(n * g * c, -1)

    def dense_out_t(m):
        q = m.shape[1]
        return jnp.einsum("gqtc,gh->thcgq", m.reshape(g, q, n, c), eye_g).reshape(n * g * c, g * q)

    a1f, a2f = coef(0)
    a1b, a2b = coef(1)
    return {
        "b": jnp.concatenate([dense_in(in_mat(0, down)), dense_in(in_mat(1, up))], 1).astype(BF16),
        "ct": jnp.concatenate([dense_out_t(out_mat(0, up1)), dense_out_t(out_mat(1, down1))], 1).astype(BF16),
        "t": t_dense.astype(BF16),
        "a1": jnp.stack([a1f, a1b])[:, None, :], "a2": jnp.stack([a2f, a2b])[:, None, :],
    }


def _merge_kernel(ya_ref, zu_ref, zv_ref, yc_ref, ga_ref, gb_ref, gc_ref, h_ref,
                  g1_ref, sh2_ref, sc2_ref, lng_ref, lnb_ref,
                  wglu_ref, wgm_ref, wda_ref, wout_ref, ws_ref, bs_ref, wr_ref,
                  h1_ref, f_ref, lg_ref, *, alpha):
    f32 = lambda ref: ref[...].astype(F32)
    glu = jnp.dot(ya_ref[...].astype(BF16), wglu_ref[...], preferred_element_type=F32)
    branch_a = glu[:, :D_MODEL] * _sigmoid_t(glu[:, D_MODEL:])

    u = _gelu(f32(zu_ref))
    v = _ln(_gelu(f32(zv_ref))).astype(BF16)
    head = lax.broadcasted_iota(jnp.int32, (GM_CHUNK, GM_WIDTH), 1) // (GM_WIDTH // GM_HEADS)
    parts = []
    for ck in range(v.shape[0] // GM_CHUNK):
        vc = v[ck * GM_CHUNK:(ck + 1) * GM_CHUNK]
        s = bs_ref[...]
        for hd in range(GM_HEADS):
            s = s + jnp.dot(ws_ref[hd], jnp.where(head == hd, vc, jnp.zeros_like(vc)),
                            preferred_element_type=F32)
        parts.append(s)
    yb = (u * jnp.concatenate(parts, axis=0)).astype(BF16)

    m = _sigmoid_t(f32(ga_ref)) * branch_a
    m = m + _sigmoid_t(f32(gb_ref)) * jnp.dot(yb, wgm_ref[...], preferred_element_type=F32)
    m = m + _sigmoid_t(f32(gc_ref)) * jnp.dot(yc_ref[...], wda_ref[...], preferred_element_type=F32)
    mix = jnp.dot(m.astype(BF16), wout_ref[...], preferred_element_type=F32)

    h1 = _ln(alpha * h_ref[...] + g1_ref[...] * mix) * lng_ref[...] + lnb_ref[...]
    h1_ref[...] = h1
    f = _ln(h1) * (1.0 + sc2_ref[...]) + sh2_ref[...]
    f_ref[...] = f.astype(BF16)
    lg_ref[...] = jnp.dot(f, wr_ref[...], preferred_element_type=F32, precision=HIGHEST)


def _merge(ya, p, yc, h, mods, ln_g, ln_b, w, *, n_rows, tiles_per_batch, n_batch, alpha):
    d = D_MODEL
    tm = MERGE_TILE
    row = lambda i: (i, 0)
    mod_row = lambda i: (jnp.minimum(i // tiles_per_batch, n_batch), 0, 0)
    pcol = lambda c, width: pl.BlockSpec((tm, width), lambda i: (i, c // width))
    full = lambda a: pl.BlockSpec(a.shape, lambda i: (0,) * a.ndim)
    mod_spec = pl.BlockSpec((None, 1, d), mod_row)
    vec = lambda a: a.reshape(1, d)
    weights = (w["glu"], w["gm"], w["da"], w["out"], w["gm_ws"], w["gm_bs"], w["router"])
    return pl.pallas_call(
        functools.partial(_merge_kernel, alpha=alpha),
        grid=(n_rows // tm,),
        in_specs=[pl.BlockSpec((tm, S5_WIDTH), row),
                  pcol(COL_ZU, GM_WIDTH), pcol(COL_ZV, GM_WIDTH),
                  pl.BlockSpec((tm, DA_V), row),
                  pcol(COL_GATE, d), pcol(COL_GATE + d, d), pcol(COL_GATE + 2 * d, d),
                  pl.BlockSpec((tm, d), row),
                  mod_spec, mod_spec, mod_spec,
                  pl.BlockSpec((1, d), lambda i: (0, 0)), pl.BlockSpec((1, d), lambda i: (0, 0))]
                 + [full(a) for a in weights],
        out_specs=[pl.BlockSpec((tm, d), row), pl.BlockSpec((tm, d), row), pl.BlockSpec((tm, LANES), row)],
        out_shape=[jax.ShapeDtypeStruct((n_rows, d), F32), jax.ShapeDtypeStruct((n_rows, d), BF16),
                   jax.ShapeDtypeStruct((n_rows, LANES), F32)],
        compiler_params=_cparams("arbitrary"),
        name="merge",
    )(ya, p, p, yc, p, p, p, h, mods["g1"], mods["sh2"], mods["sc2"], vec(ln_g), vec(ln_b), *weights)


def _router_kernel(lg_ref, b_ref, before_ref, lrow_ref, w_ref, lrow_t_ref, units_ref):
    tm = lg_ref.shape[0]
    per_group = N_EXPERTS // N_GROUPS
    neg = -jnp.inf
    logits = lg_ref[...].T[:N_EXPERTS]
    scores = _sigmoid(logits).reshape(N_GROUPS, per_group, tm)
    sel = scores + b_ref[...].reshape(N_GROUPS, per_group, 1)

    in_group = lax.broadcasted_iota(jnp.int32, sel.shape, 1)
    top1 = jnp.max(sel, axis=1, keepdims=True)
    first = jnp.min(jnp.where(sel == top1, in_group, per_group), axis=1, keepdims=True)
    top2 = jnp.max(jnp.where(in_group == first, neg, sel), axis=1, keepdims=True)
    gscore = top1 + top2

    gidx = lax.broadcasted_iota(jnp.int32, gscore.shape, 0)
    gsel = jnp.zeros(gscore.shape, jnp.bool_)
    for _ in range(TOPK_GROUPS):
        best = jnp.max(gscore, axis=0, keepdims=True)
        hit = gidx == jnp.min(jnp.where(gscore == best, gidx, N_GROUPS), axis=0, keepdims=True)
        gsel = gsel | hit
        gscore = jnp.where(hit, neg, gscore)

    eidx = lax.broadcasted_iota(jnp.int32, sel.shape, 0) * per_group + in_group
    cand = jnp.where(gsel, sel, neg)
    chosen = jnp.zeros(sel.shape, jnp.bool_)
    hits = []
    for _ in range(TOP_K):
        best = jnp.max(jnp.max(cand, axis=1, keepdims=True), axis=0, keepdims=True)
        at = jnp.where(cand == best, eidx, N_EXPERTS)
        hit = eidx == jnp.min(jnp.min(at, axis=1, keepdims=True), axis=0, keepdims=True)
        hits.append(hit)
        chosen = chosen | hit
        cand = jnp.where(hit, neg, cand)

    w = jnp.where(chosen, scores, 0.0)
    total = jnp.sum(jnp.sum(w, axis=1, keepdims=True), axis=0, keepdims=True)
    w = w / total * ROUTED_SCALE

    onehot = jnp.where(chosen, 1.0, 0.0).reshape(N_EXPERTS, tm).astype(BF16)
    rank = jnp.dot(onehot, before_ref[...], preferred_element_type=F32)
    count = jnp.dot(onehot, jnp.ones((tm, LANES), BF16), preferred_element_type=F32)
    units = jnp.floor((count + (UNIT - 1)) * (1.0 / UNIT))
    ei = lax.broadcasted_iota(jnp.int32, (N_EXPERTS, N_EXPERTS), 0)
    ej = lax.broadcasted_iota(jnp.int32, (N_EXPERTS, N_EXPERTS), 1)
    first_unit = jnp.dot(jnp.where(ej < ei, 1.0, 0.0).astype(BF16), units.astype(BF16),
                         preferred_element_type=F32)
    base = jnp.concatenate([first_unit * UNIT] * (tm // LANES), axis=1)
    pos = (base + rank).reshape(sel.shape)

    pick = lambda hit, val: jnp.sum(jnp.sum(jnp.where(hit, val, 0.0), axis=1, keepdims=True), axis=0)
    pad = jnp.zeros((LANES - TOP_K, tm), F32)
    lrow_t = jnp.concatenate([pick(hit, pos) for hit in hits] + [pad], axis=0)
    w_t = jnp.concatenate([pick(hit, w) for hit in hits] + [pad], axis=0)
    lrow_t_ref[...] = lrow_t[:TOP_K].astype(jnp.int32)
    lrow_ref[...] = lrow_t.T.astype(jnp.int32)
    w_ref[...] = w_t.T
    units_ref[...] = units.astype(jnp.int32)


def _route(logits, b_router):
    t = logits.shape[0]
    tm = SUB_TILE
    n_sub = t // tm
    before = jnp.triu(jnp.ones((tm, tm), F32), 1).astype(BF16)
    tok = pl.BlockSpec((tm, LANES), lambda i: (i, 0))
    pick = pl.BlockSpec((TOP_K, tm), lambda i: (0, i))
    return pl.pallas_call(
        _router_kernel,
        grid=(n_sub,),
        in_specs=[tok, pl.BlockSpec((N_EXPERTS, 1), lambda i: (0, 0)), pl.BlockSpec((tm, tm), lambda i: (0, 0))],
        out_specs=[tok, tok, pick, pl.BlockSpec((N_EXPERTS, LANES), lambda i: (i, 0))],
        out_shape=[jax.ShapeDtypeStruct((t, LANES), jnp.int32), jax.ShapeDtypeStruct((t, LANES), F32),
                   jax.ShapeDtypeStruct((TOP_K, t), jnp.int32),
                   jax.ShapeDtypeStruct((n_sub * N_EXPERTS, LANES), jnp.int32)],
        compiler_params=_cparams("arbitrary"),
        name="router",
    )(logits, b_router.astype(F32).reshape(N_EXPERTS, 1), before)


def _moe_layout(units, n_blocks):
    per_expert = jnp.sum(units, axis=0)
    padded = (per_expert + FFN_UNITS - 1) // FFN_UNITS * FFN_UNITS
    ends = jnp.cumsum(padded)
    starts = ends - padded
    goff = starts[None, :] + jnp.cumsum(units, axis=0) - units
    local_end = jnp.cumsum(units, axis=1)
    lu = jnp.arange(LOCAL_UNITS, dtype=jnp.int32)
    owner = jnp.minimum(jnp.sum(local_end[:, None, :] <= lu[None, :, None], axis=2), N_EXPERTS - 1)
    is_owner = owner[:, :, None] == jnp.arange(N_EXPERTS, dtype=jnp.int32)[None, None, :]
    shift = goff - (local_end - units)
    gmap = lu[None, :] + jnp.sum(jnp.where(is_owner, shift[:, None, :], 0), axis=2)
    n_used = ends[-1] // FFN_UNITS
    blk = jnp.arange(n_blocks, dtype=jnp.int32)
    blk = jnp.minimum(blk, n_used - 1)
    first = blk * FFN_UNITS
    expert = jnp.minimum(jnp.sum(ends[None, :] <= first[:, None], axis=1), N_EXPERTS - 1)
    of_block = expert[:, None] == jnp.arange(N_EXPERTS, dtype=jnp.int32)[None, :]
    range_end = jnp.sum(jnp.where(of_block, (starts + per_expert)[None, :], 0), axis=1)
    valid = jnp.clip(range_end - first, 0, FFN_UNITS)
    i32 = lambda a: a.astype(jnp.int32)
    return i32(gmap.reshape(-1)), i32(local_end[:, -1]), i32(expert), i32(valid), i32(n_used.reshape(1))


def _unit_rows(ref, unit):
    return ref.at[pl.ds(pl.multiple_of(unit * UNIT, UNIT), UNIT), :]


def _for_each_unit(gmap_ref, count_ref, tile, fn):
    def body(lu, carry):
        fn(lu, gmap_ref[tile * LOCAL_UNITS + lu])
        return carry

    lax.fori_loop(0, count_ref[tile], body, 0)


def _dispatch_kernel(gmap_ref, count_ref, f_ref, lrow_t_ref, xs_hbm, buf_ref, sem_ref, *onehot_refs):
    s = pl.program_id(0)
    last = pl.num_programs(0) - 1
    slot = s % 2

    def copy(slot, lu, gu):
        return pltpu.make_async_copy(_unit_rows(buf_ref.at[slot], lu), _unit_rows(xs_hbm, gu), sem_ref.at[slot])

    def drain(step):
        def body(i, carry):
            copy(step % 2, 0, 0).wait()
            return carry
        lax.fori_loop(0, count_ref[step], body, 0)

    @pl.when(s >= 2)
    def _():
        drain(s - 2)

    lt = lrow_t_ref[...]
    f = f_ref[...]
    chunk = MOE_CHUNK

    def sort_chunk(r0, onehot_ref):
        for p0 in range(0, chunk, ONEHOT_BLOCK):
            row = lax.broadcasted_iota(jnp.int32, (ONEHOT_BLOCK, SUB_TILE), 0) + (r0 + p0)
            hit = row == lt[0:1, :]
            for k in range(1, TOP_K):
                hit = hit | (row == lt[k:k + 1, :])
            onehot_ref[p0:p0 + ONEHOT_BLOCK, :] = jnp.where(hit, 1.0, 0.0).astype(BF16)
        buf_ref[slot, r0:r0 + chunk, :] = jnp.dot(onehot_ref[...], f, preferred_element_type=F32).astype(BF16)

    for i, r0 in enumerate(range(0, LOCAL_ROWS, chunk)):
        if r0 + chunk <= SUB_TILE * TOP_K:
            sort_chunk(r0, onehot_refs[i])
        else:
            pl.when(r0 < count_ref[s] * UNIT)(functools.partial(sort_chunk, r0, onehot_refs[i]))

    _for_each_unit(gmap_ref, count_ref, s, lambda lu, gu: copy(slot, lu, gu).start())

    @pl.when(s == last)
    def _():
        drain(s)

        @pl.when(s >= 1)
        def _():
            drain(s - 1)


def _dispatch(f, lrow_t, gmap, count, n_rows_out):
    t, d = f.shape
    grid_spec = pltpu.PrefetchScalarGridSpec(
        num_scalar_prefetch=2,
        grid=(t // SUB_TILE,),
        in_specs=[pl.BlockSpec((SUB_TILE, d), lambda s, u, g: (s, 0)),
                  pl.BlockSpec((TOP_K, SUB_TILE), lambda s, u, g: (0, s))],
        out_specs=pl.BlockSpec(memory_space=pl.ANY),
        scratch_shapes=([pltpu.VMEM((2, LOCAL_ROWS, d), BF16), pltpu.SemaphoreType.DMA((2,))]
                        + [pltpu.VMEM((MOE_CHUNK, SUB_TILE), BF16)] * (LOCAL_ROWS // MOE_CHUNK)))
    return pl.pallas_call(
        _dispatch_kernel,
        grid_spec=grid_spec,
        out_shape=jax.ShapeDtypeStruct((n_rows_out, d), BF16),
        compiler_params=_cparams("arbitrary"),
        name="moe_dispatch",
    )(gmap, count, f, lrow_t)


def _ffn_kernel(expert_ref, valid_ref, nused_ref, x_ref, wg_ref, wu_ref, wd_ref, y_ref, wgu_s, wd_s):
    b = pl.program_id(0)

    @pl.when(b < nused_ref[0])
    def _():
        @pl.when((b == 0) | (expert_ref[b] != expert_ref[jnp.maximum(b - 1, 0)]))
        def _():
            wgu_s[:, :EXPERT_HIDDEN] = wg_ref[...].astype(BF16)
            wgu_s[:, EXPERT_HIDDEN:] = wu_ref[...].astype(BF16)
            wd_s[...] = wd_ref[...].astype(BF16)

        rows = lax.broadcasted_iota(jnp.int32, (FFN_ROWS, 1), 0)
        x = x_ref[...]
        x = jnp.where(rows < valid_ref[b] * UNIT, x, jnp.zeros_like(x))
        hgu = jnp.dot(x, wgu_s[...], preferred_element_type=F32)
        hid = hgu[:, :EXPERT_HIDDEN] * _sigmoid_t(hgu[:, :EXPERT_HIDDEN]) * hgu[:, EXPERT_HIDDEN:]
        y_ref[...] = jnp.dot(hid.astype(BF16), wd_s[...], preferred_element_type=F32).astype(BF16)


def _expert_ffn(xs, w_gate, w_up, w_down, layer, expert, valid, n_used):
    n_rows, d = xs.shape
    row_blk = lambda b, e, v, n: (jnp.minimum(b, n[0] - 1), 0)
    of_expert = lambda b, e, v, n: (layer, e[b], 0, 0)
    grid_spec = pltpu.PrefetchScalarGridSpec(
        num_scalar_prefetch=3,
        grid=(n_rows // FFN_ROWS,),
        in_specs=[pl.BlockSpec((FFN_ROWS, d), row_blk),
                  pl.BlockSpec((None, None, d, EXPERT_HIDDEN), of_expert),
                  pl.BlockSpec((None, None, d, EXPERT_HIDDEN), of_expert),
                  pl.BlockSpec((None, None, EXPERT_HIDDEN, d), of_expert)],
        out_specs=pl.BlockSpec((FFN_ROWS, d), row_blk),
        scratch_shapes=[pltpu.VMEM((d, 2 * EXPERT_HIDDEN), BF16), pltpu.VMEM((EXPERT_HIDDEN, d), BF16)])
    return pl.pallas_call(
        _ffn_kernel,
        grid_spec=grid_spec,
        out_shape=jax.ShapeDtypeStruct((n_rows, d), BF16),
        compiler_params=_cparams("arbitrary"),
        name="moe_ffn",
    )(expert, valid, n_used, xs, w_gate, w_up, w_down)


def _combine_kernel(gmap_ref, count_ref, ys_hbm, lrow_ref, w_ref, f_ref, sgu_ref, sdn_ref, h1_ref, g2_ref,
                    lng_ref, lnb_ref, o_ref, buf_ref, sem_ref, acc_ref, lrow_rep_ref, w_rep_ref, *, alpha):
    s = pl.program_id(0)
    n_tiles = pl.num_programs(0)
    slot = s % 2

    def copy(slot, lu, gu):
        return pltpu.make_async_copy(_unit_rows(ys_hbm, gu), _unit_rows(buf_ref.at[slot], lu), sem_ref.at[slot])

    def fetch(tile, slot):
        _for_each_unit(gmap_ref, count_ref, tile, lambda lu, gu: copy(slot, lu, gu).start())

    @pl.when(s == 0)
    def _():
        buf_ref[...] = jnp.zeros(buf_ref.shape, BF16)
        fetch(0, 0)

    @pl.when(s + 1 < n_tiles)
    def _():
        fetch(s + 1, 1 - slot)

    n_units = count_ref[s]

    def wait_one(i, carry):
        copy(slot, 0, 0).wait()
        return carry

    lax.fori_loop(0, n_units, wait_one, 0)

    hgu = jnp.dot(f_ref[...], sgu_ref[...], preferred_element_type=F32)
    hid = hgu[:, :EXPERT_HIDDEN] * _sigmoid_t(hgu[:, :EXPERT_HIDDEN]) * hgu[:, EXPERT_HIDDEN:]
    acc_ref[...] = jnp.dot(hid.astype(BF16), sdn_ref[...], preferred_element_type=F32)

    lrow = lrow_ref[...]
    w = w_ref[...]
    for k in range(TOP_K):
        lrow_rep_ref[k] = jnp.broadcast_to(lrow[:, k:k + 1], (SUB_TILE, LANES))
        w_rep_ref[k] = jnp.broadcast_to(w[:, k:k + 1], (SUB_TILE, LANES))
    chunk = MOE_CHUNK

    def unsort_chunk(r0):
        pieces = []
        for c0 in range(r0, r0 + chunk, LANES):
            col = lax.broadcasted_iota(jnp.int32, (SUB_TILE, LANES), 1) + c0
            pw = jnp.zeros((SUB_TILE, LANES), F32)
            for k in range(TOP_K):
                pw = jnp.where(lrow_rep_ref[k] == col, w_rep_ref[k], pw)
            pieces.append(pw.astype(BF16))
        return jnp.dot(jnp.concatenate(pieces, axis=1), buf_ref[slot, r0:r0 + chunk, :],
                       preferred_element_type=F32)

    def add_chunk(r0):
        acc_ref[...] += unsort_chunk(r0)

    always = [r0 for r0 in range(0, LOCAL_ROWS, chunk) if r0 + chunk <= SUB_TILE * TOP_K]
    acc_ref[...] += functools.reduce(lambda a, b: a + b, [unsort_chunk(r0) for r0 in always])
    for r0 in range(len(always) * chunk, LOCAL_ROWS, chunk):
        pl.when(r0 < n_units * UNIT)(functools.partial(add_chunk, r0))

    o_ref[...] = _ln(alpha * h1_ref[...] + g2_ref[...] * acc_ref[...]) * lng_ref[...] + lnb_ref[...]


def _combine(ys, lrow, w, f, sgu, sdn, h1, g2, ln_g, ln_b, gmap, count, *, tiles_per_batch, n_batch, alpha):
    t, d = f.shape
    tm = SUB_TILE
    row = lambda s, u, g: (s, 0)
    fixed = lambda s, u, g: (0, 0)
    mod_row = lambda s, u, g: (jnp.minimum(s // tiles_per_batch, n_batch), 0, 0)
    grid_spec = pltpu.PrefetchScalarGridSpec(
        num_scalar_prefetch=2,
        grid=(t // tm,),
        in_specs=[pl.BlockSpec(memory_space=pl.ANY),
                  pl.BlockSpec((tm, LANES), row), pl.BlockSpec((tm, LANES), row), pl.BlockSpec((tm, d), row),
                  pl.BlockSpec(sgu.shape, fixed), pl.BlockSpec(sdn.shape, fixed),
                  pl.BlockSpec((tm, d), row), pl.BlockSpec((None, 1, d), mod_row),
                  pl.BlockSpec((1, d), fixed), pl.BlockSpec((1, d), fixed)],
        out_specs=pl.BlockSpec((tm, d), row),
        scratch_shapes=[pltpu.VMEM((2, LOCAL_ROWS, d), BF16), pltpu.SemaphoreType.DMA((2,)),
                        pltpu.VMEM((tm, d), F32), pltpu.VMEM((TOP_K, tm, LANES), jnp.int32),
                        pltpu.VMEM((TOP_K, tm, LANES), F32)])
    return pl.pallas_call(
        functools.partial(_combine_kernel, alpha=alpha),
        grid_spec=grid_spec,
        out_shape=jax.ShapeDtypeStruct((t, d), F32),
        compiler_params=_cparams("arbitrary"),
        name="moe_combine",
    )(gmap, count, ys, lrow, w, f, sgu, sdn, h1, g2, ln_g.reshape(1, d), ln_b.reshape(1, d))


def _forward(x, c, ctx, c_ctx, w_mod, b_mod, w_in, s5_lam_re, s5_lam_im, s5_log_step, s5_b_re, s5_b_im,
             s5_c_re, s5_c_im, s5_d, gm_w_s, gm_b_s, da_lam, da_subln_g, w_glu_val, w_glu_gate,
             w_proj_gm, w_proj_da, w_out, ln1_g, ln1_b, ln2_g, ln2_b, w_router, b_router,
             w_exp_gate, w_exp_up, w_exp_down, w_sh_gate, w_sh_up, w_sh_down):
    n_batch, seq, d = x.shape
    ctx_len = ctx.shape[1]
    depth = w_mod.shape[0]
    n_lat = n_batch * seq
    alpha = (2 * depth) ** 0.25
    tiles_per_batch = seq // ROW_TILE
    sub_per_batch = seq // SUB_TILE
    assert d == D_MODEL and seq % ROW_TILE == 0 and (n_batch * ctx_len) % ROW_TILE == 0
    assert ctx_len % SUB_TILE == 0 and n_batch == 4

    cond = jnp.concatenate([c, c_ctx[None], jnp.zeros((8 - n_batch - 1, d), F32)], axis=0)
    mod = _modulation(cond, w_mod, b_mod)
    cos_t, sin_t = _rope_tables(seq)
    h = jnp.concatenate([x.reshape(n_lat, d), ctx.reshape(n_batch * ctx_len, d)], axis=0)

    for l in range(depth):
        last = l == depth - 1
        lam_init = 0.8 - 0.6 * math.exp(-0.3 * l)
        names = ("sh1", "sc1", "g1", "sh2", "sc2", "g2")
        mods = {k: mod[l, :, i * d:(i + 1) * d].reshape(8, 1, d) for i, k in enumerate(names)}

        wi = w_in[l]
        a_w, z_w, q_w, k_w, v_w, g_w = jnp.split(wi, (256, 768, 1280, 1792, 2304), axis=1)
        w_in_l = jnp.concatenate([g_w, q_w, k_w, v_w, a_w, z_w], axis=1).astype(BF16)
        p = _inproj(h, mods["sh1"], mods["sc1"], w_in_l, tiles_per_batch, n_batch)

        lf = da_lam[l].astype(F32)
        lam = (jnp.exp(jnp.sum(lf[0] * lf[1])) - jnp.exp(jnp.sum(lf[2] * lf[3])) + lam_init).reshape(1)
        q, k0, k1, v = _rope_pack(p, cos_t, sin_t, n_batch, seq, ctx_len)
        subln = da_subln_g[l].astype(F32).reshape(1, LANES)
        attn = functools.partial(_attention, lam, q, k0, k1, v, subln, out_scale=1.0 - lam_init)
        yc = attn(n_rows=n_lat, q_row0=0, tq=512, kv_len=ctx_len + seq, head=ctx_len, tk=1024)
        if not last:
            yc_ctx = attn(n_rows=n_batch * ctx_len, q_row0=n_lat, tq=ctx_len, kv_len=ctx_len,
                          head=ctx_len, tk=ctx_len)
            yc = jnp.concatenate([yc, yc_ctx], axis=0)

        mats = _s5_matrices(s5_lam_re[l], s5_lam_im[l], s5_log_step[l], s5_b_re[l], s5_b_im[l],
                            s5_c_re[l], s5_c_im[l], s5_d[l])
        ya = _s5_mixer(p, mats, n_batch=n_batch, lat_chunks=seq // S5_CHUNK, ctx_chunks=ctx_len // S5_CHUNK)

        n_rows = n_lat if last else h.shape[0]
        router_w = jnp.concatenate([w_router[l].astype(F32), jnp.zeros((d, LANES - N_EXPERTS), F32)], axis=1)
        merge_w = {
            "glu": jnp.concatenate([w_glu_val[l], w_glu_gate[l]], axis=1).astype(BF16),
            "gm": w_proj_gm[l].astype(BF16), "da": w_proj_da[l].astype(BF16), "out": w_out[l].astype(BF16),
            "gm_ws": gm_w_s[l].astype(BF16),
            "gm_bs": jnp.repeat(gm_b_s[l].astype(F32).T, GM_WIDTH // GM_HEADS, axis=1),
            "router": router_w,
        }
        h1, f, logits = _merge(ya, p, yc, h, mods, ln1_g[l], ln1_b[l], merge_w, n_rows=n_rows,
                               tiles_per_batch=seq // MERGE_TILE, n_batch=n_batch, alpha=alpha)

        lrow, w_tok, lrow_t, units_lanes = _route(logits, b_router[l])
        n_tiles = n_rows // SUB_TILE
        units = units_lanes[:, 0].reshape(n_tiles, N_EXPERTS)
        max_units = n_rows * TOP_K // UNIT + n_tiles * N_EXPERTS + N_EXPERTS * (FFN_UNITS - 1)
        n_blocks = max_units // FFN_UNITS + 1
        gmap, tile_units, blk_expert, blk_valid, n_used = _moe_layout(units, n_blocks)
        xs = _dispatch(f, lrow_t, gmap, tile_units, n_blocks * FFN_ROWS)
        ys = _expert_ffn(xs, w_exp_gate, w_exp_up, w_exp_down, l, blk_expert, blk_valid, n_used)
        sgu = jnp.concatenate([w_sh_gate[l], w_sh_up[l]], axis=1).astype(BF16)
        h = _combine(ys, lrow, w_tok, f, sgu, w_sh_down[l].astype(BF16), h1, mods["g2"], ln2_g[l], ln2_b[l],
                     gmap, tile_units, tiles_per_batch=sub_per_batch, n_batch=n_batch, alpha=alpha)

    return h[:n_lat].reshape(n_batch, seq, d)


def kernel(x, c, ctx, c_ctx, w_mod, b_mod, w_in, s5_lam_re, s5_lam_im, s5_log_step, s5_b_re, s5_b_im, s5_c_re, s5_c_im, s5_d, gm_w_s, gm_b_s, da_lam, da_subln_g, w_glu_val, w_glu_gate, w_proj_gm, w_proj_da, w_out, ln1_g, ln1_b, ln2_g, ln2_b, w_router, b_router, w_exp_gate, w_exp_up, w_exp_down, w_sh_gate, w_sh_up, w_sh_down):
    return _forward(x, c, ctx, c_ctx, w_mod, b_mod, w_in, s5_lam_re, s5_lam_im, s5_log_step, s5_b_re, s5_b_im,
                    s5_c_re, s5_c_im, s5_d, gm_w_s, gm_b_s, da_lam, da_subln_g, w_glu_val, w_glu_gate,
                    w_proj_gm, w_proj_da, w_out, ln1_g, ln1_b, ln2_g, ln2_b, w_router, b_router,
                    w_exp_gate, w_exp_up, w_exp_down, w_sh_gate, w_sh_up, w_sh_down)
```

```python
import functools
import math

import jax
import jax.numpy as jnp
from jax import lax
from jax.experimental import pallas as pl
from jax.experimental.pallas import tpu as pltpu

F32 = jnp.float32
BF16 = jnp.bfloat16
HIGHEST = lax.Precision.HIGHEST

D_MODEL = 1024
GRID_W = 64
S5_WIDTH = 256
S5_GROUP = 16
S5_GROUPS = S5_WIDTH // S5_GROUP
S5_STATE = 64
S5_CHUNK = 16
GM_WIDTH = 256
GM_HEADS = 4
GM_CHUNK = 128
DA_HEADS = 4
DA_HEAD_DIM = 64
DA_QK = DA_HEADS * 2 * DA_HEAD_DIM
DA_V = DA_HEADS * 2 * DA_HEAD_DIM
ROPE_BASE = 10000.0
N_BRANCH = 3
N_EXPERTS = 64
TOP_K = 8
N_GROUPS = 8
TOPK_GROUPS = 4
EXPERT_HIDDEN = 256
ROUTED_SCALE = 2.5
LN_EPS = 1e-5

COL_GATE = 0
COL_Q = N_BRANCH * D_MODEL
COL_K = COL_Q + DA_QK
COL_V = COL_K + DA_QK
COL_S5 = COL_V + DA_V
COL_ZU = COL_S5 + S5_WIDTH
COL_ZV = COL_ZU + GM_WIDTH
IN_WIDTH = COL_ZV + GM_WIDTH

LANES = 128
ROW_TILE = 1024
SUB_TILE = 256
MERGE_TILE = 512
VMEM_LIMIT = 48 * 1024 * 1024
UNIT = 16
LOCAL_ROWS = SUB_TILE * TOP_K + N_EXPERTS * UNIT
LOCAL_UNITS = LOCAL_ROWS // UNIT
FFN_ROWS = 512
FFN_UNITS = FFN_ROWS // UNIT
ATTN_ROW_BLOCK = 64
ONEHOT_BLOCK = 128
MOE_CHUNK = 512


def _cparams(*sem):
    return pltpu.CompilerParams(dimension_semantics=sem, vmem_limit_bytes=VMEM_LIMIT)


def _ln(x):
    mu = jnp.mean(x, -1, keepdims=True)
    xc = x - mu
    var = jnp.mean(xc * xc, -1, keepdims=True)
    return xc * lax.rsqrt(var + LN_EPS)


def _gelu(x):
    return 0.5 * x * (1.0 + jnp.tanh(math.sqrt(2.0 / math.pi) * (x + 0.044715 * (x * x * x))))


def _sigmoid(x):
    return 1.0 / (1.0 + jnp.exp(-x))


def _silu(x):
    return x * _sigmoid(x)


def _sigmoid_t(x):
    return 0.5 * jnp.tanh(0.5 * x) + 0.5


def _mod_kernel(c_ref, w_ref, b_ref, o_ref):
    s = _silu(c_ref[...])
    o_ref[...] = jnp.dot(s, w_ref[...], preferred_element_type=F32, precision=HIGHEST) + b_ref[...]


def _modulation(cond, w_mod, b_mod):
    depth, d, n = w_mod.shape
    tn = 1536
    return pl.pallas_call(
        _mod_kernel,
        grid=(depth, n // tn),
        in_specs=[pl.BlockSpec((8, d), lambda l, j: (0, 0)),
                  pl.BlockSpec((None, d, tn), lambda l, j: (l, 0, j)),
                  pl.BlockSpec((None, 1, tn), lambda l, j: (l, 0, j))],
        out_specs=pl.BlockSpec((None, 8, tn), lambda l, j: (l, 0, j)),
        out_shape=jax.ShapeDtypeStruct((depth, 8, n), F32),
        compiler_params=_cparams("arbitrary", "arbitrary"),
        name="modulation",
    )(cond, w_mod, b_mod.reshape(depth, 1, n))


def _inproj_kernel(h_ref, sh_ref, sc_ref, w_ref, o_ref, xn_ref):
    @pl.when(pl.program_id(1) == 0)
    def _():
        x = _ln(h_ref[...])
        xn_ref[...] = (x * (1.0 + sc_ref[...]) + sh_ref[...]).astype(BF16)

    o_ref[...] = jnp.dot(xn_ref[...], w_ref[...], preferred_element_type=F32).astype(BF16)


def _inproj(h, shift, scale, w, tiles_per_batch, n_batch):
    t, d = h.shape
    n = w.shape[1]
    tn = n // 2
    mod_row = lambda i, j: (jnp.minimum(i // tiles_per_batch, n_batch), 0, 0)
    return pl.pallas_call(
        _inproj_kernel,
        grid=(t // ROW_TILE, n // tn),
        in_specs=[pl.BlockSpec((ROW_TILE, d), lambda i, j: (i, 0)),
                  pl.BlockSpec((None, 1, d), mod_row),
                  pl.BlockSpec((None, 1, d), mod_row),
                  pl.BlockSpec((d, tn), lambda i, j: (0, j))],
        out_specs=pl.BlockSpec((ROW_TILE, tn), lambda i, j: (i, j)),
        out_shape=jax.ShapeDtypeStruct((t, n), BF16),
        scratch_shapes=[pltpu.VMEM((ROW_TILE, d), BF16)],
        compiler_params=_cparams("arbitrary", "arbitrary"),
        name="inproj",
    )(h, shift, scale, w)


def _rope_kernel(q_ref, k_ref, v_ref, cos_ref, sin_ref, qo_ref, k0_ref, k1_ref, vo_ref, *, n_lat_tiles):
    is_lat = pl.program_id(0) < n_lat_tiles
    shape = q_ref.shape
    lane = lax.broadcasted_iota(jnp.int32, shape, 1)
    upper16 = (lane % 32) >= 16
    map1 = (lane % LANES) >= DA_HEAD_DIM
    cos = jnp.where(is_lat, cos_ref[...], 1.0)
    sin = jnp.where(is_lat, sin_ref[...], 0.0)

    def rope(x):
        partner = jnp.where(upper16, pltpu.roll(x, 16, 1), pltpu.roll(x, shape[1] - 16, 1))
        return x * cos + partner * sin

    q = rope(q_ref[...].astype(F32)) * (DA_HEAD_DIM ** -0.5)
    k = rope(k_ref[...].astype(F32))
    qo_ref[...] = q.astype(BF16)
    k0_ref[...] = jnp.where(map1, 0.0, k).astype(BF16)
    k1_ref[...] = jnp.where(map1, k, 0.0).astype(BF16)
    ones = jnp.ones((shape[0], LANES), BF16)
    v = v_ref[...]
    vo_ref[...] = jnp.concatenate(
        [piece for hd in range(DA_HEADS) for piece in (v[:, hd * LANES:(hd + 1) * LANES], ones)], axis=1)


def _rope_pack(p, cos_t, sin_t, n_batch, seq, ctx):
    t = p.shape[0]
    nl = seq // SUB_TILE
    nc = ctx // SUB_TILE
    n_lat_tiles = n_batch * nl

    def kv_map(i):
        j = i - n_lat_tiles
        b = jnp.where(i < n_lat_tiles, i // nl, j // nc)
        blk = jnp.where(i < n_lat_tiles, nc + i % nl, j % nc)
        return (b, blk, 0)

    tab_map = lambda i: (jnp.where(i < n_lat_tiles, i % nl, 0), 0)
    col = lambda c: (lambda i: (i, c // DA_QK))
    kv_shape = jax.ShapeDtypeStruct((n_batch, ctx + seq, DA_QK), BF16)
    return pl.pallas_call(
        functools.partial(_rope_kernel, n_lat_tiles=n_lat_tiles),
        grid=(t // SUB_TILE,),
        in_specs=[pl.BlockSpec((SUB_TILE, DA_QK), col(COL_Q)),
                  pl.BlockSpec((SUB_TILE, DA_QK), col(COL_K)),
                  pl.BlockSpec((SUB_TILE, DA_V), col(COL_V)),
                  pl.BlockSpec((SUB_TILE, DA_QK), tab_map),
                  pl.BlockSpec((SUB_TILE, DA_QK), tab_map)],
        out_specs=[pl.BlockSpec((SUB_TILE, DA_QK), lambda i: (i, 0)),
                   pl.BlockSpec((None, SUB_TILE, DA_QK), kv_map),
                   pl.BlockSpec((None, SUB_TILE, DA_QK), kv_map),
                   pl.BlockSpec((None, SUB_TILE, 2 * DA_V), kv_map)],
        out_shape=[jax.ShapeDtypeStruct((t, DA_QK), BF16), kv_shape, kv_shape,
                   jax.ShapeDtypeStruct((n_batch, ctx + seq, 2 * DA_V), BF16)],
        compiler_params=_cparams("arbitrary"),
        name="rope_pack",
    )(p, p, p, cos_t, sin_t)


def _rope_tables(seq):
    pos = jnp.arange(seq)
    row = (pos // GRID_W).astype(F32)[:, None]
    colp = (pos % GRID_W).astype(F32)[:, None]
    axis_dim = DA_HEAD_DIM // 2
    inv_freq = ROPE_BASE ** (-jnp.arange(0, axis_dim, 2, dtype=F32) / axis_dim)
    ang_r = row * inv_freq
    ang_c = colp * inv_freq
    ang = jnp.concatenate([ang_r, ang_r, ang_c, ang_c], -1)
    sign = jnp.concatenate([-jnp.ones((16,), F32), jnp.ones((16,), F32)] * 2)
    reps = DA_QK // DA_HEAD_DIM
    return jnp.tile(jnp.cos(ang), (1, reps)), jnp.tile(jnp.sin(ang) * sign, (1, reps))


def _attn_kernel(lam_ref, q_ref, k0_ref, k1_ref, v_ref, g_ref, o_ref, *scratch, head, n_main, tk, out_scale):
    q = q_ref[...]
    nt = (((1,), (1,)), ((), ()))
    k_refs = (k0_ref, k1_ref)
    s_refs = (scratch[0:2], scratch[2:4])
    m_refs, acc_refs, p_refs = scratch[4:6], scratch[6:8], scratch[8:10]

    opaque_zero = jnp.minimum(pl.program_id(2), 0)

    def keys(j, size):
        if size == head:
            return pl.ds(0, head)
        return pl.ds(pl.multiple_of(head + (j - 1) * tk, math.gcd(head, tk)), tk)

    def scores(j, buf, size=tk):
        for mp in range(2):
            s_refs[buf][mp][:, :size] = lax.dot_general(q, k_refs[mp][keys(j, size), :], nt,
                                                        preferred_element_type=F32)

    def absorb(j, buf, size=tk):
        vc = v_ref[keys(j, size), :]
        for mp in range(2):
            for r0 in range(0, q.shape[0], ATTN_ROW_BLOCK):
                rows = slice(r0, r0 + ATTN_ROW_BLOCK)
                s = s_refs[buf][mp][pl.ds(pl.multiple_of(r0 + opaque_zero, ATTN_ROW_BLOCK), ATTN_ROW_BLOCK),
                                    :size]
                m = m_refs[mp][rows, :]
                m_new = jnp.maximum(m, jnp.broadcast_to(jnp.max(s, -1, keepdims=True), m.shape))
                alpha = jnp.exp(m - m_new)
                p = jnp.exp(s - jnp.concatenate([m_new] * (size // LANES), axis=1))
                acc_refs[mp][rows, :] = jnp.concatenate([alpha, alpha], axis=1) * acc_refs[mp][rows, :]
                m_refs[mp][rows, :] = m_new
                p_refs[mp][rows, :size] = p.astype(BF16)
            acc_refs[mp][...] += jnp.dot(p_refs[mp][:, :size], vc, preferred_element_type=F32)

    for mp in range(2):
        m_refs[mp][...] = jnp.full(m_refs[mp].shape, -jnp.inf, F32)
        acc_refs[mp][...] = jnp.zeros(acc_refs[mp].shape, F32)
    scores(0, 0, head)
    if n_main:
        scores(1, 1)
    absorb(0, 0, head)
    if n_main:
        def body(i, carry):
            scores(2 * i + 2, 0)
            absorb(2 * i + 1, 1)
            scores(2 * i + 3, 1)
            absorb(2 * i + 2, 0)
            return carry

        lax.fori_loop(0, n_main // 2 - 1, body, 0)
        scores(n_main, 0)
        absorb(n_main - 1, 1)
        absorb(n_main, 0)
    a0, a1 = acc_refs[0][...], acc_refs[1][...]
    o = a0[:, :LANES] / a0[:, LANES:] - lam_ref[0] * (a1[:, :LANES] / a1[:, LANES:])
    o = o * lax.rsqrt(jnp.mean(o * o, -1, keepdims=True) + LN_EPS)
    o_ref[...] = (o * g_ref[...] * out_scale).astype(BF16)


def _attention(lam, q, k0, k1, v, subln_g, *, n_rows, q_row0, tq, kv_len, head, tk, out_scale):
    n_batch = k0.shape[0]
    per_batch = n_rows // n_batch // tq
    q0 = q_row0 // tq
    assert (kv_len - head) % (2 * tk) == 0, "the main key chunks are consumed in pairs"
    kv_spec = pl.BlockSpec((None, kv_len, LANES), lambda b, h, i: (b, 0, h))
    return pl.pallas_call(
        functools.partial(_attn_kernel, head=head, n_main=(kv_len - head) // tk, tk=tk, out_scale=out_scale),
        grid=(n_batch, DA_HEADS, per_batch),
        scratch_shapes=([pltpu.VMEM((tq, tk), F32)] * 4 + [pltpu.VMEM((tq, LANES), F32)] * 2
                        + [pltpu.VMEM((tq, 2 * LANES), F32)] * 2 + [pltpu.VMEM((tq, tk), BF16)] * 2),
        in_specs=[pl.BlockSpec(memory_space=pltpu.SMEM),
                  pl.BlockSpec((tq, LANES), lambda b, h, i: (q0 + b * per_batch + i, h)),
                  kv_spec, kv_spec,
                  pl.BlockSpec((None, kv_len, 2 * LANES), lambda b, h, i: (b, 0, h)),
                  pl.BlockSpec((1, LANES), lambda b, h, i: (0, 0))],
        out_specs=pl.BlockSpec((tq, LANES), lambda b, h, i: (b * per_batch + i, h)),
        out_shape=jax.ShapeDtypeStruct((n_rows, DA_V), BF16),
        compiler_params=_cparams("arbitrary", "arbitrary", "arbitrary"),
        name="diff_attention",
    )(lam, q, k0, k1, v, subln_g)


def _s5_in_kernel(x_ref, b_ref, u_ref, z_ref, stage_ref):
    @pl.when(pl.program_id(1) == 0)
    def _():
        x = x_ref[...].astype(F32)
        for half in range(S5_WIDTH // LANES):
            stage_ref[half] = x[:, half * LANES:(half + 1) * LANES]
        for s in range(S5_CHUNK):
            for half in range(S5_WIDTH // LANES):
                col = s * S5_WIDTH + half * LANES
                u_ref[:, col:col + LANES] = (
                    stage_ref[half, pl.ds(s, u_ref.shape[0], stride=S5_CHUNK), :].astype(BF16))

    z_ref[...] = jnp.dot(u_ref[...], b_ref[...], preferred_element_type=F32)


def _s5_scan_kernel(z_ref, a1_ref, a2_ref, p_ref, *, n_batch, lat_chunks, ctx_chunks):
    reverse = pl.program_id(0) == 1
    a1 = a1_ref[...]
    a2 = a2_ref[...]
    width = a1.shape[1]

    def swap_halves(s):
        return jnp.concatenate(
            [pltpu.roll(s[:, j * LANES:(j + 1) * LANES], LANES // 2, 1) for j in range(width // LANES)], axis=1)

    def run(base, count, carry):
        def body(i, st):
            s, ssw = st
            row = base + jnp.where(reverse, count - 1 - i, i)
            z = z_ref[pl.ds(row, 1), :]
            p_ref[pl.ds(row, 1), :] = s
            return a1 * s + a2 * ssw + z, a1 * ssw - a2 * s + swap_halves(z)
        return lax.fori_loop(0, count, body, carry, unroll=4)

    zero = jnp.zeros(a1.shape, F32)
    for b in range(n_batch):
        st = run(n_batch * lat_chunks + b * ctx_chunks, ctx_chunks, (zero, zero))
        run(b * lat_chunks, lat_chunks, st)


def _s5_out_kernel(u_ref, p_ref, t_ref, ct_ref, y_ref, stage_ref):
    y = jnp.dot(u_ref[...], t_ref[...], preferred_element_type=F32)
    y = y + lax.dot_general(p_ref[...].astype(BF16), ct_ref[...], (((1,), (1,)), ((), ())),
                            preferred_element_type=F32)
    y = _gelu(y)
    per_step = y.shape[1] // S5_WIDTH
    first = pl.program_id(1) * per_step
    halves = S5_WIDTH // LANES
    for k in range(per_step):
        for half in range(halves):
            col = k * S5_WIDTH + half * LANES
            stage_ref[half, pl.ds(first + k, y.shape[0], stride=S5_CHUNK), :] = y[:, col:col + LANES]

    @pl.when(pl.program_id(1) == pl.num_programs(1) - 1)
    def _():
        for half in range(halves):
            y_ref[:, half * LANES:(half + 1) * LANES] = stage_ref[half]


def _s5_mixer(p, mats, *, n_batch, lat_chunks, ctx_chunks):
    t_all = p.shape[0]
    r = t_all // S5_CHUNK
    w = S5_CHUNK * S5_WIDTH
    tr = r // 4
    state_w = mats["b"].shape[1]
    tn = 1024
    u, z = pl.pallas_call(
        _s5_in_kernel,
        grid=(r // tr, state_w // tn),
        in_specs=[pl.BlockSpec((tr * S5_CHUNK, S5_WIDTH), lambda i, j: (i, COL_S5 // S5_WIDTH)),
                  pl.BlockSpec((w, tn), lambda i, j: (0, j))],
        out_specs=[pl.BlockSpec((tr, w), lambda i, j: (i, 0)),
                   pl.BlockSpec((tr, tn), lambda i, j: (i, j))],
        out_shape=[jax.ShapeDtypeStruct((r, w), BF16), jax.ShapeDtypeStruct((r, state_w), F32)],
        scratch_shapes=[pltpu.VMEM((S5_WIDTH // LANES, tr * S5_CHUNK, LANES), F32)],
        compiler_params=_cparams("arbitrary", "arbitrary"),
        name="s5_in",
    )(p, mats["b"])
    half = state_w // 2
    coef = pl.BlockSpec((None, 1, half), lambda d: (d, 0, 0))
    prev = pl.pallas_call(
        functools.partial(_s5_scan_kernel, n_batch=n_batch, lat_chunks=lat_chunks, ctx_chunks=ctx_chunks),
        grid=(2,),
        in_specs=[pl.BlockSpec((r, half), lambda d: (0, d)), coef, coef],
        out_specs=pl.BlockSpec((r, half), lambda d: (0, d)),
        out_shape=jax.ShapeDtypeStruct((r, state_w), F32),
        compiler_params=_cparams("arbitrary"),
        name="s5_scan",
    )(z, mats["a1"], mats["a2"])
    tn = 512
    return pl.pallas_call(
        _s5_out_kernel,
        grid=(r // tr, w // tn),
        in_specs=[pl.BlockSpec((tr, w), lambda i, j: (i, 0)),
                  pl.BlockSpec((tr, state_w), lambda i, j: (i, 0)),
                  pl.BlockSpec((w, tn), lambda i, j: (0, j)),
                  pl.BlockSpec((tn, state_w), lambda i, j: (j, 0))],
        out_specs=pl.BlockSpec((tr * S5_CHUNK, S5_WIDTH), lambda i, j: (i, 0)),
        out_shape=jax.ShapeDtypeStruct((t_all, S5_WIDTH), F32),
        scratch_shapes=[pltpu.VMEM((S5_WIDTH // LANES, tr * S5_CHUNK, LANES), F32)],
        compiler_params=_cparams("arbitrary", "arbitrary"),
        name="s5_out",
    )(u, prev, mats["t"], mats["ct"])


def _s5_matrices(lam_re, lam_im, log_step, b_re, b_im, c_re, c_im, d_skip):
    n = S5_CHUNK
    dt = jnp.exp(log_step.astype(F32))[..., None]
    lr, li = lam_re.astype(F32), lam_im.astype(F32)
    mag = jnp.exp(lr * dt)
    a_re, a_im = mag * jnp.cos(li * dt), mag * jnp.sin(li * dt)
    den = lr * lr + li * li
    n_re = a_re - 1.0
    z_re = (n_re * lr + a_im * li) / den
    z_im = (a_im * lr - n_re * li) / den
    br, bi = b_re.astype(F32), b_im.astype(F32)
    bb_re = z_re[..., None] * br - z_im[..., None] * bi
    bb_im = z_re[..., None] * bi + z_im[..., None] * br
    j = jnp.arange(n + 1, dtype=F32)[:, None, None, None]
    pmag = jnp.exp(lr * dt * j)
    pw_re, pw_im = pmag * jnp.cos(li * dt * j), pmag * jnp.sin(li * dt * j)
    cr, ci = c_re.astype(F32), c_im.astype(F32)
    g, p, c = S5_GROUPS, S5_STATE, S5_GROUP

    def cmul(xr, xi, yr, yi):
        return xr * yr - xi * yi, xr * yi + xi * yr

    def in_mat(direction, powers):
        er = pw_re[powers, direction][:, :, :, None]
        ei = pw_im[powers, direction][:, :, :, None]
        xr, xi = cmul(er, ei, bb_re[direction][None], bb_im[direction][None])
        m = jnp.concatenate([xr, xi], axis=2)
        return m.transpose(1, 0, 3, 2).reshape(g, n * c, 2 * p)

    def out_mat(direction, powers):
        er = pw_re[powers, direction][:, :, None, :]
        ei = pw_im[powers, direction][:, :, None, :]
        wr, wi = cmul(cr[direction][None], ci[direction][None], er, ei)
        m = jnp.concatenate([wr, -wi], axis=3)
        return m.transpose(1, 3, 0, 2).reshape(g, 2 * p, n * c)

    def toeplitz(direction):
        er = pw_re[:n, direction][:, :, None, :, None]
        ei = pw_im[:n, direction][:, :, None, :, None]
        wr, wi = cmul(cr[direction][None, :, :, :, None], ci[direction][None, :, :, :, None], er, ei)
        k = jnp.sum(wr * bb_re[direction][None, :, None] - wi * bb_im[direction][None, :, None], axis=3)
        return k

    up, down = slice(0, n), slice(n - 1, None, -1)
    up1, down1 = slice(1, n + 1), slice(n, 0, -1)
    kf, kb = toeplitz(0), toeplitz(1)
    skip = d_skip.astype(F32).reshape(g, c)
    k0 = kf[0] + kb[0] + jnp.eye(c, dtype=F32)[None] * skip[:, :, None]
    by_lag = jnp.concatenate([kb[:0:-1], k0[None], kf[1:]], axis=0)
    eye_g = jnp.eye(g, dtype=F32)
    lag_blocks = jnp.einsum("mgca,gh->mgahc", by_lag, eye_g).reshape(2 * n - 1, g * c, g * c)
    steps = jnp.arange(n)
    t_dense = lag_blocks[steps[None, :] - steps[:, None] + n - 1]
    t_dense = t_dense.transpose(0, 2, 1, 3).reshape(n * g * c, n * g * c)

    def coef(direction):
        ar, ai = pw_re[n, direction], pw_im[n, direction]
        return (jnp.concatenate([ar, ar], -1).reshape(-1), jnp.concatenate([-ai, ai], -1).reshape(-1))

    def dense_in(m):
        m = m.reshape(g, n, c, -1)
        return jnp.einsum("gscq,gh->sgchq", m, eye_g).reshape(n * g * c, -1)

    def dense_out_t(m):
        q = m.shape[1]
        return jnp.einsum("gqtc,gh->thcgq", m.reshape(g, q, n, c), eye_g).reshape(n * g * c, g * q)

    a1f, a2f = coef(0)
    a1b, a2b = coef(1)
    return {
        "b": jnp.concatenate([dense_in(in_mat(0, down)), dense_in(in_mat(1, up))], 1).astype(BF16),
        "ct": jnp.concatenate([dense_out_t(out_mat(0, up1)), dense_out_t(out_mat(1, down1))], 1).astype(BF16),
        "t": t_dense.astype(BF16),
        "a1": jnp.stack([a1f, a1b])[:, None, :], "a2": jnp.stack([a2f, a2b])[:, None, :],
    }


def _merge_kernel(ya_ref, zu_ref, zv_ref, yc_ref, ga_ref, gb_ref, gc_ref, h_ref,
                  g1_ref, sh2_ref, sc2_ref, lng_ref, lnb_ref,
                  wglu_ref, wgm_ref, wda_ref, wout_ref, ws_ref, bs_ref, wr_ref,
                  h1_ref, f_ref, lg_ref, *, alpha):
    f32 = lambda ref: ref[...].astype(F32)
    glu = jnp.dot(ya_ref[...].astype(BF16), wglu_ref[...], preferred_element_type=F32)
    branch_a = glu[:, :D_MODEL] * _sigmoid_t(glu[:, D_MODEL:])

    u = _gelu(f32(zu_ref))
    v = _ln(_gelu(f32(zv_ref))).astype(BF16)
    head = lax.broadcasted_iota(jnp.int32, (GM_CHUNK, GM_WIDTH), 1) // (GM_WIDTH // GM_HEADS)
    parts = []
    for ck in range(v.shape[0] // GM_CHUNK):
        vc = v[ck * GM_CHUNK:(ck + 1) * GM_CHUNK]
        s = bs_ref[...]
        for hd in range(GM_HEADS):
            s = s + jnp.dot(ws_ref[hd], jnp.where(head == hd, vc, jnp.zeros_like(vc)),
                            preferred_element_type=F32)
        parts.append(s)
    yb = (u * jnp.concatenate(parts, axis=0)).astype(BF16)

    m = _sigmoid_t(f32(ga_ref)) * branch_a
    m = m + _sigmoid_t(f32(gb_ref)) * jnp.dot(yb, wgm_ref[...], preferred_element_type=F32)
    m = m + _sigmoid_t(f32(gc_ref)) * jnp.dot(yc_ref[...], wda_ref[...], preferred_element_type=F32)
    mix = jnp.dot(m.astype(BF16), wout_ref[...], preferred_element_type=F32)

    h1 = _ln(alpha * h_ref[...] + g1_ref[...] * mix) * lng_ref[...] + lnb_ref[...]
    h1_ref[...] = h1
    f = _ln(h1) * (1.0 + sc2_ref[...]) + sh2_ref[...]
    f_ref[...] = f.astype(BF16)
    lg_ref[...] = jnp.dot(f, wr_ref[...], preferred_element_type=F32, precision=HIGHEST)


def _merge(ya, p, yc, h, mods, ln_g, ln_b, w, *, n_rows, tiles_per_batch, n_batch, alpha):
    d = D_MODEL
    tm = MERGE_TILE
    row = lambda i: (i, 0)
    mod_row = lambda i: (jnp.minimum(i // tiles_per_batch, n_batch), 0, 0)
    pcol = lambda c, width: pl.BlockSpec((tm, width), lambda i: (i, c // width))
    full = lambda a: pl.BlockSpec(a.shape, lambda i: (0,) * a.ndim)
    mod_spec = pl.BlockSpec((None, 1, d), mod_row)
    vec = lambda a: a.reshape(1, d)
    weights = (w["glu"], w["gm"], w["da"], w["out"], w["gm_ws"], w["gm_bs"], w["router"])
    return pl.pallas_call(
        functools.partial(_merge_kernel, alpha=alpha),
        grid=(n_rows // tm,),
        in_specs=[pl.BlockSpec((tm, S5_WIDTH), row),
                  pcol(COL_ZU, GM_WIDTH), pcol(COL_ZV, GM_WIDTH),
                  pl.BlockSpec((tm, DA_V), row),
                  pcol(COL_GATE, d), pcol(COL_GATE + d, d), pcol(COL_GATE + 2 * d, d),
                  pl.BlockSpec((tm, d), row),
                  mod_spec, mod_spec, mod_spec,
                  pl.BlockSpec((1, d), lambda i: (0, 0)), pl.BlockSpec((1, d), lambda i: (0, 0))]
                 + [full(a) for a in weights],
        out_specs=[pl.BlockSpec((tm, d), row), pl.BlockSpec((tm, d), row), pl.BlockSpec((tm, LANES), row)],
        out_shape=[jax.ShapeDtypeStruct((n_rows, d), F32), jax.ShapeDtypeStruct((n_rows, d), BF16),
                   jax.ShapeDtypeStruct((n_rows, LANES), F32)],
        compiler_params=_cparams("arbitrary"),
        name="merge",
    )(ya, p, p, yc, p, p, p, h, mods["g1"], mods["sh2"], mods["sc2"], vec(ln_g), vec(ln_b), *weights)


def _router_kernel(lg_ref, b_ref, before_ref, lrow_ref, w_ref, lrow_t_ref, units_ref):
    tm = lg_ref.shape[0]
    per_group = N_EXPERTS // N_GROUPS
    neg = -jnp.inf
    logits = lg_ref[...].T[:N_EXPERTS]
    scores = _sigmoid(logits).reshape(N_GROUPS, per_group, tm)
    sel = scores + b_ref[...].reshape(N_GROUPS, per_group, 1)

    in_group = lax.broadcasted_iota(jnp.int32, sel.shape, 1)
    top1 = jnp.max(sel, axis=1, keepdims=True)
    first = jnp.min(jnp.where(sel == top1, in_group, per_group), axis=1, keepdims=True)
    top2 = jnp.max(jnp.where(in_group == first, neg, sel), axis=1, keepdims=True)
    gscore = top1 + top2

    gidx = lax.broadcasted_iota(jnp.int32, gscore.shape, 0)
    gsel = jnp.zeros(gscore.shape, jnp.bool_)
    for _ in range(TOPK_GROUPS):
        best = jnp.max(gscore, axis=0, keepdims=True)
        hit = gidx == jnp.min(jnp.where(gscore == best, gidx, N_GROUPS), axis=0, keepdims=True)
        gsel = gsel | hit
        gscore = jnp.where(hit, neg, gscore)

    eidx = lax.broadcasted_iota(jnp.int32, sel.shape, 0) * per_group + in_group
    cand = jnp.where(gsel, sel, neg)
    chosen = jnp.zeros(sel.shape, jnp.bool_)
    hits = []
    for _ in range(TOP_K):
        best = jnp.max(jnp.max(cand, axis=1, keepdims=True), axis=0, keepdims=True)
        at = jnp.where(cand == best, eidx, N_EXPERTS)
        hit = eidx == jnp.min(jnp.min(at, axis=1, keepdims=True), axis=0, keepdims=True)
        hits.append(hit)
        chosen = chosen | hit
        cand = jnp.where(hit, neg, cand)

    w = jnp.where(chosen, scores, 0.0)
    total = jnp.sum(jnp.sum(w, axis=1, keepdims=True), axis=0, keepdims=True)
    w = w / total * ROUTED_SCALE

    onehot = jnp.where(chosen, 1.0, 0.0).reshape(N_EXPERTS, tm).astype(BF16)
    rank = jnp.dot(onehot, before_ref[...], preferred_element_type=F32)
    count = jnp.dot(onehot, jnp.ones((tm, LANES), BF16), preferred_element_type=F32)
    units = jnp.floor((count + (UNIT - 1)) * (1.0 / UNIT))
    ei = lax.broadcasted_iota(jnp.int32, (N_EXPERTS, N_EXPERTS), 0)
    ej = lax.broadcasted_iota(jnp.int32, (N_EXPERTS, N_EXPERTS), 1)
    first_unit = jnp.dot(jnp.where(ej < ei, 1.0, 0.0).astype(BF16), units.astype(BF16),
                         preferred_element_type=F32)
    base = jnp.concatenate([first_unit * UNIT] * (tm // LANES), axis=1)
    pos = (base + rank).reshape(sel.shape)

    pick = lambda hit, val: jnp.sum(jnp.sum(jnp.where(hit, val, 0.0), axis=1, keepdims=True), axis=0)
    pad = jnp.zeros((LANES - TOP_K, tm), F32)
    lrow_t = jnp.concatenate([pick(hit, pos) for hit in hits] + [pad], axis=0)
    w_t = jnp.concatenate([pick(hit, w) for hit in hits] + [pad], axis=0)
    lrow_t_ref[...] = lrow_t[:TOP_K].astype(jnp.int32)
    lrow_ref[...] = lrow_t.T.astype(jnp.int32)
    w_ref[...] = w_t.T
    units_ref[...] = units.astype(jnp.int32)


def _route(logits, b_router):
    t = logits.shape[0]
    tm = SUB_TILE
    n_sub = t // tm
    before = jnp.triu(jnp.ones((tm, tm), F32), 1).astype(BF16)
    tok = pl.BlockSpec((tm, LANES), lambda i: (i, 0))
    pick = pl.BlockSpec((TOP_K, tm), lambda i: (0, i))
    return pl.pallas_call(
        _router_kernel,
        grid=(n_sub,),
        in_specs=[tok, pl.BlockSpec((N_EXPERTS, 1), lambda i: (0, 0)), pl.BlockSpec((tm, tm), lambda i: (0, 0))],
        out_specs=[tok, tok, pick, pl.BlockSpec((N_EXPERTS, LANES), lambda i: (i, 0))],
        out_shape=[jax.ShapeDtypeStruct((t, LANES), jnp.int32), jax.ShapeDtypeStruct((t, LANES), F32),
                   jax.ShapeDtypeStruct((TOP_K, t), jnp.int32),
                   jax.ShapeDtypeStruct((n_sub * N_EXPERTS, LANES), jnp.int32)],
        compiler_params=_cparams("arbitrary"),
        name="router",
    )(logits, b_router.astype(F32).reshape(N_EXPERTS, 1), before)


def _moe_layout(units, n_blocks):
    per_expert = jnp.sum(units, axis=0)
    padded = (per_expert + FFN_UNITS - 1) // FFN_UNITS * FFN_UNITS
    ends = jnp.cumsum(padded)
    starts = ends - padded
    goff = starts[None, :] + jnp.cumsum(units, axis=0) - units
    local_end = jnp.cumsum(units, axis=1)
    lu = jnp.arange(LOCAL_UNITS, dtype=jnp.int32)
    owner = jnp.minimum(jnp.sum(local_end[:, None, :] <= lu[None, :, None], axis=2), N_EXPERTS - 1)
    is_owner = owner[:, :, None] == jnp.arange(N_EXPERTS, dtype=jnp.int32)[None, None, :]
    shift = goff - (local_end - units)
    gmap = lu[None, :] + jnp.sum(jnp.where(is_owner, shift[:, None, :], 0), axis=2)
    n_used = ends[-1] // FFN_UNITS
    blk = jnp.arange(n_blocks, dtype=jnp.int32)
    blk = jnp.minimum(blk, n_used - 1)
    first = blk * FFN_UNITS
    expert = jnp.minimum(jnp.sum(ends[None, :] <= first[:, None], axis=1), N_EXPERTS - 1)
    of_block = expert[:, None] == jnp.arange(N_EXPERTS, dtype=jnp.int32)[None, :]
    range_end = jnp.sum(jnp.where(of_block, (starts + per_expert)[None, :], 0), axis=1)
    valid = jnp.clip(range_end - first, 0, FFN_UNITS)
    i32 = lambda a: a.astype(jnp.int32)
    return i32(gmap.reshape(-1)), i32(local_end[:, -1]), i32(expert), i32(valid), i32(n_used.reshape(1))


def _unit_rows(ref, unit):
    return ref.at[pl.ds(pl.multiple_of(unit * UNIT, UNIT), UNIT), :]


def _for_each_unit(gmap_ref, count_ref, tile, fn):
    def body(lu, carry):
        fn(lu, gmap_ref[tile * LOCAL_UNITS + lu])
        return carry

    lax.fori_loop(0, count_ref[tile], body, 0)


def _dispatch_kernel(gmap_ref, count_ref, f_ref, lrow_t_ref, xs_hbm, buf_ref, sem_ref, *onehot_refs):
    s = pl.program_id(0)
    last = pl.num_programs(0) - 1
    slot = s % 2

    def copy(slot, lu, gu):
        return pltpu.make_async_copy(_unit_rows(buf_ref.at[slot], lu), _unit_rows(xs_hbm, gu), sem_ref.at[slot])

    def drain(step):
        def body(i, carry):
            copy(step % 2, 0, 0).wait()
            return carry
        lax.fori_loop(0, count_ref[step], body, 0)

    @pl.when(s >= 2)
    def _():
        drain(s - 2)

    lt = lrow_t_ref[...]
    f = f_ref[...]
    chunk = MOE_CHUNK

    def sort_chunk(r0, onehot_ref):
        for p0 in range(0, chunk, ONEHOT_BLOCK):
            row = lax.broadcasted_iota(jnp.int32, (ONEHOT_BLOCK, SUB_TILE), 0) + (r0 + p0)
            hit = row == lt[0:1, :]
            for k in range(1, TOP_K):
                hit = hit | (row == lt[k:k + 1, :])
            onehot_ref[p0:p0 + ONEHOT_BLOCK, :] = jnp.where(hit, 1.0, 0.0).astype(BF16)
        buf_ref[slot, r0:r0 + chunk, :] = jnp.dot(onehot_ref[...], f, preferred_element_type=F32).astype(BF16)

    for i, r0 in enumerate(range(0, LOCAL_ROWS, chunk)):
        if r0 + chunk <= SUB_TILE * TOP_K:
            sort_chunk(r0, onehot_refs[i])
        else:
            pl.when(r0 < count_ref[s] * UNIT)(functools.partial(sort_chunk, r0, onehot_refs[i]))

    _for_each_unit(gmap_ref, count_ref, s, lambda lu, gu: copy(slot, lu, gu).start())

    @pl.when(s == last)
    def _():
        drain(s)

        @pl.when(s >= 1)
        def _():
            drain(s - 1)


def _dispatch(f, lrow_t, gmap, count, n_rows_out):
    t, d = f.shape
    grid_spec = pltpu.PrefetchScalarGridSpec(
        num_scalar_prefetch=2,
        grid=(t // SUB_TILE,),
        in_specs=[pl.BlockSpec((SUB_TILE, d), lambda s, u, g: (s, 0)),
                  pl.BlockSpec((TOP_K, SUB_TILE), lambda s, u, g: (0, s))],
        out_specs=pl.BlockSpec(memory_space=pl.ANY),
        scratch_shapes=([pltpu.VMEM((2, LOCAL_ROWS, d), BF16), pltpu.SemaphoreType.DMA((2,))]
                        + [pltpu.VMEM((MOE_CHUNK, SUB_TILE), BF16)] * (LOCAL_ROWS // MOE_CHUNK)))
    return pl.pallas_call(
        _dispatch_kernel,
        grid_spec=grid_spec,
        out_shape=jax.ShapeDtypeStruct((n_rows_out, d), BF16),
        compiler_params=_cparams("arbitrary"),
        name="moe_dispatch",
    )(gmap, count, f, lrow_t)


def _ffn_kernel(expert_ref, valid_ref, nused_ref, x_ref, wg_ref, wu_ref, wd_ref, y_ref, wgu_s, wd_s):
    b = pl.program_id(0)

    @pl.when(b < nused_ref[0])
    def _():
        @pl.when((b == 0) | (expert_ref[b] != expert_ref[jnp.maximum(b - 1, 0)]))
        def _():
            wgu_s[:, :EXPERT_HIDDEN] = wg_ref[...].astype(BF16)
            wgu_s[:, EXPERT_HIDDEN:] = wu_ref[...].astype(BF16)
            wd_s[...] = wd_ref[...].astype(BF16)

        rows = lax.broadcasted_iota(jnp.int32, (FFN_ROWS, 1), 0)
        x = x_ref[...]
        x = jnp.where(rows < valid_ref[b] * UNIT, x, jnp.zeros_like(x))
        hgu = jnp.dot(x, wgu_s[...], preferred_element_type=F32)
        hid = hgu[:, :EXPERT_HIDDEN] * _sigmoid_t(hgu[:, :EXPERT_HIDDEN]) * hgu[:, EXPERT_HIDDEN:]
        y_ref[...] = jnp.dot(hid.astype(BF16), wd_s[...], preferred_element_type=F32).astype(BF16)


def _expert_ffn(xs, w_gate, w_up, w_down, layer, expert, valid, n_used):
    n_rows, d = xs.shape
    row_blk = lambda b, e, v, n: (jnp.minimum(b, n[0] - 1), 0)
    of_expert = lambda b, e, v, n: (layer, e[b], 0, 0)
    grid_spec = pltpu.PrefetchScalarGridSpec(
        num_scalar_prefetch=3,
        grid=(n_rows // FFN_ROWS,),
        in_specs=[pl.BlockSpec((FFN_ROWS, d), row_blk),
                  pl.BlockSpec((None, None, d, EXPERT_HIDDEN), of_expert),
                  pl.BlockSpec((None, None, d, EXPERT_HIDDEN), of_expert),
                  pl.BlockSpec((None, None, EXPERT_HIDDEN, d), of_expert)],
        out_specs=pl.BlockSpec((FFN_ROWS, d), row_blk),
        scratch_shapes=[pltpu.VMEM((d, 2 * EXPERT_HIDDEN), BF16), pltpu.VMEM((EXPERT_HIDDEN, d), BF16)])
    return pl.pallas_call(
        _ffn_kernel,
        grid_spec=grid_spec,
        out_shape=jax.ShapeDtypeStruct((n_rows, d), BF16),
        compiler_params=_cparams("arbitrary"),
        name="moe_ffn",
    )(expert, valid, n_used, xs, w_gate, w_up, w_down)


def _combine_kernel(gmap_ref, count_ref, ys_hbm, lrow_ref, w_ref, f_ref, sgu_ref, sdn_ref, h1_ref, g2_ref,
                    lng_ref, lnb_ref, o_ref, buf_ref, sem_ref, acc_ref, lrow_rep_ref, w_rep_ref, *, alpha):
    s = pl.program_id(0)
    n_tiles = pl.num_programs(0)
    slot = s % 2

    def copy(slot, lu, gu):
        return pltpu.make_async_copy(_unit_rows(ys_hbm, gu), _unit_rows(buf_ref.at[slot], lu), sem_ref.at[slot])

    def fetch(tile, slot):
        _for_each_unit(gmap_ref, count_ref, tile, lambda lu, gu: copy(slot, lu, gu).start())

    @pl.when(s == 0)
    def _():
        buf_ref[...] = jnp.zeros(buf_ref.shape, BF16)
        fetch(0, 0)

    @pl.when(s + 1 < n_tiles)
    def _():
        fetch(s + 1, 1 - slot)

    n_units = count_ref[s]

    def wait_one(i, carry):
        copy(slot, 0, 0).wait()
        return carry

    lax.fori_loop(0, n_units, wait_one, 0)

    hgu = jnp.dot(f_ref[...], sgu_ref[...], preferred_element_type=F32)
    hid = hgu[:, :EXPERT_HIDDEN] * _sigmoid_t(hgu[:, :EXPERT_HIDDEN]) * hgu[:, EXPERT_HIDDEN:]
    acc_ref[...] = jnp.dot(hid.astype(BF16), sdn_ref[...], preferred_element_type=F32)

    lrow = lrow_ref[...]
    w = w_ref[...]
    for k in range(TOP_K):
        lrow_rep_ref[k] = jnp.broadcast_to(lrow[:, k:k + 1], (SUB_TILE, LANES))
        w_rep_ref[k] = jnp.broadcast_to(w[:, k:k + 1], (SUB_TILE, LANES))
    chunk = MOE_CHUNK

    def unsort_chunk(r0):
        pieces = []
        for c0 in range(r0, r0 + chunk, LANES):
            col = lax.broadcasted_iota(jnp.int32, (SUB_TILE, LANES), 1) + c0
            pw = jnp.zeros((SUB_TILE, LANES), F32)
            for k in range(TOP_K):
                pw = jnp.where(lrow_rep_ref[k] == col, w_rep_ref[k], pw)
            pieces.append(pw.astype(BF16))
        return jnp.dot(jnp.concatenate(pieces, axis=1), buf_ref[slot, r0:r0 + chunk, :],
                       preferred_element_type=F32)

    def add_chunk(r0):
        acc_ref[...] += unsort_chunk(r0)

    always = [r0 for r0 in range(0, LOCAL_ROWS, chunk) if r0 + chunk <= SUB_TILE * TOP_K]
    acc_ref[...] += functools.reduce(lambda a, b: a + b, [unsort_chunk(r0) for r0 in always])
    for r0 in range(len(always) * chunk, LOCAL_ROWS, chunk):
        pl.when(r0 < n_units * UNIT)(functools.partial(add_chunk, r0))

    o_ref[...] = _ln(alpha * h1_ref[...] + g2_ref[...] * acc_ref[...]) * lng_ref[...] + lnb_ref[...]


def _combine(ys, lrow, w, f, sgu, sdn, h1, g2, ln_g, ln_b, gmap, count, *, tiles_per_batch, n_batch, alpha):
    t, d = f.shape
    tm = SUB_TILE
    row = lambda s, u, g: (s, 0)
    fixed = lambda s, u, g: (0, 0)
    mod_row = lambda s, u, g: (jnp.minimum(s // tiles_per_batch, n_batch), 0, 0)
    grid_spec = pltpu.PrefetchScalarGridSpec(
        num_scalar_prefetch=2,
        grid=(t // tm,),
        in_specs=[pl.BlockSpec(memory_space=pl.ANY),
                  pl.BlockSpec((tm, LANES), row), pl.BlockSpec((tm, LANES), row), pl.BlockSpec((tm, d), row),
                  pl.BlockSpec(sgu.shape, fixed), pl.BlockSpec(sdn.shape, fixed),
                  pl.BlockSpec((tm, d), row), pl.BlockSpec((None, 1, d), mod_row),
                  pl.BlockSpec((1, d), fixed), pl.BlockSpec((1, d), fixed)],
        out_specs=pl.BlockSpec((tm, d), row),
        scratch_shapes=[pltpu.VMEM((2, LOCAL_ROWS, d), BF16), pltpu.SemaphoreType.DMA((2,)),
                        pltpu.VMEM((tm, d), F32), pltpu.VMEM((TOP_K, tm, LANES), jnp.int32),
                        pltpu.VMEM((TOP_K, tm, LANES), F32)])
    return pl.pallas_call(
        functools.partial(_combine_kernel, alpha=alpha),
        grid_spec=grid_spec,
        out_shape=jax.ShapeDtypeStruct((t, d), F32),
        compiler_params=_cparams("arbitrary"),
        name="moe_combine",
    )(gmap, count, ys, lrow, w, f, sgu, sdn, h1, g2, ln_g.reshape(1, d), ln_b.reshape(1, d))


def _forward(x, c, ctx, c_ctx, w_mod, b_mod, w_in, s5_lam_re, s5_lam_im, s5_log_step, s5_b_re, s5_b_im,
             s5_c_re, s5_c_im, s5_d, gm_w_s, gm_b_s, da_lam, da_subln_g, w_glu_val, w_glu_gate,
             w_proj_gm, w_proj_da, w_out, ln1_g, ln1_b, ln2_g, ln2_b, w_router, b_router,
             w_exp_gate, w_exp_up, w_exp_down, w_sh_gate, w_sh_up, w_sh_down):
    n_batch, seq, d = x.shape
    ctx_len = ctx.shape[1]
    depth = w_mod.shape[0]
    n_lat = n_batch * seq
    alpha = (2 * depth) ** 0.25
    tiles_per_batch = seq // ROW_TILE
    sub_per_batch = seq // SUB_TILE
    assert d == D_MODEL and seq % ROW_TILE == 0 and (n_batch * ctx_len) % ROW_TILE == 0
    assert ctx_len % SUB_TILE == 0 and n_batch == 4

    cond = jnp.concatenate([c, c_ctx[None], jnp.zeros((8 - n_batch - 1, d), F32)], axis=0)
    mod = _modulation(cond, w_mod, b_mod)
    cos_t, sin_t = _rope_tables(seq)
    h = jnp.concatenate([x.reshape(n_lat, d), ctx.reshape(n_batch * ctx_len, d)], axis=0)

    for l in range(depth):
        last = l == depth - 1
        lam_init = 0.8 - 0.6 * math.exp(-0.3 * l)
        names = ("sh1", "sc1", "g1", "sh2", "sc2", "g2")
        mods = {k: mod[l, :, i * d:(i + 1) * d].reshape(8, 1, d) for i, k in enumerate(names)}

        wi = w_in[l]
        a_w, z_w, q_w, k_w, v_w, g_w = jnp.split(wi, (256, 768, 1280, 1792, 2304), axis=1)
        w_in_l = jnp.concatenate([g_w, q_w, k_w, v_w, a_w, z_w], axis=1).astype(BF16)
        p = _inproj(h, mods["sh1"], mods["sc1"], w_in_l, tiles_per_batch, n_batch)

        lf = da_lam[l].astype(F32)
        lam = (jnp.exp(jnp.sum(lf[0] * lf[1])) - jnp.exp(jnp.sum(lf[2] * lf[3])) + lam_init).reshape(1)
        q, k0, k1, v = _rope_pack(p, cos_t, sin_t, n_batch, seq, ctx_len)
        subln = da_subln_g[l].astype(F32).reshape(1, LANES)
        attn = functools.partial(_attention, lam, q, k0, k1, v, subln, out_scale=1.0 - lam_init)
        yc = attn(n_rows=n_lat, q_row0=0, tq=512, kv_len=ctx_len + seq, head=ctx_len, tk=2048)
        if not last:
            yc_ctx = attn(n_rows=n_batch * ctx_len, q_row0=n_lat, tq=ctx_len, kv_len=ctx_len,
                          head=ctx_len, tk=ctx_len)
            yc = jnp.concatenate([yc, yc_ctx], axis=0)

        mats = _s5_matrices(s5_lam_re[l], s5_lam_im[l], s5_log_step[l], s5_b_re[l], s5_b_im[l],
                            s5_c_re[l], s5_c_im[l], s5_d[l])
        ya = _s5_mixer(p, mats, n_batch=n_batch, lat_chunks=seq // S5_CHUNK, ctx_chunks=ctx_len // S5_CHUNK)

        n_rows = n_lat if last else h.shape[0]
        router_w = jnp.concatenate([w_router[l].astype(F32), jnp.zeros((d, LANES - N_EXPERTS), F32)], axis=1)
        merge_w = {
            "glu": jnp.concatenate([w_glu_val[l], w_glu_gate[l]], axis=1).astype(BF16),
            "gm": w_proj_gm[l].astype(BF16), "da": w_proj_da[l].astype(BF16), "out": w_out[l].astype(BF16),
            "gm_ws": gm_w_s[l].astype(BF16),
            "gm_bs": jnp.repeat(gm_b_s[l].astype(F32).T, GM_WIDTH // GM_HEADS, axis=1),
            "router": router_w,
        }
        h1, f, logits = _merge(ya, p, yc, h, mods, ln1_g[l], ln1_b[l], merge_w, n_rows=n_rows,
                               tiles_per_batch=seq // MERGE_TILE, n_batch=n_batch, alpha=alpha)

        lrow, w_tok, lrow_t, units_lanes = _route(logits, b_router[l])
        n_tiles = n_rows // SUB_TILE
        units = units_lanes[:, 0].reshape(n_tiles, N_EXPERTS)
        max_units = n_rows * TOP_K // UNIT + n_tiles * N_EXPERTS + N_EXPERTS * (FFN_UNITS - 1)
        n_blocks = max_units // FFN_UNITS + 1
        gmap, tile_units, blk_expert, blk_valid, n_used = _moe_layout(units, n_blocks)
        xs = _dispatch(f, lrow_t, gmap, tile_units, n_blocks * FFN_ROWS)
        ys = _expert_ffn(xs, w_exp_gate, w_exp_up, w_exp_down, l, blk_expert, blk_valid, n_used)
        sgu = jnp.concatenate([w_sh_gate[l], w_sh_up[l]], axis=1).astype(BF16)
        h = _combine(ys, lrow, w_tok, f, sgu, w_sh_down[l].astype(BF16), h1, mods["g2"], ln2_g[l], ln2_b[l],
                     gmap, tile_units, tiles_per_batch=sub_per_batch, n_batch=n_batch, alpha=alpha)

    return h[:n_lat].reshape(n_batch, seq, d)


def kernel(x, c, ctx, c_ctx, w_mod, b_mod, w_in, s5_lam_re, s5_lam_im, s5_log_step, s5_b_re, s5_b_im, s5_c_re, s5_c_im, s5_d, gm_w_s, gm_b_s, da_lam, da_subln_g, w_glu_val, w_glu_gate, w_proj_gm, w_proj_da, w_out, ln1_g, ln1_b, ln2_g, ln2_b, w_router, b_router, w_exp_gate, w_exp_up, w_exp_down, w_sh_gate, w_sh_up, w_sh_down):
    return _forward(x, c, ctx, c_ctx, w_mod, b_mod, w_in, s5_lam_re, s5_lam_im, s5_log_step, s5_b_re, s5_b_im,
                    s5_c_re, s5_c_im, s5_d, gm_w_s, gm_b_s, da_lam, da_subln_g, w_glu_val, w_glu_gate,
                    w_proj_gm, w_proj_da, w_out, ln1_g, ln1_b, ln2_g, ln2_b, w_router, b_router,
                    w_exp_gate, w_exp_up, w_exp_down, w_sh_gate, w_sh_up, w_sh_down)
```
